```python
import math
import jax, jax.numpy as jnp
from jax import lax
import numpy as np

D_MODEL = 1024
BATCH = 16
SEQ = 4096
DEPTH = 1
DEC_BATCH = 8
DEC_SEQ = 16
PAST_LEN = 2048

CHUNK = 64
QBLOCK = 128
HEAD_DIM = 64
FOX_HEADS = 8
DIFF_HEADS = 4
DIFF_V_DIM = 2 * HEAD_DIM
ROT_DIM = HEAD_DIM // 4
ROPE_THETA = 500000.0
MEM_LEN = 256
MEM_HEADS = 4
MEM_HEAD_DIM = D_MODEL // MEM_HEADS
N_GROUPS = 4
EXPERTS_PER_GROUP = 8
N_EXPERTS = N_GROUPS * EXPERTS_PER_GROUP
TOP_K = 2
D_EXPERT = 512
MOE_BLOCK = 128
EPS = 1e-6
FORGET_BIAS_INIT = 3.0

FOX_W = FOX_HEADS * HEAD_DIM
DIFF_QK_W = DIFF_HEADS * 2 * HEAD_DIM
DIFF_V_W = DIFF_HEADS * DIFF_V_DIM
MIX_W = FOX_W + DIFF_V_W
IN_W = 3 * FOX_W + FOX_HEADS + 2 * DIFF_QK_W + DIFF_V_W
SPLITS = [FOX_W, 2 * FOX_W, 3 * FOX_W, 3 * FOX_W + FOX_HEADS,
          3 * FOX_W + FOX_HEADS + DIFF_QK_W, 3 * FOX_W + FOX_HEADS + 2 * DIFF_QK_W]

kernel_name = "streaming_fox_diffattn_hiermoe_step"

F32 = jnp.float32


def _rmsnorm(x, g):
    xf = x.astype(F32)
    y = xf * lax.rsqrt(jnp.mean(xf * xf, axis=-1, keepdims=True) + EPS)
    return (y * g.astype(F32)).astype(x.dtype)


def _partial_rope(x, pos):
    half = ROT_DIM // 2
    inv = jnp.power(ROPE_THETA, -jnp.arange(half, dtype=F32) / half)
    ang = pos.astype(F32)[:, None] * inv[None, :]
    ang = ang.reshape((pos.shape[0],) + (1,) * (x.ndim - 3) + (half,))
    cos, sin = jnp.cos(ang), jnp.sin(ang)
    xr = x[..., :ROT_DIM].astype(F32)
    x1, x2 = xr[..., :half], xr[..., half:]
    rot = jnp.concatenate([x1 * cos - x2 * sin, x2 * cos + x1 * sin], axis=-1).astype(x.dtype)
    return jnp.concatenate([rot, x[..., ROT_DIM:]], axis=-1)


def _sweep_queries(fn, q_args, T):
    if T <= QBLOCK:
        return fn(*q_args)
    nb = T // QBLOCK
    blocks = tuple(jnp.moveaxis(a.reshape((a.shape[0], nb, QBLOCK) + a.shape[2:]), 1, 0) for a in q_args)
    out = lax.map(lambda args: fn(*args), blocks)
    return jnp.moveaxis(out, 0, 1).reshape((out.shape[1], T) + out.shape[3:])


def _fox_attention(q, k, v, cq, ck, qpos, kpos):
    scale = HEAD_DIM ** -0.5
    ckT = jnp.swapaxes(ck, 1, 2)

    def block(qb, cqb, qposb):
        s = jnp.einsum('bqhd,bkhd->bhqk', qb, k, preferred_element_type=F32) * scale
        s = s + jnp.swapaxes(cqb, 1, 2)[..., :, None] - ckT[:, :, None, :]
        mask = kpos[None, :] <= qposb[0][:, None]
        p = jax.nn.softmax(jnp.where(mask, s, -jnp.inf), axis=-1)
        return jnp.einsum('bhqk,bkhd->bqhd', p.astype(v.dtype), v)

    return _sweep_queries(block, (q, cq, qpos[None, :]), q.shape[1])


def _diff_attention(q, k, v, lam, qpos, kpos):
    scale = HEAD_DIM ** -0.5

    def block(qb, qposb):
        s = jnp.einsum('bqhcd,bkhcd->bhcqk', qb, k, preferred_element_type=F32) * scale
        mask = (kpos[None, :] // CHUNK) <= (qposb[0][:, None] // CHUNK)
        p = jax.nn.softmax(jnp.where(mask, s, -jnp.inf), axis=-1)
        a = p[:, :, 0] - lam * p[:, :, 1]
        return jnp.einsum('bhqk,bkhe->bqhe', a.astype(v.dtype), v)

    return _sweep_queries(block, (q, qpos[None, :]), q.shape[1])


def _mem_attention(q, k, v):
    scale = MEM_HEAD_DIM ** -0.5

    def block(qb):
        s = jnp.einsum('bqhd,bkhd->bhqk', qb, k, preferred_element_type=F32) * scale
        p = jax.nn.softmax(s, axis=-1)
        return jnp.einsum('bhqk,bkhd->bqhd', p.astype(v.dtype), v)

    return _sweep_queries(block, (q,), q.shape[1])


def _memory_kv(mem, g_mem_kv, w_mk, w_mv):
    B = mem.shape[0]
    m = _rmsnorm(mem, g_mem_kv)
    k = (m @ w_mk).reshape(B, MEM_LEN, MEM_HEADS, MEM_HEAD_DIM)
    v = (m @ w_mv).reshape(B, MEM_LEN, MEM_HEADS, MEM_HEAD_DIM)
    return k, v


def _hier_moe(h, w_group, b_group, w_erouter, b_erouter, w1, w3, w2):
    B, T, D = h.shape
    N = B * T
    x = h.reshape(N, D)
    g_logits = (x @ w_group).astype(F32) + b_group.astype(F32)
    g_prob = jax.nn.softmax(g_logits, axis=-1)
    g_idx = jnp.argmax(g_logits, axis=-1)
    g_p = jnp.take_along_axis(g_prob, g_idx[:, None], axis=-1)
    e_logits = ((x @ w_erouter).astype(F32) + b_erouter.astype(F32)).reshape(N, N_GROUPS, EXPERTS_PER_GROUP)
    e_logits = jnp.take_along_axis(e_logits, g_idx[:, None, None], axis=1)[:, 0]
    top_p, top_i = lax.top_k(jax.nn.softmax(e_logits, axis=-1), TOP_K)
    gate = g_p * top_p / jnp.sum(top_p, axis=-1, keepdims=True)
    eid = (g_idx[:, None] * EXPERTS_PER_GROUP + top_i).reshape(-1).astype(jnp.int32)
    tok = jnp.repeat(jnp.arange(N, dtype=jnp.int32), TOP_K)
    wgt = gate.reshape(-1)
    order = jnp.argsort(eid)
    eid_s, tok_s, wgt_s = eid[order], tok[order], wgt[order]
    counts = jnp.bincount(eid, length=N_EXPERTS)
    padded = (counts + MOE_BLOCK - 1) // MOE_BLOCK * MOE_BLOCK
    start = jnp.cumsum(counts) - counts
    pend = jnp.cumsum(padded)
    pstart = pend - padded
    dest = pstart[eid_s] + jnp.arange(N * TOP_K, dtype=jnp.int32) - start[eid_s]
    n_rows = -(-(N * TOP_K) // MOE_BLOCK) * MOE_BLOCK + N_EXPERTS * MOE_BLOCK
    n_blocks = n_rows // MOE_BLOCK
    row_tok = jnp.full((n_rows,), N, dtype=jnp.int32).at[dest].set(tok_s)
    row_w = jnp.zeros((n_rows,), F32).at[dest].set(wgt_s)
    block_e = jnp.minimum(jnp.searchsorted(pend, jnp.arange(n_blocks) * MOE_BLOCK, side='right'),
                          N_EXPERTS - 1)
    xs = jnp.concatenate([x, jnp.zeros((1, D), x.dtype)], axis=0)[row_tok].reshape(n_blocks, MOE_BLOCK, D)

    def expert_block(args):
        xb, e = args
        return (jax.nn.silu(xb @ w1[e]) * (xb @ w3[e])) @ w2[e]

    ys = lax.map(expert_block, (xs, block_e)).reshape(n_rows, D)
    out = jnp.zeros((N + 1, D), F32).at[row_tok].add(ys.astype(F32) * row_w[:, None])[:N]
    return out.astype(h.dtype).reshape(B, T, D)


def _mixer(n, past, prm, lam_init):
    B, T, _ = n.shape
    P = 0 if past is None else past[0].shape[1]
    qpos = P + jnp.arange(T)
    kpos = jnp.arange(P + T)
    proj = n @ prm['w_in']
    fq, fk, fv, ff, dq, dk, dv = jnp.split(proj, SPLITS, axis=-1)
    fq = fq.reshape(B, T, FOX_HEADS, HEAD_DIM)
    fk = fk.reshape(B, T, FOX_HEADS, HEAD_DIM)
    fv = fv.reshape(B, T, FOX_HEADS, HEAD_DIM)
    logf = jax.nn.log_sigmoid(ff.astype(F32) + prm['b_forget'].astype(F32))
    dq = _partial_rope(dq.reshape(B, T, DIFF_HEADS, 2, HEAD_DIM), qpos)
    dk = _partial_rope(dk.reshape(B, T, DIFF_HEADS, 2, HEAD_DIM), qpos)
    dv = dv.reshape(B, T, DIFF_HEADS, DIFF_V_DIM)
    new = (fk, fv, logf, dk, dv)
    if past is None:
        a_fk, a_fv, a_logf, a_dk, a_dv = fk, fv, logf, dk, dv
    else:
        a_fk = jnp.concatenate([past[0], fk], axis=1)
        a_fv = jnp.concatenate([past[1], fv], axis=1)
        a_logf = jnp.concatenate([past[2].astype(F32), logf], axis=1)
        a_dk = jnp.concatenate([past[3], dk], axis=1)
        a_dv = jnp.concatenate([past[4], dv], axis=1)
    c = jnp.cumsum(a_logf, axis=1)
    fox_o = _fox_attention(fq, a_fk, a_fv, c[:, P:], c, qpos, kpos)
    lam = (jnp.exp(jnp.sum(prm['lam_q1'].astype(F32) * prm['lam_k1'].astype(F32)))
           - jnp.exp(jnp.sum(prm['lam_q2'].astype(F32) * prm['lam_k2'].astype(F32))) + lam_init)
    diff_o = _diff_attention(dq, a_dk, a_dv, lam, qpos, kpos)
    diff_o = _rmsnorm(diff_o, prm['g_diff']) * (1.0 - lam_init)
    mix = jnp.concatenate([fox_o.reshape(B, T, FOX_W), diff_o.reshape(B, T, DIFF_V_W)], axis=-1)
    return mix @ prm['w_o'], new


def _layer(x, past, mem_k, mem_v, prm, lam_init):
    B, T, D = x.shape
    a, new = _mixer(_rmsnorm(x, prm['g_mix']), past, prm, lam_init)
    h = x + a
    q = (_rmsnorm(h, prm['g_mem_q']) @ prm['w_mq']).reshape(B, T, MEM_HEADS, MEM_HEAD_DIM)
    h = h + _mem_attention(q, mem_k, mem_v).reshape(B, T, D) @ prm['w_mo']
    h = h + _hier_moe(_rmsnorm(h, prm['g_ffn']), prm['w_group'], prm['b_group'], prm['w_erouter'],
                      prm['b_erouter'], prm['w1'], prm['w3'], prm['w2'])
    return h, new


def setup_inputs(seed: int = 0) -> dict:
    key = jax.random.key(seed)
    ks = iter(jax.random.split(key, 48))

    def nrm(shape, scale):
        return scale * jax.random.normal(next(ks), shape, jnp.float32)

    L, D = DEPTH, D_MODEL
    return {
        'x_prompt': nrm((BATCH, SEQ, D), 1.0),
        'x_sample': nrm((DEC_BATCH, DEC_SEQ, D), 1.0),
        'cache_fox_k': nrm((L, DEC_BATCH, PAST_LEN, FOX_HEADS, HEAD_DIM), 1.0),
        'cache_fox_v': nrm((L, DEC_BATCH, PAST_LEN, FOX_HEADS, HEAD_DIM), 1.0),
        'cache_fox_logf': jax.nn.log_sigmoid(FORGET_BIAS_INIT + nrm((L, DEC_BATCH, PAST_LEN, FOX_HEADS), 1.0)),
        'cache_diff_k': nrm((L, DEC_BATCH, PAST_LEN, DIFF_HEADS, 2, HEAD_DIM), 1.0),
        'cache_diff_v': nrm((L, DEC_BATCH, PAST_LEN, DIFF_HEADS, DIFF_V_DIM), 1.0),
        'cache_mem_k': nrm((L, DEC_BATCH, MEM_LEN, MEM_HEADS, MEM_HEAD_DIM), 1.0),
        'cache_mem_v': nrm((L, DEC_BATCH, MEM_LEN, MEM_HEADS, MEM_HEAD_DIM), 1.0),
        'mem_prompt': nrm((BATCH, MEM_LEN, D), 1.0),
        'g_mix': 1.0 + nrm((L, D), 0.02),
        'w_in': nrm((L, D, IN_W), D ** -0.5),
        'b_forget': FORGET_BIAS_INIT + nrm((L, FOX_HEADS), 0.5),
        'lam_q1': nrm((L, HEAD_DIM), 0.1),
        'lam_k1': nrm((L, HEAD_DIM), 0.1),
        'lam_q2': nrm((L, HEAD_DIM), 0.1),
        'lam_k2': nrm((L, HEAD_DIM), 0.1),
        'g_diff': 1.0 + nrm((L, DIFF_V_DIM), 0.02),
        'w_o': nrm((L, MIX_W, D), MIX_W ** -0.5),
        'g_mem_q': 1.0 + nrm((L, D), 0.02),
        'g_mem_kv': 1.0 + nrm((L, D), 0.02),
        'w_mq': nrm((L, D, D), D ** -0.5),
        'w_mk': nrm((L, D, D), D ** -0.5),
        'w_mv': nrm((L, D, D), D ** -0.5),
        'w_mo': nrm((L, D, D), D ** -0.5),
        'g_ffn': 1.0 + nrm((L, D), 0.02),
        'w_group': nrm((L, D, N_GROUPS), D ** -0.5),
        'b_group': nrm((L, N_GROUPS), 0.01),
        'w_erouter': nrm((L, D, N_EXPERTS), D ** -0.5),
        'b_erouter': nrm((L, N_EXPERTS), 0.01),
        'w1': nrm((L, N_EXPERTS, D, D_EXPERT), D ** -0.5),
        'w3': nrm((L, N_EXPERTS, D, D_EXPERT), D ** -0.5),
        'w2': nrm((L, N_EXPERTS, D_EXPERT, D), D_EXPERT ** -0.5),
        'g_final': 1.0 + nrm((D,), 0.02),
    }


def reference(x_prompt, x_sample, cache_fox_k, cache_fox_v, cache_fox_logf, cache_diff_k, cache_diff_v,
              cache_mem_k, cache_mem_v, mem_prompt, g_mix, w_in, b_forget, lam_q1, lam_k1, lam_q2, lam_k2,
              g_diff, w_o, g_mem_q, g_mem_kv, w_mq, w_mk, w_mv, w_mo, g_ffn, w_group, b_group,
              w_erouter, b_erouter, w1, w3, w2, g_final):
    hp, hs = x_prompt, x_sample
    new_p, new_s, mem_p = [], [], []
    for l in range(DEPTH):
        prm = {
            'g_mix': g_mix[l], 'w_in': w_in[l], 'b_forget': b_forget[l],
            'lam_q1': lam_q1[l], 'lam_k1': lam_k1[l], 'lam_q2': lam_q2[l], 'lam_k2': lam_k2[l],
            'g_diff': g_diff[l], 'w_o': w_o[l], 'g_mem_q': g_mem_q[l], 'w_mq': w_mq[l], 'w_mo': w_mo[l],
            'g_ffn': g_ffn[l], 'w_group': w_group[l], 'b_group': b_group[l],
            'w_erouter': w_erouter[l], 'b_erouter': b_erouter[l], 'w1': w1[l], 'w3': w3[l], 'w2': w2[l],
        }
        lam_init = 0.8 - 0.6 * math.exp(-0.3 * l)
        mk, mv = _memory_kv(mem_prompt, g_mem_kv[l], w_mk[l], w_mv[l])
        hp, np_new = _layer(hp, None, mk, mv, prm, lam_init)
        past = (cache_fox_k[l], cache_fox_v[l], cache_fox_logf[l], cache_diff_k[l], cache_diff_v[l])
        hs, ns_new = _layer(hs, past, cache_mem_k[l], cache_mem_v[l], prm, lam_init)
        new_p.append(np_new)
        new_s.append(ns_new)
        mem_p.append((mk, mv))
    y_prompt = _rmsnorm(hp, g_final)
    y_sample = _rmsnorm(hs, g_final)
    p_fk = jnp.stack([e[0] for e in new_p])
    p_fv = jnp.stack([e[1] for e in new_p])
    p_logf = jnp.stack([e[2] for e in new_p])
    p_dk = jnp.stack([e[3] for e in new_p])
    p_dv = jnp.stack([e[4] for e in new_p])
    p_mk = jnp.stack([e[0] for e in mem_p])
    p_mv = jnp.stack([e[1] for e in mem_p])
    s_fk = jnp.stack([e[0] for e in new_s])
    s_fv = jnp.stack([e[1] for e in new_s])
    s_logf = jnp.stack([e[2] for e in new_s])
    s_dk = jnp.stack([e[3] for e in new_s])
    s_dv = jnp.stack([e[4] for e in new_s])
    return (y_prompt, y_sample, p_fk, p_fv, p_logf, p_dk, p_dv, p_mk, p_mv, s_fk, s_fv, s_logf, s_dk, s_dv)
```

```python
import functools
import math

import jax
import jax.numpy as jnp
from jax import lax
from jax.experimental import pallas as pl
from jax.experimental.pallas import tpu as pltpu

F32 = jnp.float32
BF16 = jnp.bfloat16
I32 = jnp.int32

D_MODEL = 1024
HEAD_DIM = 64
FOX_HEADS = 8
DIFF_HEADS = 4
FOX_W = FOX_HEADS * HEAD_DIM
DIFF_W = DIFF_HEADS * 2 * HEAD_DIM
ROT_DIM = HEAD_DIM // 4
ROPE_THETA = 500000.0
CHUNK = 64
MEM_LEN = 256
MEM_HEADS = 4
MEM_HEAD_DIM = D_MODEL // MEM_HEADS
N_GROUPS = 4
EXPERTS_PER_GROUP = 8
N_EXPERTS = N_GROUPS * EXPERTS_PER_GROUP
D_EXPERT = 512
EPS = 1e-6

LANES = 128
LOG2E = 1.4426950408889634
NEG = -1e30
VMEM_LIMIT_BYTES = 56 * 1024 * 1024
MOE_ROWS = 256
GROUP_LANE0 = N_EXPERTS


def _params(*sem):
    return pltpu.CompilerParams(dimension_semantics=sem, vmem_limit_bytes=VMEM_LIMIT_BYTES)


def _rms(x, g):
    return (x * lax.rsqrt(jnp.mean(x * x, axis=-1, keepdims=True) + EPS)) * g


def _rope(x, cos, sin_lo, sin_hi):
    outs = []
    for c in range(x.shape[1] // LANES):
        xc = x[:, c * LANES:(c + 1) * LANES]
        up = pltpu.roll(xc, LANES - ROT_DIM // 2, axis=1)
        dn = pltpu.roll(xc, ROT_DIM // 2, axis=1)
        outs.append(xc * cos + up * sin_lo + dn * sin_hi)
    return jnp.concatenate(outs, axis=1)


def _inproj_kernel(x_ref, g_ref, w_ref, wf_ref, bf_ref, cos_ref, slo_ref, shi_ref,
                   fk_ref, fv_ref, logf_ref, dk_ref, dv_ref,
                   qf_ref, kf_ref, vf_ref, qd_ref, kd_ref, vd_ref, *, q_scale):
    n = _rms(x_ref[...], g_ref[...]).astype(BF16)

    def proj(c):
        return jnp.dot(n, w_ref[:, c * FOX_W:(c + 1) * FOX_W], preferred_element_type=F32)

    cos, slo, shi = cos_ref[...], slo_ref[...], shi_ref[...]
    qf_ref[...] = (proj(0) * q_scale).astype(BF16)
    fk = proj(1)
    fk_ref[...] = fk
    kf_ref[...] = fk.astype(BF16)
    fv = proj(2)
    fv_ref[...] = fv
    vf_ref[...] = fv.astype(BF16)
    qd_ref[...] = (_rope(proj(3), cos, slo, shi) * q_scale).astype(BF16)
    dk = _rope(proj(4), cos, slo, shi)
    dk_ref[...] = dk
    kd_ref[...] = dk.astype(BF16)
    dv = proj(5)
    dv_ref[...] = dv
    vd_ref[...] = dv.astype(BF16)
    z = jnp.dot(n, wf_ref[...], preferred_element_type=F32)[:, :FOX_HEADS] + bf_ref[...]
    logf_ref[...] = jnp.minimum(z, 0.0) - jnp.log1p(jnp.exp(-jnp.abs(z)))


def _rope_tables(pos):
    half = ROT_DIM // 2
    inv = jnp.power(ROPE_THETA, -jnp.arange(half, dtype=F32) / half)
    ang = pos.astype(F32)[:, None] * inv[None, :]
    cos, sin = jnp.cos(ang), jnp.sin(ang)
    t = pos.shape[0]
    pad = jnp.zeros((t, HEAD_DIM - ROT_DIM), F32)
    zero = jnp.zeros((t, half), F32)
    cos64 = jnp.concatenate([cos, cos, pad + 1.0], axis=1)
    slo64 = jnp.concatenate([-sin, zero, pad], axis=1)
    shi64 = jnp.concatenate([zero, sin, pad], axis=1)
    return tuple(jnp.tile(a, (1, LANES // HEAD_DIM)) for a in (cos64, slo64, shi64))


def _inproj(x, g, w_main, w_ff, b_ff, pos, tm):
    b, t, d = x.shape
    n = b * t
    tables = _rope_tables(pos)
    if tm <= t:
        assert t % tm == 0
        per = t // tm
        tab_map = lambda i: (i % per, 0)
    else:
        assert tm % t == 0
        tables = tuple(jnp.tile(a, (tm // t, 1)) for a in tables)
        tab_map = lambda i: (0, 0)
    assert n % tm == 0
    row = lambda i: (i, 0)
    const = lambda i: (0, 0)
    wide = pl.BlockSpec((tm, FOX_W), row)
    tab = pl.BlockSpec((tm, LANES), tab_map)
    f32o = jax.ShapeDtypeStruct((n, FOX_W), F32)
    b16o = jax.ShapeDtypeStruct((n, FOX_W), BF16)
    return pl.pallas_call(
        functools.partial(_inproj_kernel, q_scale=HEAD_DIM ** -0.5 * LOG2E),
        grid=(n // tm,),
        in_specs=[pl.BlockSpec((tm, d), row), pl.BlockSpec((1, d), const),
                  pl.BlockSpec(w_main.shape, const), pl.BlockSpec(w_ff.shape, const),
                  pl.BlockSpec((1, FOX_HEADS), const), tab, tab, tab],
        out_specs=[wide, wide, pl.BlockSpec((tm, FOX_HEADS), row), wide, wide,
                   wide, wide, wide, wide, wide, wide],
        out_shape=[f32o, f32o, jax.ShapeDtypeStruct((n, FOX_HEADS), F32), f32o, f32o,
                   b16o, b16o, b16o, b16o, b16o, b16o],
        compiler_params=_params("arbitrary"),
    )(x.reshape(n, d), g.reshape(1, d), w_main, w_ff, b_ff.reshape(1, FOX_HEADS), *tables)


def _cumsum_kernel(x_ref, o_ref):
    x = x_ref[...]
    s_len = x.shape[1]
    lane = lax.broadcasted_iota(I32, x.shape, 1)
    shift = 1
    while shift < s_len:
        x = x + jnp.where(lane >= shift, pltpu.roll(x, shift, axis=1), 0.0)
        shift *= 2
    o_ref[...] = x * LOG2E


def _cumsum_time(logf_t):
    b, h, s = logf_t.shape
    spec = pl.BlockSpec((None, h, s), lambda i: (i, 0, 0))
    return pl.pallas_call(
        _cumsum_kernel, grid=(b,), in_specs=[spec], out_specs=spec,
        out_shape=jax.ShapeDtypeStruct((b, h, s), F32),
        compiler_params=_params("arbitrary"),
    )(logf_t)


def _softmax_step(t, m_ref, l_ref, slot):
    m_prev = m_ref[slot]
    m_new = jnp.maximum(m_prev, jnp.max(t, axis=1, keepdims=True))
    alpha = jnp.exp2(m_prev - m_new)
    p = jnp.exp2(t - jnp.concatenate([m_new] * (t.shape[1] // LANES), axis=1))
    l_ref[slot] = alpha * l_ref[slot] + jnp.sum(p, axis=1, keepdims=True)
    m_ref[slot] = m_new
    return p.astype(BF16), alpha


def _kv_loop(i, step, *, tq, tk, past, s_valid, span):
    q0 = past + i * tq
    last = jnp.minimum((q0 + tq + span - 1) // span * span, s_valid)
    n_vis = (last + tk - 1) // tk
    n_full = jnp.minimum((q0 + 1) // tk, n_vis)

    def plain(j, c):
        step(j, False)
        return c

    def masked(j, c):
        step(j, True)
        return c

    lax.fori_loop(0, n_full, plain, 0)
    lax.fori_loop(n_full, n_vis, masked, 0)


def _positions(i, j, tq, tk, past):
    qpos = past + i * tq + lax.broadcasted_iota(I32, (tq, tk), 0)
    kpos = j * tk + lax.broadcasted_iota(I32, (tq, tk), 1)
    return qpos, kpos


def _fox_kernel(q_ref, k_ref, v_ref, ck_ref, cq_ref, o_ref, m_ref, l_ref, acc_ref,
                *, tq, tk, past, s_valid):
    i = pl.program_id(2)
    q = q_ref[...]
    lane = lax.broadcasted_iota(I32, (tq, LANES), 1)
    low = lane < HEAD_DIM
    zero = jnp.zeros_like(q)
    q_head = (jnp.where(low, q, zero), jnp.where(low, zero, q))
    m_ref[...] = jnp.full(m_ref.shape, NEG, F32)
    l_ref[...] = jnp.zeros(l_ref.shape, F32)
    acc_ref[...] = jnp.zeros(acc_ref.shape, F32)

    def step(j, use_mask):
        ks = pl.multiple_of(j * tk, tk)
        k = k_ref[pl.ds(ks, tk), :]
        v = v_ref[pl.ds(ks, tk), :]
        pv, alpha = [], []
        for h in range(2):
            s = lax.dot_general(q_head[h], k, (((1,), (1,)), ((), ())), preferred_element_type=F32)
            t = (s + cq_ref[:, h:h + 1]) - ck_ref[h:h + 1, pl.ds(ks, tk)]
            if use_mask:
                qpos, kpos = _positions(i, j, tq, tk, past)
                t = jnp.where(kpos <= qpos, t, NEG)
            p, a = _softmax_step(t, m_ref, l_ref, h)
            pv.append(jnp.dot(p, v, preferred_element_type=F32))
            alpha.append(a)
        acc_ref[...] = acc_ref[...] * jnp.where(low, alpha[0], alpha[1]) + jnp.where(low, pv[0], pv[1])

    _kv_loop(i, step, tq=tq, tk=tk, past=past, s_valid=s_valid, span=1)
    o_ref[...] = (acc_ref[...] / jnp.where(low, l_ref[0], l_ref[1])).astype(o_ref.dtype)


def _fox_attention(q, k, v, ck, cq, *, b, t, s_pad, s_valid, past, tq, tk):
    pairs = FOX_W // LANES
    n_q = t // tq
    kv_spec = pl.BlockSpec((None, s_pad, LANES), lambda bi, p, i: (bi, 0, p))
    q_spec = pl.BlockSpec((tq, LANES), lambda bi, p, i: (bi * n_q + i, p))
    return pl.pallas_call(
        functools.partial(_fox_kernel, tq=tq, tk=tk, past=past, s_valid=s_valid),
        grid=(b, pairs, n_q),
        in_specs=[q_spec, kv_spec, kv_spec,
                  pl.BlockSpec((None, None, 2, s_pad), lambda bi, p, i: (bi, p, 0, 0)),
                  pl.BlockSpec((None, None, tq, 2), lambda bi, p, i: (bi, p, i, 0))],
        out_specs=q_spec,
        out_shape=jax.ShapeDtypeStruct((b * t, FOX_W), BF16),
        scratch_shapes=[pltpu.VMEM((2, tq, LANES), F32), pltpu.VMEM((2, tq, LANES), F32),
                        pltpu.VMEM((tq, LANES), F32)],
        compiler_params=_params("arbitrary", "arbitrary", "arbitrary"),
    )(q, k, v, ck, cq)


def _diff_kernel(q_ref, k_ref, v_ref, lam_ref, g_ref, o_ref, m_ref, l_ref, acc_ref,
                 *, tq, tk, past, s_valid, lam_init):
    i = pl.program_id(2)
    q = q_ref[...]
    lane = lax.broadcasted_iota(I32, (tq, LANES), 1)
    low = lane < HEAD_DIM
    zero = jnp.zeros_like(q)
    q_comp = (jnp.where(low, q, zero), jnp.where(low, zero, q))
    m_ref[...] = jnp.full(m_ref.shape, NEG, F32)
    l_ref[...] = jnp.zeros(l_ref.shape, F32)
    acc_ref[...] = jnp.zeros(acc_ref.shape, F32)

    def step(j, use_mask):
        ks = pl.multiple_of(j * tk, tk)
        k = k_ref[pl.ds(ks, tk), :]
        v = v_ref[pl.ds(ks, tk), :]
        for c in range(2):
            t = lax.dot_general(q_comp[c], k, (((1,), (1,)), ((), ())), preferred_element_type=F32)
            if use_mask:
                qpos, kpos = _positions(i, j, tq, tk, past)
                vis = jnp.minimum((qpos // CHUNK + 1) * CHUNK, s_valid)
                t = jnp.where(kpos < vis, t, NEG)
            p, a = _softmax_step(t, m_ref, l_ref, c)
            acc_ref[c] = acc_ref[c] * a + jnp.dot(p, v, preferred_element_type=F32)

    _kv_loop(i, step, tq=tq, tk=tk, past=past, s_valid=s_valid, span=CHUNK)
    lam_v = lam_ref[...]
    lam = (jnp.exp(jnp.sum(lam_v[0:1] * lam_v[1:2], axis=1, keepdims=True))
           - jnp.exp(jnp.sum(lam_v[2:3] * lam_v[3:4], axis=1, keepdims=True)) + lam_init)
    o = acc_ref[0] / l_ref[0] - lam * (acc_ref[1] / l_ref[1])
    o_ref[...] = (_rms(o, g_ref[...]) * (1.0 - lam_init)).astype(o_ref.dtype)


def _diff_attention(q, k, v, lam_vecs, g_diff, *, b, t, s_pad, s_valid, past, tq, tk, lam_init):
    n_q = t // tq
    kv_spec = pl.BlockSpec((None, s_pad, LANES), lambda bi, h, i: (bi, 0, h))
    q_spec = pl.BlockSpec((tq, LANES), lambda bi, h, i: (bi * n_q + i, h))
    const = lambda bi, h, i: (0, 0)
    return pl.pallas_call(
        functools.partial(_diff_kernel, tq=tq, tk=tk, past=past, s_valid=s_valid, lam_init=lam_init),
        grid=(b, DIFF_HEADS, n_q),
        in_specs=[q_spec, kv_spec, kv_spec,
                  pl.BlockSpec((4, HEAD_DIM), const), pl.BlockSpec((1, LANES), const)],
        out_specs=q_spec,
        out_shape=jax.ShapeDtypeStruct((b * t, DIFF_W), BF16),
        scratch_shapes=[pltpu.VMEM((2, tq, LANES), F32), pltpu.VMEM((2, tq, LANES), F32),
                        pltpu.VMEM((2, tq, LANES), F32)],
        compiler_params=_params("arbitrary", "arbitrary", "arbitrary"),
    )(q, k, v, lam_vecs, g_diff.reshape(1, LANES))


def _outproj_kernel(x_ref, fo_ref, do_ref, wof_ref, wod_ref, g_ref, wq_ref, h_ref, q_ref, *, q_scale):
    h = (x_ref[...] + jnp.dot(fo_ref[...], wof_ref[...], preferred_element_type=F32)
         + jnp.dot(do_ref[...], wod_ref[...], preferred_element_type=F32))
    h_ref[...] = h
    n = _rms(h, g_ref[...]).astype(BF16)
    q_ref[...] = (jnp.dot(n, wq_ref[...], preferred_element_type=F32) * q_scale).astype(BF16)


def _outproj(x2, fox_o, diff_o, wo_f, wo_d, g_mem_q, w_mq, tm):
    n, d = x2.shape
    row = lambda i: (i, 0)
    const = lambda i: (0, 0)
    return pl.pallas_call(
        functools.partial(_outproj_kernel, q_scale=MEM_HEAD_DIM ** -0.5 * LOG2E),
        grid=(n // tm,),
        in_specs=[pl.BlockSpec((tm, d), row), pl.BlockSpec((tm, FOX_W), row),
                  pl.BlockSpec((tm, DIFF_W), row), pl.BlockSpec((FOX_W, d), const),
                  pl.BlockSpec((DIFF_W, d), const), pl.BlockSpec((1, d), const),
                  pl.BlockSpec((d, d), const)],
        out_specs=[pl.BlockSpec((tm, d), row), pl.BlockSpec((tm, d), row)],
        out_shape=[jax.ShapeDtypeStruct((n, d), F32), jax.ShapeDtypeStruct((n, d), BF16)],
        compiler_params=_params("arbitrary"),
    )(x2, fox_o, diff_o, wo_f, wo_d, g_mem_q.reshape(1, d), w_mq)


def _memkv_kernel(m_ref, g_ref, wk_ref, wv_ref, k_ref, v_ref, kb_ref, vb_ref):
    n = _rms(m_ref[...], g_ref[...]).astype(BF16)
    k = jnp.dot(n, wk_ref[...], preferred_element_type=F32)
    v = jnp.dot(n, wv_ref[...], preferred_element_type=F32)
    k_ref[...] = k
    v_ref[...] = v
    kb_ref[...] = k.astype(BF16)
    vb_ref[...] = v.astype(BF16)


def _memkv(mem2, g, wk, wv, tm):
    n, d = mem2.shape
    row = lambda i: (i, 0)
    const = lambda i: (0, 0)
    blk = pl.BlockSpec((tm, d), row)
    wspec = pl.BlockSpec((d, d), const)
    return pl.pallas_call(
        _memkv_kernel, grid=(n // tm,),
        in_specs=[blk, pl.BlockSpec((1, d), const), wspec, wspec],
        out_specs=[blk, blk, blk, blk],
        out_shape=[jax.ShapeDtypeStruct((n, d), F32)] * 2 + [jax.ShapeDtypeStruct((n, d), BF16)] * 2,
        compiler_params=_params("arbitrary"),
    )(mem2, g.reshape(1, d), wk, wv)


ROW_TILE = D_MODEL // LANES


def _load_tile_rows(ref):
    rows = ref.shape[0] // ROW_TILE
    return jnp.concatenate([ref[pl.ds(s, rows, stride=ROW_TILE), :] for s in range(ROW_TILE)], axis=1)


def _store_tile_rows(ref, val):
    rows = val.shape[0]
    for s in range(ROW_TILE):
        ref[pl.ds(s, rows, stride=ROW_TILE), :] = val[:, s * LANES:(s + 1) * LANES]


def _first_index(hit, lane):
    return jnp.min(jnp.where(hit, lane.astype(F32), float(LANES)), axis=1, keepdims=True).astype(I32)


def _memattn_kernel(q_ref, k_ref, v_ref, h_ref, wo_ref, g_ref, wr_ref, br_ref,
                    h2_ref, xn_ref, ri_ref, rg_ref, cnt_ref, carry_ref, *, tq):
    @pl.when((pl.program_id(0) == 0) & (pl.program_id(1) == 0))
    def _():
        carry_ref[...] = jnp.zeros(carry_ref.shape, F32)

    q = q_ref[...]
    outs = []
    for h in range(MEM_HEADS):
        sl = slice(h * MEM_HEAD_DIM, (h + 1) * MEM_HEAD_DIM)
        s = lax.dot_general(q[:, sl], k_ref[:, sl], (((1,), (1,)), ((), ())), preferred_element_type=F32)
        p = jnp.exp2(s - jnp.max(s, axis=1, keepdims=True))
        o = jnp.dot(p.astype(BF16), v_ref[:, sl], preferred_element_type=F32)
        outs.append((o / jnp.sum(p, axis=1, keepdims=True)).astype(BF16))
    h2 = h_ref[...] + jnp.dot(jnp.concatenate(outs, axis=1), wo_ref[...], preferred_element_type=F32)
    h2_ref[...] = h2
    xn_f32 = _rms(h2, g_ref[...])
    _store_tile_rows(xn_ref, xn_f32)
    xn = xn_f32.astype(BF16)

    logits = jnp.dot(xn, wr_ref[...], preferred_element_type=F32) + br_ref[...]
    lane = lax.broadcasted_iota(I32, (tq, LANES), 1)
    g_log = jnp.where((lane >= GROUP_LANE0) & (lane < GROUP_LANE0 + N_GROUPS), logits, NEG)
    g_max = jnp.max(g_log, axis=1, keepdims=True)
    g_idx = _first_index(g_log == g_max, lane) - GROUP_LANE0
    g_p = 1.0 / jnp.sum(jnp.exp(g_log - g_max), axis=1, keepdims=True)
    in_group = (lane >= g_idx * EXPERTS_PER_GROUP) & (lane < (g_idx + 1) * EXPERTS_PER_GROUP)
    e_log = jnp.where(in_group, logits, NEG)
    e_exp = jnp.exp(e_log - jnp.max(e_log, axis=1, keepdims=True))
    prob = jnp.where(in_group, e_exp / jnp.sum(e_exp, axis=1, keepdims=True), -1.0)
    p1 = jnp.max(prob, axis=1, keepdims=True)
    i1 = _first_index(prob == p1, lane)
    rest = jnp.where(lane == i1, -1.0, prob)
    p2 = jnp.max(rest, axis=1, keepdims=True)
    i2 = _first_index(rest == p2, lane)
    top_sum = p1 + p2

    hot1, hot2 = lane == i1, lane == i2
    onehot = jnp.where(hot1 | hot2, 1.0, 0.0)
    r_i = lax.broadcasted_iota(I32, (tq, tq), 0)
    c_i = lax.broadcasted_iota(I32, (tq, tq), 1)
    before = jnp.dot(jnp.where(c_i < r_i, 1.0, 0.0).astype(BF16), onehot.astype(BF16),
                     preferred_element_type=F32) + carry_ref[0:1, :]
    r1 = jnp.sum(jnp.where(hot1, before, 0.0), axis=1, keepdims=True)
    r2 = jnp.sum(jnp.where(hot2, before, 0.0), axis=1, keepdims=True)
    carry_ref[...] = carry_ref[...] + jnp.sum(onehot, axis=0, keepdims=True)
    cnt_ref[...] = carry_ref[...]

    l8 = lax.broadcasted_iota(I32, (tq, 8), 1)
    ri_ref[...] = jnp.where(l8 == 0, i1, jnp.where(l8 == 1, i2, jnp.where(
        l8 == 2, r1.astype(I32), jnp.where(l8 == 3, r2.astype(I32), 0))))
    rg_ref[...] = jnp.where(l8 == 0, g_p * p1 / top_sum, jnp.where(l8 == 1, g_p * p2 / top_sum, 0.0))


def _memattn(qm, mk, mv, h1, w_mo, g_ffn, w_router, b_router, *, b, t, tq):
    n, d = h1.shape
    n_q = t // tq
    row = lambda bi, i: (bi * n_q + i, 0)
    const = lambda bi, i: (0, 0)
    mem = pl.BlockSpec((None, MEM_LEN, d), lambda bi, i: (bi, 0, 0))
    blk = pl.BlockSpec((tq, d), row)
    tiled = pl.BlockSpec((tq * ROW_TILE, LANES), row)
    small = pl.BlockSpec((tq, 8), row)
    return pl.pallas_call(
        functools.partial(_memattn_kernel, tq=tq),
        grid=(b, n_q),
        in_specs=[blk, mem, mem, blk, pl.BlockSpec((d, d), const), pl.BlockSpec((1, d), const),
                  pl.BlockSpec((d, LANES), const), pl.BlockSpec((1, LANES), const)],
        out_specs=[blk, tiled, small, small, pl.BlockSpec((8, LANES), const)],
        out_shape=[jax.ShapeDtypeStruct((n, d), F32), jax.ShapeDtypeStruct((n * ROW_TILE, LANES), F32),
                   jax.ShapeDtypeStruct((n, 8), I32), jax.ShapeDtypeStruct((n, 8), F32),
                   jax.ShapeDtypeStruct((8, LANES), F32)],
        scratch_shapes=[pltpu.VMEM((8, LANES), F32)],
        compiler_params=_params("arbitrary", "arbitrary"),
    )(qm, mk, mv, h1, w_mo, g_ffn.reshape(1, d), w_router, b_router)


def _row_copy(src_hbm, dst_hbm, src_row, dst_row, sem):
    src = src_hbm.at[pl.ds(pl.multiple_of(src_row * ROW_TILE, ROW_TILE), ROW_TILE)]
    dst = dst_hbm.at[pl.ds(pl.multiple_of(dst_row * ROW_TILE, ROW_TILE), ROW_TILE)]
    return pltpu.make_async_copy(src, dst, sem)


def _dispatch_kernel(pos_ref, src_hbm, dst_hbm, sem, *, tile):
    base = pl.program_id(0) * tile

    def issue(t, c):
        for k in range(2):
            _row_copy(src_hbm, dst_hbm, base + t, pos_ref[2 * t + k], sem).start()
        return c

    def drain(t, c):
        for k in range(2):
            _row_copy(src_hbm, dst_hbm, base + t, pos_ref[2 * t + k], sem).wait()
        return c

    lax.fori_loop(0, tile, issue, 0)
    lax.fori_loop(0, tile, drain, 0)


def _return_kernel(pos_ref, src_hbm, dst_hbm, sem, *, tile, n_tok):
    base = pl.program_id(0) * tile

    def issue(t, c):
        for k in range(2):
            _row_copy(src_hbm, dst_hbm, pos_ref[2 * t + k], k * n_tok + base + t, sem).start()
        return c

    def drain(t, c):
        for k in range(2):
            _row_copy(src_hbm, dst_hbm, pos_ref[2 * t + k], k * n_tok + base + t, sem).wait()
        return c

    lax.fori_loop(0, tile, issue, 0)
    lax.fori_loop(0, tile, drain, 0)


def _permute_rows(body, pos_flat, src, n_out, tile):
    n_tok = pos_flat.shape[0] // 2
    return pl.pallas_call(
        body, grid=(n_tok // tile,),
        in_specs=[pl.BlockSpec((2 * tile,), lambda i: (i,), memory_space=pltpu.SMEM),
                  pl.BlockSpec(memory_space=pl.ANY)],
        out_specs=pl.BlockSpec(memory_space=pl.ANY),
        out_shape=jax.ShapeDtypeStruct((n_out * ROW_TILE, LANES), src.dtype),
        scratch_shapes=[pltpu.SemaphoreType.DMA(())],
        compiler_params=pltpu.CompilerParams(dimension_semantics=("arbitrary",), has_side_effects=True),
    )(pos_flat, src)


def _expert_kernel(blk_ref, e_ref, lo_ref, hi_ref, x_ref, w1_ref, w3_ref, w2_ref, y_ref):
    w = pl.program_id(0)
    lo, hi = lo_ref[w], hi_ref[w]
    row0 = blk_ref[w] * MOE_ROWS

    def compute():
        x = _load_tile_rows(x_ref).astype(BF16)
        a = jnp.dot(x, w1_ref[...], preferred_element_type=F32)
        g = jnp.dot(x, w3_ref[...], preferred_element_type=F32)
        mid = ((a / (1.0 + jnp.exp(-a))) * g).astype(BF16)
        y = jnp.dot(mid, w2_ref[...], preferred_element_type=F32)
        rows = row0 + lax.broadcasted_iota(I32, y.shape, 0)
        return y, (rows >= lo) & (rows < hi)

    @pl.when((hi > lo) & (lo == row0))
    def _():
        y, mine = compute()
        _store_tile_rows(y_ref, jnp.where(mine, y, 0.0))

    @pl.when((hi > lo) & (lo != row0))
    def _():
        y, mine = compute()
        _store_tile_rows(y_ref, jnp.where(mine, y, _load_tile_rows(y_ref)))


def _experts(items, xs, w1, w3, w2):
    d = D_MODEL
    n_items = items[0].shape[0]
    xmap = lambda w, blk, e, lo, hi: (blk[w], 0)
    wmap = lambda w, blk, e, lo, hi: (e[w], 0, 0)
    return pl.pallas_call(
        _expert_kernel,
        grid_spec=pltpu.PrefetchScalarGridSpec(
            num_scalar_prefetch=4, grid=(n_items,),
            in_specs=[pl.BlockSpec((MOE_ROWS * ROW_TILE, LANES), xmap),
                      pl.BlockSpec((None, d, D_EXPERT), wmap), pl.BlockSpec((None, d, D_EXPERT), wmap),
                      pl.BlockSpec((None, D_EXPERT, d), wmap)],
            out_specs=pl.BlockSpec((MOE_ROWS * ROW_TILE, LANES), xmap)),
        out_shape=jax.ShapeDtypeStruct(xs.shape, F32),
        compiler_params=_params("arbitrary"),
    )(*items, xs, w1, w3, w2)


def _work_items(counts, n_rows):
    n_blocks = n_rows // MOE_ROWS
    n_items = n_blocks + N_EXPERTS - 1
    ends = jnp.cumsum(counts)
    starts = ends - counts
    first_blk = starts // MOE_ROWS
    n_blk = jnp.where(counts > 0, (ends - 1) // MOE_ROWS - first_blk + 1, 0)
    item_end = jnp.cumsum(n_blk)
    total = item_end[-1]
    w = jnp.arange(n_items, dtype=I32)
    wc = jnp.minimum(w, total - 1)
    e = jnp.searchsorted(item_end, wc, side='right').astype(I32)
    blk = (first_blk[e] + wc - (item_end[e] - n_blk[e])).astype(I32)
    lo = jnp.maximum(starts[e], blk * MOE_ROWS).astype(I32)
    hi = jnp.minimum(ends[e], (blk + 1) * MOE_ROWS).astype(I32)
    live = w < total
    return blk, e, jnp.where(live, lo, 0), jnp.where(live, hi, 0), starts


def _combine_kernel(h_ref, z0_ref, z1_ref, rg_ref, g_ref, y_ref):
    rg = rg_ref[...]
    moe = _load_tile_rows(z0_ref) * rg[:, 0:1] + _load_tile_rows(z1_ref) * rg[:, 1:2]
    y_ref[...] = _rms(h_ref[...] + moe, g_ref[...])


def _combine(h2, z, rg, g_final, tm):
    n, d = h2.shape
    n_t = n // tm
    row = lambda i: (i, 0)
    return pl.pallas_call(
        _combine_kernel, grid=(n_t,),
        in_specs=[pl.BlockSpec((tm, d), row), pl.BlockSpec((tm * ROW_TILE, LANES), row),
                  pl.BlockSpec((tm * ROW_TILE, LANES), lambda i: (n_t + i, 0)), pl.BlockSpec((tm, 8), row),
                  pl.BlockSpec((1, d), lambda i: (0, 0))],
        out_specs=pl.BlockSpec((tm, d), row),
        out_shape=jax.ShapeDtypeStruct((n, d), F32),
        compiler_params=_params("arbitrary"),
    )(h2, z, z, rg, g_final.reshape(1, d))


def _moe_and_final(h2, xn, ri, rg, counts, w1, w3, w2, g_final, *, tile, tm):
    n = h2.shape[0]
    blk, e, lo, hi, starts = _work_items(counts, 2 * n)
    pos = (starts[ri[:, 0:2]] + ri[:, 2:4]).astype(I32).reshape(2 * n)
    xs = _permute_rows(functools.partial(_dispatch_kernel, tile=tile), pos, xn, 2 * n, tile)
    ys = _experts((blk, e, lo, hi), xs, w1, w3, w2)
    z = _permute_rows(functools.partial(_return_kernel, tile=tile, n_tok=n), pos, ys, 2 * n, tile)
    return _combine(h2, z, rg, g_final, tm)


def _pad_time(a, s_pad):
    return jnp.pad(a, ((0, 0), (0, s_pad - a.shape[1])) + ((0, 0),) * (a.ndim - 2))


def _layer(x, past, mem_k, mem_v, wts, lam_init, g_final, *, tm, tq, tk, tile):
    b, t, d = x.shape
    n = b * t
    p_len = 0 if past is None else past[0].shape[1]
    s_valid = p_len + t
    s_pad = -(-s_valid // tk) * tk
    pos = p_len + jnp.arange(t)

    fk, fv, logf, dk, dv, qf, kf, vf, qd, kd, vd = _inproj(
        x, wts['g_mix'], wts['w_main'], wts['w_ff'], wts['b_forget'], pos, tm)
    new = (fk.reshape(b, t, FOX_HEADS, HEAD_DIM), fv.reshape(b, t, FOX_HEADS, HEAD_DIM),
           logf.reshape(b, t, FOX_HEADS), dk.reshape(b, t, DIFF_HEADS, 2, HEAD_DIM),
           dv.reshape(b, t, DIFF_HEADS, 2 * HEAD_DIM))

    def with_past(new_b16, old):
        cur = new_b16.reshape(b, t, FOX_W)
        if old is not None:
            cur = jnp.concatenate([old.reshape(b, p_len, FOX_W).astype(BF16), cur], axis=1)
        return _pad_time(cur, s_pad)

    logf_all = logf.reshape(b, t, FOX_HEADS)
    if past is not None:
        logf_all = jnp.concatenate([past[2].astype(F32), logf_all], axis=1)
    c2 = _cumsum_time(_pad_time(jnp.swapaxes(logf_all, 1, 2).reshape(b * FOX_HEADS, s_valid), s_pad)
                      .reshape(b, FOX_HEADS, s_pad))
    ck = c2.reshape(b, FOX_HEADS // 2, 2, s_pad)
    cq = jnp.swapaxes(ck[:, :, :, p_len:p_len + t], 2, 3)

    old = (None,) * 5 if past is None else past
    fox_o = _fox_attention(qf, with_past(kf, old[0]), with_past(vf, old[1]), ck, cq,
                           b=b, t=t, s_pad=s_pad, s_valid=s_valid, past=p_len, tq=tq, tk=tk)
    diff_o = _diff_attention(qd, with_past(kd, old[3]), with_past(vd, old[4]), wts['lam_vecs'],
                             wts['g_diff'], b=b, t=t, s_pad=s_pad, s_valid=s_valid, past=p_len,
                             tq=tq, tk=tk, lam_init=lam_init)
    h1, qm = _outproj(x.reshape(n, d), fox_o, diff_o, wts['wo_f'], wts['wo_d'], wts['g_mem_q'],
                      wts['w_mq'], tm)
    h2, xn, ri, rg, cnt = _memattn(qm, mem_k, mem_v, h1, wts['w_mo'], wts['g_ffn'], wts['w_router'],
                                   wts['b_router'], b=b, t=t, tq=tq)
    counts = cnt[0, :N_EXPERTS].astype(I32)
    y = _moe_and_final(h2, xn, ri, rg, counts, wts['w1'], wts['w3'], wts['w2'], g_final,
                       tile=tile, tm=tm)
    return y.reshape(b, t, d), new


def kernel(x_prompt, x_sample, cache_fox_k, cache_fox_v, cache_fox_logf, cache_diff_k, cache_diff_v, cache_mem_k, cache_mem_v, mem_prompt, g_mix, w_in, b_forget, lam_q1, lam_k1, lam_q2, lam_k2, g_diff, w_o, g_mem_q, g_mem_kv, w_mq, w_mk, w_mv, w_mo, g_ffn, w_group, b_group, w_erouter, b_erouter, w1, w3, w2, g_final):
    depth = w_in.shape[0]
    assert depth == 1
    l = 0
    lam_init = 0.8 - 0.6 * math.exp(-0.3 * l)
    d = D_MODEL
    w = w_in[l]
    ff0 = 3 * FOX_W
    w_router = jnp.zeros((d, LANES), F32).at[:, :N_EXPERTS].set(w_erouter[l])
    w_router = w_router.at[:, GROUP_LANE0:GROUP_LANE0 + N_GROUPS].set(w_group[l])
    b_router = jnp.zeros((1, LANES), F32).at[0, :N_EXPERTS].set(b_erouter[l])
    b_router = b_router.at[0, GROUP_LANE0:GROUP_LANE0 + N_GROUPS].set(b_group[l])
    wts = {
        'g_mix': g_mix[l],
        'w_main': jnp.concatenate([w[:, :ff0], w[:, ff0 + FOX_HEADS:]], axis=1).astype(BF16),
        'w_ff': jnp.pad(w[:, ff0:ff0 + FOX_HEADS], ((0, 0), (0, LANES - FOX_HEADS))).astype(BF16),
        'b_forget': b_forget[l],
        'lam_vecs': jnp.stack([lam_q1[l], lam_k1[l], lam_q2[l], lam_k2[l]]),
        'g_diff': g_diff[l],
        'wo_f': w_o[l][:FOX_W].astype(BF16), 'wo_d': w_o[l][FOX_W:].astype(BF16),
        'g_mem_q': g_mem_q[l], 'w_mq': w_mq[l].astype(BF16), 'w_mo': w_mo[l].astype(BF16),
        'g_ffn': g_ffn[l], 'w_router': w_router.astype(BF16), 'b_router': b_router,
        'w1': w1[l].astype(BF16), 'w3': w3[l].astype(BF16), 'w2': w2[l].astype(BF16),
    }

    bp, tp, _ = x_prompt.shape
    mk, mv, mk_b, mv_b = _memkv(mem_prompt.reshape(bp * MEM_LEN, d), g_mem_kv[l],
                                w_mk[l].astype(BF16), w_mv[l].astype(BF16), 512)
    yp, new_p = _layer(x_prompt, None, mk_b.reshape(bp, MEM_LEN, d), mv_b.reshape(bp, MEM_LEN, d),
                       wts, lam_init, g_final, tm=512, tq=512, tk=512, tile=512)

    bs, ts, _ = x_sample.shape
    past = (cache_fox_k[l], cache_fox_v[l], cache_fox_logf[l], cache_diff_k[l], cache_diff_v[l])
    ys, new_s = _layer(x_sample, past, cache_mem_k[l].reshape(bs, MEM_LEN, d).astype(BF16),
                       cache_mem_v[l].reshape(bs, MEM_LEN, d).astype(BF16),
                       wts, lam_init, g_final, tm=bs * ts, tq=ts, tk=512, tile=bs * ts)

    mem_shape = (1, bp, MEM_LEN, MEM_HEADS, MEM_HEAD_DIM)
    return (yp, ys) + tuple(a[None] for a in new_p) + (mk.reshape(mem_shape), mv.reshape(mem_shape)) \
        + tuple(a[None] for a in new_s)
```

```python
import functools
import math

import jax
import jax.numpy as jnp
from jax import lax
from jax.experimental import pallas as pl
from jax.experimental.pallas import tpu as pltpu
from jax.experimental.pallas import tpu_sc as plsc

F32 = jnp.float32
BF16 = jnp.bfloat16
I32 = jnp.int32

D_MODEL = 1024
HEAD_DIM = 64
FOX_HEADS = 8
DIFF_HEADS = 4
FOX_W = FOX_HEADS * HEAD_DIM
DIFF_W = DIFF_HEADS * 2 * HEAD_DIM
ROT_DIM = HEAD_DIM // 4
ROPE_THETA = 500000.0
CHUNK = 64
MEM_LEN = 256
MEM_HEADS = 4
MEM_HEAD_DIM = D_MODEL // MEM_HEADS
N_GROUPS = 4
EXPERTS_PER_GROUP = 8
N_EXPERTS = N_GROUPS * EXPERTS_PER_GROUP
D_EXPERT = 512
EPS = 1e-6

LANES = 128
LOG2E = 1.4426950408889634
NEG = -1e30
VMEM_LIMIT_BYTES = 56 * 1024 * 1024
MOE_ROWS = 256
GROUP_LANE0 = N_EXPERTS


def _params(*sem):
    return pltpu.CompilerParams(dimension_semantics=sem, vmem_limit_bytes=VMEM_LIMIT_BYTES)


def _rms(x, g):
    return (x * lax.rsqrt(jnp.mean(x * x, axis=-1, keepdims=True) + EPS)) * g


def _rope(x, cos, sin_lo, sin_hi):
    outs = []
    for c in range(x.shape[1] // LANES):
        xc = x[:, c * LANES:(c + 1) * LANES]
        up = pltpu.roll(xc, LANES - ROT_DIM // 2, axis=1)
        dn = pltpu.roll(xc, ROT_DIM // 2, axis=1)
        outs.append(xc * cos + up * sin_lo + dn * sin_hi)
    return jnp.concatenate(outs, axis=1)


def _inproj_kernel(x_ref, g_ref, w_ref, wf_ref, bf_ref, cos_ref, slo_ref, shi_ref,
                   fk_ref, fv_ref, logf_ref, dk_ref, dv_ref,
                   qf_ref, kf_ref, vf_ref, qd_ref, kd_ref, vd_ref, *, q_scale):
    n = _rms(x_ref[...], g_ref[...]).astype(BF16)

    def proj(c):
        return jnp.dot(n, w_ref[:, c * FOX_W:(c + 1) * FOX_W], preferred_element_type=F32)

    cos, slo, shi = cos_ref[...], slo_ref[...], shi_ref[...]
    qf_ref[...] = (proj(0) * q_scale).astype(BF16)
    fk = proj(1)
    fk_ref[...] = fk
    kf_ref[...] = fk.astype(BF16)
    fv = proj(2)
    fv_ref[...] = fv
    vf_ref[...] = fv.astype(BF16)
    qd_ref[...] = (_rope(proj(3), cos, slo, shi) * q_scale).astype(BF16)
    dk = _rope(proj(4), cos, slo, shi)
    dk_ref[...] = dk
    kd_ref[...] = dk.astype(BF16)
    dv = proj(5)
    dv_ref[...] = dv
    vd_ref[...] = dv.astype(BF16)
    z = jnp.dot(n, wf_ref[...], preferred_element_type=F32)[:, :FOX_HEADS] + bf_ref[...]
    logf_ref[...] = jnp.minimum(z, 0.0) - jnp.log1p(jnp.exp(-jnp.abs(z)))


def _rope_tables(pos):
    half = ROT_DIM // 2
    inv = jnp.power(ROPE_THETA, -jnp.arange(half, dtype=F32) / half)
    ang = pos.astype(F32)[:, None] * inv[None, :]
    cos, sin = jnp.cos(ang), jnp.sin(ang)
    t = pos.shape[0]
    pad = jnp.zeros((t, HEAD_DIM - ROT_DIM), F32)
    zero = jnp.zeros((t, half), F32)
    cos64 = jnp.concatenate([cos, cos, pad + 1.0], axis=1)
    slo64 = jnp.concatenate([-sin, zero, pad], axis=1)
    shi64 = jnp.concatenate([zero, sin, pad], axis=1)
    return tuple(jnp.tile(a, (1, LANES // HEAD_DIM)) for a in (cos64, slo64, shi64))


def _inproj(x, g, w_main, w_ff, b_ff, pos, tm):
    b, t, d = x.shape
    n = b * t
    tables = _rope_tables(pos)
    if tm <= t:
        assert t % tm == 0
        per = t // tm
        tab_map = lambda i: (i % per, 0)
    else:
        assert tm % t == 0
        tables = tuple(jnp.tile(a, (tm // t, 1)) for a in tables)
        tab_map = lambda i: (0, 0)
    assert n % tm == 0
    row = lambda i: (i, 0)
    const = lambda i: (0, 0)
    wide = pl.BlockSpec((tm, FOX_W), row)
    tab = pl.BlockSpec((tm, LANES), tab_map)
    f32o = jax.ShapeDtypeStruct((n, FOX_W), F32)
    b16o = jax.ShapeDtypeStruct((n, FOX_W), BF16)
    return pl.pallas_call(
        functools.partial(_inproj_kernel, q_scale=HEAD_DIM ** -0.5 * LOG2E),
        grid=(n // tm,),
        in_specs=[pl.BlockSpec((tm, d), row), pl.BlockSpec((1, d), const),
                  pl.BlockSpec(w_main.shape, const), pl.BlockSpec(w_ff.shape, const),
                  pl.BlockSpec((1, FOX_HEADS), const), tab, tab, tab],
        out_specs=[wide, wide, pl.BlockSpec((tm, FOX_HEADS), row), wide, wide,
                   wide, wide, wide, wide, wide, wide],
        out_shape=[f32o, f32o, jax.ShapeDtypeStruct((n, FOX_HEADS), F32), f32o, f32o,
                   b16o, b16o, b16o, b16o, b16o, b16o],
        compiler_params=_params("arbitrary"),
    )(x.reshape(n, d), g.reshape(1, d), w_main, w_ff, b_ff.reshape(1, FOX_HEADS), *tables)


def _cumsum_kernel(x_ref, o_ref):
    x = x_ref[...]
    s_len = x.shape[1]
    lane = lax.broadcasted_iota(I32, x.shape, 1)
    shift = 1
    while shift < s_len:
        x = x + jnp.where(lane >= shift, pltpu.roll(x, shift, axis=1), 0.0)
        shift *= 2
    o_ref[...] = x * LOG2E


def _cumsum_time(logf_t):
    b, h, s = logf_t.shape
    spec = pl.BlockSpec((None, h, s), lambda i: (i, 0, 0))
    return pl.pallas_call(
        _cumsum_kernel, grid=(b,), in_specs=[spec], out_specs=spec,
        out_shape=jax.ShapeDtypeStruct((b, h, s), F32),
        compiler_params=_params("arbitrary"),
    )(logf_t)


def _softmax_step(t, m_ref, l_ref, slot):
    m_prev = m_ref[slot]
    m_new = jnp.maximum(m_prev, jnp.max(t, axis=1, keepdims=True))
    alpha = jnp.exp2(m_prev - m_new)
    p = jnp.exp2(t - jnp.concatenate([m_new] * (t.shape[1] // LANES), axis=1))
    l_ref[slot] = alpha * l_ref[slot] + jnp.sum(p, axis=1, keepdims=True)
    m_ref[slot] = m_new
    return p.astype(BF16), alpha


def _kv_loop(i, step, *, tq, tk, past, s_valid, span):
    q0 = past + i * tq
    last = jnp.minimum((q0 + tq + span - 1) // span * span, s_valid)
    n_vis = (last + tk - 1) // tk
    n_full = jnp.minimum((q0 + 1) // tk, n_vis)

    def plain(j, c):
        step(j, False)
        return c

    def masked(j, c):
        step(j, True)
        return c

    lax.fori_loop(0, n_full, plain, 0)
    lax.fori_loop(n_full, n_vis, masked, 0)


def _positions(i, j, tq, tk, past):
    qpos = past + i * tq + lax.broadcasted_iota(I32, (tq, tk), 0)
    kpos = j * tk + lax.broadcasted_iota(I32, (tq, tk), 1)
    return qpos, kpos


def _fox_kernel(q_ref, k_ref, v_ref, ck_ref, cq_ref, o_ref, m_ref, l_ref, acc_ref,
                *, tq, tk, past, s_valid):
    i = pl.program_id(2)
    q = q_ref[...]
    lane = lax.broadcasted_iota(I32, (tq, LANES), 1)
    low = lane < HEAD_DIM
    zero = jnp.zeros_like(q)
    q_head = (jnp.where(low, q, zero), jnp.where(low, zero, q))
    m_ref[...] = jnp.full(m_ref.shape, NEG, F32)
    l_ref[...] = jnp.zeros(l_ref.shape, F32)
    acc_ref[...] = jnp.zeros(acc_ref.shape, F32)

    def step(j, use_mask):
        ks = pl.multiple_of(j * tk, tk)
        k = k_ref[pl.ds(ks, tk), :]
        v = v_ref[pl.ds(ks, tk), :]
        pv, alpha = [], []
        for h in range(2):
            s = lax.dot_general(q_head[h], k, (((1,), (1,)), ((), ())), preferred_element_type=F32)
            t = (s + cq_ref[:, h:h + 1]) - ck_ref[h:h + 1, pl.ds(ks, tk)]
            if use_mask:
                qpos, kpos = _positions(i, j, tq, tk, past)
                t = jnp.where(kpos <= qpos, t, NEG)
            p, a = _softmax_step(t, m_ref, l_ref, h)
            pv.append(jnp.dot(p, v, preferred_element_type=F32))
            alpha.append(a)
        acc_ref[...] = acc_ref[...] * jnp.where(low, alpha[0], alpha[1]) + jnp.where(low, pv[0], pv[1])

    _kv_loop(i, step, tq=tq, tk=tk, past=past, s_valid=s_valid, span=1)
    o_ref[...] = (acc_ref[...] / jnp.where(low, l_ref[0], l_ref[1])).astype(o_ref.dtype)


def _fox_attention(q, k, v, ck, cq, *, b, t, s_pad, s_valid, past, tq, tk):
    pairs = FOX_W // LANES
    n_q = t // tq
    kv_spec = pl.BlockSpec((None, s_pad, LANES), lambda bi, p, i: (bi, 0, p))
    q_spec = pl.BlockSpec((tq, LANES), lambda bi, p, i: (bi * n_q + i, p))
    return pl.pallas_call(
        functools.partial(_fox_kernel, tq=tq, tk=tk, past=past, s_valid=s_valid),
        grid=(b, pairs, n_q),
        in_specs=[q_spec, kv_spec, kv_spec,
                  pl.BlockSpec((None, None, 2, s_pad), lambda bi, p, i: (bi, p, 0, 0)),
                  pl.BlockSpec((None, None, tq, 2), lambda bi, p, i: (bi, p, i, 0))],
        out_specs=q_spec,
        out_shape=jax.ShapeDtypeStruct((b * t, FOX_W), BF16),
        scratch_shapes=[pltpu.VMEM((2, tq, LANES), F32), pltpu.VMEM((2, tq, LANES), F32),
                        pltpu.VMEM((tq, LANES), F32)],
        compiler_params=_params("arbitrary", "arbitrary", "arbitrary"),
    )(q, k, v, ck, cq)


def _diff_kernel(q_ref, k_ref, v_ref, lam_ref, g_ref, o_ref, m_ref, l_ref, acc_ref,
                 *, tq, tk, past, s_valid, lam_init):
    i = pl.program_id(2)
    q = q_ref[...]
    lane = lax.broadcasted_iota(I32, (tq, LANES), 1)
    low = lane < HEAD_DIM
    zero = jnp.zeros_like(q)
    q_comp = (jnp.where(low, q, zero), jnp.where(low, zero, q))
    m_ref[...] = jnp.full(m_ref.shape, NEG, F32)
    l_ref[...] = jnp.zeros(l_ref.shape, F32)
    acc_ref[...] = jnp.zeros(acc_ref.shape, F32)

    def step(j, use_mask):
        ks = pl.multiple_of(j * tk, tk)
        k = k_ref[pl.ds(ks, tk), :]
        v = v_ref[pl.ds(ks, tk), :]
        for c in range(2):
            t = lax.dot_general(q_comp[c], k, (((1,), (1,)), ((), ())), preferred_element_type=F32)
            if use_mask:
                qpos, kpos = _positions(i, j, tq, tk, past)
                vis = jnp.minimum((qpos // CHUNK + 1) * CHUNK, s_valid)
                t = jnp.where(kpos < vis, t, NEG)
            p, a = _softmax_step(t, m_ref, l_ref, c)
            acc_ref[c] = acc_ref[c] * a + jnp.dot(p, v, preferred_element_type=F32)

    _kv_loop(i, step, tq=tq, tk=tk, past=past, s_valid=s_valid, span=CHUNK)
    lam_v = lam_ref[...]
    lam = (jnp.exp(jnp.sum(lam_v[0:1] * lam_v[1:2], axis=1, keepdims=True))
           - jnp.exp(jnp.sum(lam_v[2:3] * lam_v[3:4], axis=1, keepdims=True)) + lam_init)
    o = acc_ref[0] / l_ref[0] - lam * (acc_ref[1] / l_ref[1])
    o_ref[...] = (_rms(o, g_ref[...]) * (1.0 - lam_init)).astype(o_ref.dtype)


def _diff_attention(q, k, v, lam_vecs, g_diff, *, b, t, s_pad, s_valid, past, tq, tk, lam_init):
    n_q = t // tq
    kv_spec = pl.BlockSpec((None, s_pad, LANES), lambda bi, h, i: (bi, 0, h))
    q_spec = pl.BlockSpec((tq, LANES), lambda bi, h, i: (bi * n_q + i, h))
    const = lambda bi, h, i: (0, 0)
    return pl.pallas_call(
        functools.partial(_diff_kernel, tq=tq, tk=tk, past=past, s_valid=s_valid, lam_init=lam_init),
        grid=(b, DIFF_HEADS, n_q),
        in_specs=[q_spec, kv_spec, kv_spec,
                  pl.BlockSpec((4, HEAD_DIM), const), pl.BlockSpec((1, LANES), const)],
        out_specs=q_spec,
        out_shape=jax.ShapeDtypeStruct((b * t, DIFF_W), BF16),
        scratch_shapes=[pltpu.VMEM((2, tq, LANES), F32), pltpu.VMEM((2, tq, LANES), F32),
                        pltpu.VMEM((2, tq, LANES), F32)],
        compiler_params=_params("arbitrary", "arbitrary", "arbitrary"),
    )(q, k, v, lam_vecs, g_diff.reshape(1, LANES))


def _outproj_kernel(x_ref, fo_ref, do_ref, wof_ref, wod_ref, g_ref, wq_ref, h_ref, q_ref, *, q_scale):
    h = (x_ref[...] + jnp.dot(fo_ref[...], wof_ref[...], preferred_element_type=F32)
         + jnp.dot(do_ref[...], wod_ref[...], preferred_element_type=F32))
    h_ref[...] = h
    n = _rms(h, g_ref[...]).astype(BF16)
    q_ref[...] = (jnp.dot(n, wq_ref[...], preferred_element_type=F32) * q_scale).astype(BF16)


def _outproj(x2, fox_o, diff_o, wo_f, wo_d, g_mem_q, w_mq, tm):
    n, d = x2.shape
    row = lambda i: (i, 0)
    const = lambda i: (0, 0)
    return pl.pallas_call(
        functools.partial(_outproj_kernel, q_scale=MEM_HEAD_DIM ** -0.5 * LOG2E),
        grid=(n // tm,),
        in_specs=[pl.BlockSpec((tm, d), row), pl.BlockSpec((tm, FOX_W), row),
                  pl.BlockSpec((tm, DIFF_W), row), pl.BlockSpec((FOX_W, d), const),
                  pl.BlockSpec((DIFF_W, d), const), pl.BlockSpec((1, d), const),
                  pl.BlockSpec((d, d), const)],
        out_specs=[pl.BlockSpec((tm, d), row), pl.BlockSpec((tm, d), row)],
        out_shape=[jax.ShapeDtypeStruct((n, d), F32), jax.ShapeDtypeStruct((n, d), BF16)],
        compiler_params=_params("arbitrary"),
    )(x2, fox_o, diff_o, wo_f, wo_d, g_mem_q.reshape(1, d), w_mq)


def _memkv_kernel(m_ref, g_ref, wk_ref, wv_ref, k_ref, v_ref, kb_ref, vb_ref):
    n = _rms(m_ref[...], g_ref[...]).astype(BF16)
    k = jnp.dot(n, wk_ref[...], preferred_element_type=F32)
    v = jnp.dot(n, wv_ref[...], preferred_element_type=F32)
    k_ref[...] = k
    v_ref[...] = v
    kb_ref[...] = k.astype(BF16)
    vb_ref[...] = v.astype(BF16)


def _memkv(mem2, g, wk, wv, tm):
    n, d = mem2.shape
    row = lambda i: (i, 0)
    const = lambda i: (0, 0)
    blk = pl.BlockSpec((tm, d), row)
    wspec = pl.BlockSpec((d, d), const)
    return pl.pallas_call(
        _memkv_kernel, grid=(n // tm,),
        in_specs=[blk, pl.BlockSpec((1, d), const), wspec, wspec],
        out_specs=[blk, blk, blk, blk],
        out_shape=[jax.ShapeDtypeStruct((n, d), F32)] * 2 + [jax.ShapeDtypeStruct((n, d), BF16)] * 2,
        compiler_params=_params("arbitrary"),
    )(mem2, g.reshape(1, d), wk, wv)


ROW_TILE = D_MODEL // LANES


def _load_tile_rows(ref):
    rows = ref.shape[0] // ROW_TILE
    return jnp.concatenate([ref[pl.ds(s, rows, stride=ROW_TILE), :] for s in range(ROW_TILE)], axis=1)


def _store_tile_rows(ref, val):
    rows = val.shape[0]
    for s in range(ROW_TILE):
        ref[pl.ds(s, rows, stride=ROW_TILE), :] = val[:, s * LANES:(s + 1) * LANES]


def _first_index(hit, lane):
    return jnp.min(jnp.where(hit, lane.astype(F32), float(LANES)), axis=1, keepdims=True).astype(I32)


def _memattn_kernel(q_ref, k_ref, v_ref, h_ref, wo_ref, g_ref, wr_ref, br_ref,
                    h2_ref, xn_ref, ri_ref, rg_ref, cnt_ref, carry_ref, *, tq):
    @pl.when((pl.program_id(0) == 0) & (pl.program_id(1) == 0))
    def _():
        carry_ref[...] = jnp.zeros(carry_ref.shape, F32)

    q = q_ref[...]
    outs = []
    for h in range(MEM_HEADS):
        sl = slice(h * MEM_HEAD_DIM, (h + 1) * MEM_HEAD_DIM)
        s = lax.dot_general(q[:, sl], k_ref[:, sl], (((1,), (1,)), ((), ())), preferred_element_type=F32)
        p = jnp.exp2(s - jnp.max(s, axis=1, keepdims=True))
        o = jnp.dot(p.astype(BF16), v_ref[:, sl], preferred_element_type=F32)
        outs.append((o / jnp.sum(p, axis=1, keepdims=True)).astype(BF16))
    h2 = h_ref[...] + jnp.dot(jnp.concatenate(outs, axis=1), wo_ref[...], preferred_element_type=F32)
    h2_ref[...] = h2
    xn_f32 = _rms(h2, g_ref[...])
    _store_tile_rows(xn_ref, xn_f32)
    xn = xn_f32.astype(BF16)

    logits = jnp.dot(xn, wr_ref[...], preferred_element_type=F32) + br_ref[...]
    lane = lax.broadcasted_iota(I32, (tq, LANES), 1)
    g_log = jnp.where((lane >= GROUP_LANE0) & (lane < GROUP_LANE0 + N_GROUPS), logits, NEG)
    g_max = jnp.max(g_log, axis=1, keepdims=True)
    g_idx = _first_index(g_log == g_max, lane) - GROUP_LANE0
    g_p = 1.0 / jnp.sum(jnp.exp(g_log - g_max), axis=1, keepdims=True)
    in_group = (lane >= g_idx * EXPERTS_PER_GROUP) & (lane < (g_idx + 1) * EXPERTS_PER_GROUP)
    e_log = jnp.where(in_group, logits, NEG)
    e_exp = jnp.exp(e_log - jnp.max(e_log, axis=1, keepdims=True))
    prob = jnp.where(in_group, e_exp / jnp.sum(e_exp, axis=1, keepdims=True), -1.0)
    p1 = jnp.max(prob, axis=1, keepdims=True)
    i1 = _first_index(prob == p1, lane)
    rest = jnp.where(lane == i1, -1.0, prob)
    p2 = jnp.max(rest, axis=1, keepdims=True)
    i2 = _first_index(rest == p2, lane)
    top_sum = p1 + p2

    hot1, hot2 = lane == i1, lane == i2
    onehot = jnp.where(hot1 | hot2, 1.0, 0.0)
    r_i = lax.broadcasted_iota(I32, (tq, tq), 0)
    c_i = lax.broadcasted_iota(I32, (tq, tq), 1)
    before = jnp.dot(jnp.where(c_i < r_i, 1.0, 0.0).astype(BF16), onehot.astype(BF16),
                     preferred_element_type=F32) + carry_ref[0:1, :]
    r1 = jnp.sum(jnp.where(hot1, before, 0.0), axis=1, keepdims=True)
    r2 = jnp.sum(jnp.where(hot2, before, 0.0), axis=1, keepdims=True)
    carry_ref[...] = carry_ref[...] + jnp.sum(onehot, axis=0, keepdims=True)
    cnt_ref[...] = carry_ref[...]

    l8 = lax.broadcasted_iota(I32, (tq, 8), 1)
    ri_ref[...] = jnp.where(l8 == 0, i1, jnp.where(l8 == 1, i2, jnp.where(
        l8 == 2, r1.astype(I32), jnp.where(l8 == 3, r2.astype(I32), 0))))
    rg_ref[...] = jnp.where(l8 == 0, g_p * p1 / top_sum, jnp.where(l8 == 1, g_p * p2 / top_sum, 0.0))


def _memattn(qm, mk, mv, h1, w_mo, g_ffn, w_router, b_router, *, b, t, tq):
    n, d = h1.shape
    n_q = t // tq
    row = lambda bi, i: (bi * n_q + i, 0)
    const = lambda bi, i: (0, 0)
    mem = pl.BlockSpec((None, MEM_LEN, d), lambda bi, i: (bi, 0, 0))
    blk = pl.BlockSpec((tq, d), row)
    tiled = pl.BlockSpec((tq * ROW_TILE, LANES), row)
    small = pl.BlockSpec((tq, 8), row)
    return pl.pallas_call(
        functools.partial(_memattn_kernel, tq=tq),
        grid=(b, n_q),
        in_specs=[blk, mem, mem, blk, pl.BlockSpec((d, d), const), pl.BlockSpec((1, d), const),
                  pl.BlockSpec((d, LANES), const), pl.BlockSpec((1, LANES), const)],
        out_specs=[blk, tiled, small, small, pl.BlockSpec((8, LANES), const)],
        out_shape=[jax.ShapeDtypeStruct((n, d), F32), jax.ShapeDtypeStruct((n * ROW_TILE, LANES), F32),
                   jax.ShapeDtypeStruct((n, 8), I32), jax.ShapeDtypeStruct((n, 8), F32),
                   jax.ShapeDtypeStruct((8, LANES), F32)],
        scratch_shapes=[pltpu.VMEM((8, LANES), F32)],
        compiler_params=_params("arbitrary", "arbitrary"),
    )(qm, mk, mv, h1, w_mo, g_ffn.reshape(1, d), w_router, b_router)


SC_CORES, SC_SUBCORES = 2, 16
SC_WINDOW = 128


def _sc_mesh():
    return plsc.VectorSubcoreMesh(core_axis_name="core", subcore_axis_name="subcore")


def _sc_split(n_windows):
    if n_windows % (SC_CORES * SC_SUBCORES) == 0:
        return ("core", "subcore")
    assert n_windows % SC_SUBCORES == 0
    return "subcore"


def _sc_scatter_rows(src, idx, n_out):
    rows = src.shape[0]
    win = SC_WINDOW
    n_src_blocks = rows // win

    @functools.partial(pl.kernel, out_type=jax.ShapeDtypeStruct((n_out, LANES), src.dtype),
                       mesh=_sc_mesh(), scratch_types=[])
    def scatter(x_hbm, i_hbm, o_hbm):
        def body(x_vmem, i_vmem):
            pltpu.sync_copy(x_vmem, o_hbm.at[i_vmem.at[0]])

        pltpu.emit_pipeline(
            body, grid=(n_out // win,),
            in_specs=[pl.BlockSpec((win, LANES), lambda i: (i % n_src_blocks, 0)),
                      pl.BlockSpec((1, win), lambda i: (0, i))],
            out_specs=[],
            core_axis_name=_sc_split(n_out // win), dimension_semantics=(pltpu.PARALLEL,),
        )(x_hbm, i_hbm)

    return scatter(src, idx.reshape(1, n_out))


def _sc_gather_rows(src, idx):
    n_out = idx.shape[0]
    win = SC_WINDOW

    @functools.partial(pl.kernel, out_type=jax.ShapeDtypeStruct((n_out, LANES), src.dtype),
                       mesh=_sc_mesh(), scratch_types=[])
    def gather(x_hbm, i_hbm, o_hbm):
        def body(i_vmem, o_vmem):
            pltpu.sync_copy(x_hbm.at[i_vmem.at[0]], o_vmem)

        pltpu.emit_pipeline(
            body, grid=(n_out // win,),
            in_specs=[pl.BlockSpec((1, win), lambda i: (0, i))],
            out_specs=[pl.BlockSpec((win, LANES), lambda i: (i, 0))],
            core_axis_name=_sc_split(n_out // win), dimension_semantics=(pltpu.PARALLEL,),
        )(i_hbm, o_hbm)

    return gather(src, idx.reshape(1, n_out))


def _expert_kernel(blk_ref, e_ref, lo_ref, hi_ref, x_ref, w1_ref, w3_ref, w2_ref, y_ref):
    w = pl.program_id(0)
    lo, hi = lo_ref[w], hi_ref[w]
    row0 = blk_ref[w] * MOE_ROWS

    def compute():
        x = _load_tile_rows(x_ref).astype(BF16)
        a = jnp.dot(x, w1_ref[...], preferred_element_type=F32)
        g = jnp.dot(x, w3_ref[...], preferred_element_type=F32)
        mid = ((a / (1.0 + jnp.exp(-a))) * g).astype(BF16)
        y = jnp.dot(mid, w2_ref[...], preferred_element_type=F32)
        rows = row0 + lax.broadcasted_iota(I32, y.shape, 0)
        return y, (rows >= lo) & (rows < hi)

    @pl.when((hi > lo) & (lo == row0))
    def _():
        y, mine = compute()
        _store_tile_rows(y_ref, jnp.where(mine, y, 0.0))

    @pl.when((hi > lo) & (lo != row0))
    def _():
        y, mine = compute()
        _store_tile_rows(y_ref, jnp.where(mine, y, _load_tile_rows(y_ref)))


def _experts(items, xs, w1, w3, w2):
    d = D_MODEL
    n_items = items[0].shape[0]
    xmap = lambda w, blk, e, lo, hi: (blk[w], 0)
    wmap = lambda w, blk, e, lo, hi: (e[w], 0, 0)
    return pl.pallas_call(
        _expert_kernel,
        grid_spec=pltpu.PrefetchScalarGridSpec(
            num_scalar_prefetch=4, grid=(n_items,),
            in_specs=[pl.BlockSpec((MOE_ROWS * ROW_TILE, LANES), xmap),
                      pl.BlockSpec((None, d, D_EXPERT), wmap), pl.BlockSpec((None, d, D_EXPERT), wmap),
                      pl.BlockSpec((None, D_EXPERT, d), wmap)],
            out_specs=pl.BlockSpec((MOE_ROWS * ROW_TILE, LANES), xmap)),
        out_shape=jax.ShapeDtypeStruct(xs.shape, F32),
        compiler_params=_params("arbitrary"),
    )(*items, xs, w1, w3, w2)


def _work_items(counts, n_rows):
    n_blocks = n_rows // MOE_ROWS
    n_items = n_blocks + N_EXPERTS - 1
    ends = jnp.cumsum(counts)
    starts = ends - counts
    first_blk = starts // MOE_ROWS
    n_blk = jnp.where(counts > 0, (ends - 1) // MOE_ROWS - first_blk + 1, 0)
    item_end = jnp.cumsum(n_blk)
    total = item_end[-1]
    w = jnp.arange(n_items, dtype=I32)
    wc = jnp.minimum(w, total - 1)
    e = jnp.searchsorted(item_end, wc, side='right').astype(I32)
    blk = (first_blk[e] + wc - (item_end[e] - n_blk[e])).astype(I32)
    lo = jnp.maximum(starts[e], blk * MOE_ROWS).astype(I32)
    hi = jnp.minimum(ends[e], (blk + 1) * MOE_ROWS).astype(I32)
    live = w < total
    return blk, e, jnp.where(live, lo, 0), jnp.where(live, hi, 0), starts


def _combine_kernel(h_ref, z0_ref, z1_ref, rg_ref, g_ref, y_ref):
    rg = rg_ref[...]
    moe = _load_tile_rows(z0_ref) * rg[:, 0:1] + _load_tile_rows(z1_ref) * rg[:, 1:2]
    y_ref[...] = _rms(h_ref[...] + moe, g_ref[...])


def _combine(h2, z, rg, g_final, tm):
    n, d = h2.shape
    n_t = n // tm
    row = lambda i: (i, 0)
    return pl.pallas_call(
        _combine_kernel, grid=(n_t,),
        in_specs=[pl.BlockSpec((tm, d), row), pl.BlockSpec((tm * ROW_TILE, LANES), row),
                  pl.BlockSpec((tm * ROW_TILE, LANES), lambda i: (n_t + i, 0)), pl.BlockSpec((tm, 8), row),
                  pl.BlockSpec((1, d), lambda i: (0, 0))],
        out_specs=pl.BlockSpec((tm, d), row),
        out_shape=jax.ShapeDtypeStruct((n, d), F32),
        compiler_params=_params("arbitrary"),
    )(h2, z, z, rg, g_final.reshape(1, d))


def _moe_and_final(h2, xn, ri, rg, counts, w1, w3, w2, g_final, *, tm):
    n = h2.shape[0]
    blk, e, lo, hi, starts = _work_items(counts, 2 * n)
    pos = (starts[ri[:, 0:2]] + ri[:, 2:4]).astype(I32)
    sub = (pos.T[:, :, None] * ROW_TILE + jnp.arange(ROW_TILE, dtype=I32)).reshape(2 * n * ROW_TILE)
    xs = _sc_scatter_rows(xn, sub, 2 * n * ROW_TILE)
    ys = _experts((blk, e, lo, hi), xs, w1, w3, w2)
    z = _sc_gather_rows(ys, sub)
    return _combine(h2, z, rg, g_final, tm)


def _pad_time(a, s_pad):
    return jnp.pad(a, ((0, 0), (0, s_pad - a.shape[1])) + ((0, 0),) * (a.ndim - 2))


def _layer(x, past, mem_k, mem_v, wts, lam_init, g_final, *, tm, tq, tk):
    b, t, d = x.shape
    n = b * t
    p_len = 0 if past is None else past[0].shape[1]
    s_valid = p_len + t
    s_pad = -(-s_valid // tk) * tk
    pos = p_len + jnp.arange(t)

    fk, fv, logf, dk, dv, qf, kf, vf, qd, kd, vd = _inproj(
        x, wts['g_mix'], wts['w_main'], wts['w_ff'], wts['b_forget'], pos, tm)
    new = (fk.reshape(b, t, FOX_HEADS, HEAD_DIM), fv.reshape(b, t, FOX_HEADS, HEAD_DIM),
           logf.reshape(b, t, FOX_HEADS), dk.reshape(b, t, DIFF_HEADS, 2, HEAD_DIM),
           dv.reshape(b, t, DIFF_HEADS, 2 * HEAD_DIM))

    def with_past(new_b16, old):
        cur = new_b16.reshape(b, t, FOX_W)
        if old is not None:
            cur = jnp.concatenate([old.reshape(b, p_len, FOX_W).astype(BF16), cur], axis=1)
        return _pad_time(cur, s_pad)

    logf_all = logf.reshape(b, t, FOX_HEADS)
    if past is not None:
        logf_all = jnp.concatenate([past[2].astype(F32), logf_all], axis=1)
    c2 = _cumsum_time(_pad_time(jnp.swapaxes(logf_all, 1, 2).reshape(b * FOX_HEADS, s_valid), s_pad)
                      .reshape(b, FOX_HEADS, s_pad))
    ck = c2.reshape(b, FOX_HEADS // 2, 2, s_pad)
    cq = jnp.swapaxes(ck[:, :, :, p_len:p_len + t], 2, 3)

    old = (None,) * 5 if past is None else past
    fox_o = _fox_attention(qf, with_past(kf, old[0]), with_past(vf, old[1]), ck, cq,
                           b=b, t=t, s_pad=s_pad, s_valid=s_valid, past=p_len, tq=tq, tk=tk)
    diff_o = _diff_attention(qd, with_past(kd, old[3]), with_past(vd, old[4]), wts['lam_vecs'],
                             wts['g_diff'], b=b, t=t, s_pad=s_pad, s_valid=s_valid, past=p_len,
                             tq=tq, tk=tk, lam_init=lam_init)
    h1, qm = _outproj(x.reshape(n, d), fox_o, diff_o, wts['wo_f'], wts['wo_d'], wts['g_mem_q'],
                      wts['w_mq'], tm)
    h2, xn, ri, rg, cnt = _memattn(qm, mem_k, mem_v, h1, wts['w_mo'], wts['g_ffn'], wts['w_router'],
                                   wts['b_router'], b=b, t=t, tq=tq)
    counts = cnt[0, :N_EXPERTS].astype(I32)
    y = _moe_and_final(h2, xn, ri, rg, counts, wts['w1'], wts['w3'], wts['w2'], g_final, tm=tm)
    return y.reshape(b, t, d), new


def kernel(x_prompt, x_sample, cache_fox_k, cache_fox_v, cache_fox_logf, cache_diff_k, cache_diff_v, cache_mem_k, cache_mem_v, mem_prompt, g_mix, w_in, b_forget, lam_q1, lam_k1, lam_q2, lam_k2, g_diff, w_o, g_mem_q, g_mem_kv, w_mq, w_mk, w_mv, w_mo, g_ffn, w_group, b_group, w_erouter, b_erouter, w1, w3, w2, g_final):
    depth = w_in.shape[0]
    assert depth == 1
    l = 0
    lam_init = 0.8 - 0.6 * math.exp(-0.3 * l)
    d = D_MODEL
    w = w_in[l]
    ff0 = 3 * FOX_W
    w_router = jnp.zeros((d, LANES), F32).at[:, :N_EXPERTS].set(w_erouter[l])
    w_router = w_router.at[:, GROUP_LANE0:GROUP_LANE0 + N_GROUPS].set(w_group[l])
    b_router = jnp.zeros((1, LANES), F32).at[0, :N_EXPERTS].set(b_erouter[l])
    b_router = b_router.at[0, GROUP_LANE0:GROUP_LANE0 + N_GROUPS].set(b_group[l])
    wts = {
        'g_mix': g_mix[l],
        'w_main': jnp.concatenate([w[:, :ff0], w[:, ff0 + FOX_HEADS:]], axis=1).astype(BF16),
        'w_ff': jnp.pad(w[:, ff0:ff0 + FOX_HEADS], ((0, 0), (0, LANES - FOX_HEADS))).astype(BF16),
        'b_forget': b_forget[l],
        'lam_vecs': jnp.stack([lam_q1[l], lam_k1[l], lam_q2[l], lam_k2[l]]),
        'g_diff': g_diff[l],
        'wo_f': w_o[l][:FOX_W].astype(BF16), 'wo_d': w_o[l][FOX_W:].astype(BF16),
        'g_mem_q': g_mem_q[l], 'w_mq': w_mq[l].astype(BF16), 'w_mo': w_mo[l].astype(BF16),
        'g_ffn': g_ffn[l], 'w_router': w_router.astype(BF16), 'b_router': b_router,
        'w1': w1[l].astype(BF16), 'w3': w3[l].astype(BF16), 'w2': w2[l].astype(BF16),
    }

    bp, tp, _ = x_prompt.shape
    mk, mv, mk_b, mv_b = _memkv(mem_prompt.reshape(bp * MEM_LEN, d), g_mem_kv[l],
                                w_mk[l].astype(BF16), w_mv[l].astype(BF16), 512)
    yp, new_p = _layer(x_prompt, None, mk_b.reshape(bp, MEM_LEN, d), mv_b.reshape(bp, MEM_LEN, d),
                       wts, lam_init, g_final, tm=512, tq=512, tk=512)

    bs, ts, _ = x_sample.shape
    past = (cache_fox_k[l], cache_fox_v[l], cache_fox_logf[l], cache_diff_k[l], cache_diff_v[l])
    ys, new_s = _layer(x_sample, past, cache_mem_k[l].reshape(bs, MEM_LEN, d).astype(BF16),
                       cache_mem_v[l].reshape(bs, MEM_LEN, d).astype(BF16),
                       wts, lam_init, g_final, tm=bs * ts, tq=ts, tk=512)

    mem_shape = (1, bp, MEM_LEN, MEM_HEADS, MEM_HEAD_DIM)
    return (yp, ys) + tuple(a[None] for a in new_p) + (mk.reshape(mem_shape), mv.reshape(mem_shape)) \
        + tuple(a[None] for a in new_s)
```

```python
import functools
import math

import jax
import jax.numpy as jnp
from jax import lax
from jax.experimental import pallas as pl
from jax.experimental.pallas import tpu as pltpu
from jax.experimental.pallas import tpu_sc as plsc

F32 = jnp.float32
BF16 = jnp.bfloat16
I32 = jnp.int32

D_MODEL = 1024
HEAD_DIM = 64
FOX_HEADS = 8
DIFF_HEADS = 4
FOX_W = FOX_HEADS * HEAD_DIM
DIFF_W = DIFF_HEADS * 2 * HEAD_DIM
ROT_DIM = HEAD_DIM // 4
ROPE_THETA = 500000.0
CHUNK = 64
MEM_LEN = 256
MEM_HEADS = 4
MEM_HEAD_DIM = D_MODEL // MEM_HEADS
N_GROUPS = 4
EXPERTS_PER_GROUP = 8
N_EXPERTS = N_GROUPS * EXPERTS_PER_GROUP
D_EXPERT = 512
EPS = 1e-6

LANES = 128
LOG2E = 1.4426950408889634
NEG = -1e30
VMEM_LIMIT_BYTES = 56 * 1024 * 1024
MOE_ROWS = 256
GROUP_LANE0 = N_EXPERTS


def _params(*sem):
    return pltpu.CompilerParams(dimension_semantics=sem, vmem_limit_bytes=VMEM_LIMIT_BYTES)


def _rms(x, g):
    return (x * lax.rsqrt(jnp.mean(x * x, axis=-1, keepdims=True) + EPS)) * g


def _rope(x, cos, sin_lo, sin_hi):
    outs = []
    for c in range(x.shape[1] // LANES):
        xc = x[:, c * LANES:(c + 1) * LANES]
        up = pltpu.roll(xc, LANES - ROT_DIM // 2, axis=1)
        dn = pltpu.roll(xc, ROT_DIM // 2, axis=1)
        outs.append(xc * cos + up * sin_lo + dn * sin_hi)
    return jnp.concatenate(outs, axis=1)


def _inproj_kernel(x_ref, g_ref, w_ref, wf_ref, bf_ref, cos_ref, slo_ref, shi_ref,
                   fk_ref, fv_ref, logf_ref, dk_ref, dv_ref,
                   qf_ref, kf_ref, vf_ref, qd_ref, kd_ref, vd_ref, *, q_scale):
    n = _rms(x_ref[...], g_ref[...]).astype(BF16)

    def proj(c):
        return jnp.dot(n, w_ref[:, c * FOX_W:(c + 1) * FOX_W], preferred_element_type=F32)

    cos, slo, shi = cos_ref[...], slo_ref[...], shi_ref[...]
    qf_ref[...] = (proj(0) * q_scale).astype(BF16)
    fk = proj(1)
    fk_ref[...] = fk
    kf_ref[...] = fk.astype(BF16)
    fv = proj(2)
    fv_ref[...] = fv
    vf_ref[...] = fv.astype(BF16)
    qd_ref[...] = (_rope(proj(3), cos, slo, shi) * q_scale).astype(BF16)
    dk = _rope(proj(4), cos, slo, shi)
    dk_ref[...] = dk
    kd_ref[...] = dk.astype(BF16)
    dv = proj(5)
    dv_ref[...] = dv
    vd_ref[...] = dv.astype(BF16)
    z = jnp.dot(n, wf_ref[...], preferred_element_type=F32)[:, :FOX_HEADS] + bf_ref[...]
    logf_ref[...] = jnp.minimum(z, 0.0) - jnp.log1p(jnp.exp(-jnp.abs(z)))


def _rope_tables(pos):
    half = ROT_DIM // 2
    inv = jnp.power(ROPE_THETA, -jnp.arange(half, dtype=F32) / half)
    ang = pos.astype(F32)[:, None] * inv[None, :]
    cos, sin = jnp.cos(ang), jnp.sin(ang)
    t = pos.shape[0]
    pad = jnp.zeros((t, HEAD_DIM - ROT_DIM), F32)
    zero = jnp.zeros((t, half), F32)
    cos64 = jnp.concatenate([cos, cos, pad + 1.0], axis=1)
    slo64 = jnp.concatenate([-sin, zero, pad], axis=1)
    shi64 = jnp.concatenate([zero, sin, pad], axis=1)
    return tuple(jnp.tile(a, (1, LANES // HEAD_DIM)) for a in (cos64, slo64, shi64))


def _inproj(x, g, w_main, w_ff, b_ff, pos, tm):
    b, t, d = x.shape
    n = b * t
    tables = _rope_tables(pos)
    if tm <= t:
        assert t % tm == 0
        per = t // tm
        tab_map = lambda i: (i % per, 0)
    else:
        assert tm % t == 0
        tables = tuple(jnp.tile(a, (tm // t, 1)) for a in tables)
        tab_map = lambda i: (0, 0)
    assert n % tm == 0
    row = lambda i: (i, 0)
    const = lambda i: (0, 0)
    wide = pl.BlockSpec((tm, FOX_W), row)
    tab = pl.BlockSpec((tm, LANES), tab_map)
    f32o = jax.ShapeDtypeStruct((n, FOX_W), F32)
    b16o = jax.ShapeDtypeStruct((n, FOX_W), BF16)
    return pl.pallas_call(
        functools.partial(_inproj_kernel, q_scale=HEAD_DIM ** -0.5 * LOG2E),
        grid=(n // tm,),
        in_specs=[pl.BlockSpec((tm, d), row), pl.BlockSpec((1, d), const),
                  pl.BlockSpec(w_main.shape, const), pl.BlockSpec(w_ff.shape, const),
                  pl.BlockSpec((1, FOX_HEADS), const), tab, tab, tab],
        out_specs=[wide, wide, pl.BlockSpec((tm, FOX_HEADS), row), wide, wide,
                   wide, wide, wide, wide, wide, wide],
        out_shape=[f32o, f32o, jax.ShapeDtypeStruct((n, FOX_HEADS), F32), f32o, f32o,
                   b16o, b16o, b16o, b16o, b16o, b16o],
        compiler_params=_params("arbitrary"),
    )(x.reshape(n, d), g.reshape(1, d), w_main, w_ff, b_ff.reshape(1, FOX_HEADS), *tables)


def _cumsum_kernel(x_ref, o_ref):
    x = x_ref[...]
    s_len = x.shape[1]
    lane = lax.broadcasted_iota(I32, x.shape, 1)
    shift = 1
    while shift < s_len:
        x = x + jnp.where(lane >= shift, pltpu.roll(x, shift, axis=1), 0.0)
        shift *= 2
    o_ref[...] = x * LOG2E


def _cumsum_time(logf_t):
    b, h, s = logf_t.shape
    spec = pl.BlockSpec((None, h, s), lambda i: (i, 0, 0))
    return pl.pallas_call(
        _cumsum_kernel, grid=(b,), in_specs=[spec], out_specs=spec,
        out_shape=jax.ShapeDtypeStruct((b, h, s), F32),
        compiler_params=_params("arbitrary"),
    )(logf_t)


AUX_LANES = 8
MXU_DEPTH = 256


def _bf16_part(x):
    bits = lax.bitcast_convert_type(x, jnp.uint32) & jnp.uint32(0xFFFF0000)
    return lax.bitcast_convert_type(bits, F32)


def _split3(x):
    hi = _bf16_part(x)
    mid = _bf16_part(x - hi)
    lo = _bf16_part(x - hi - mid)
    return hi.astype(BF16), mid.astype(BF16), lo.astype(BF16)


def _fox_bias_lanes(c_time_major, q_side):
    parts = _split3(c_time_major if q_side else -c_time_major)
    ones = jnp.ones_like(parts[0])
    zeros = jnp.zeros_like(parts[0])
    lanes = (parts + (ones,) * 3) if q_side else ((ones,) * 3 + parts)
    aux = jnp.stack(lanes + (zeros, zeros), axis=-1)
    b, s = aux.shape[:2]
    return jnp.pad(aux.reshape(b, s, FOX_HEADS * AUX_LANES), ((0, 0), (0, 0), (0, LANES - FOX_HEADS * AUX_LANES)))


def _attn_kernel(*refs, tq, tk, past, s_valid, fox, lam_init):
    if fox:
        q_ref, qa_ref, k_ref, ka_ref, v_ref, o_ref, sa_ref, sb_ref, mx_ref, m_ref, l_ref, acc_ref = refs
    else:
        q_ref, k_ref, v_ref, lam_ref, g_ref, o_ref, sa_ref, sb_ref, mx_ref, m_ref, l_ref, acc_ref = refs
    i = pl.program_id(2)
    q = q_ref[...]
    lane = lax.broadcasted_iota(I32, (tq, LANES), 1)
    low = lane < HEAD_DIM
    zero = jnp.zeros_like(q)
    q_part = [jnp.where(low, q, zero), jnp.where(low, zero, q)]
    if fox:
        qa = qa_ref[...]
        for h in range(2):
            lane0 = (2 * pl.program_id(1) + h) * AUX_LANES
            aux = jnp.where((lane >= lane0) & (lane < lane0 + AUX_LANES), qa, zero)
            q_part[h] = jnp.concatenate([q_part[h], aux], axis=1)
    m_ref[...] = jnp.full(m_ref.shape, NEG, F32)
    l_ref[...] = jnp.zeros(l_ref.shape, F32)
    acc_ref[...] = jnp.zeros(acc_ref.shape, F32)
    q0 = past + i * tq
    nb = q0 // tk
    bufs = (sa_ref, sb_ref)

    def scores(j, buf):
        ks = pl.multiple_of(j * tk, tk)
        k = k_ref[pl.ds(ks, tk), :]
        if fox:
            k = jnp.concatenate([k, ka_ref[pl.ds(ks, tk), :]], axis=1)
        for h in range(2):
            s = lax.dot_general(q_part[h], k, (((1,), (1,)), ((), ())), preferred_element_type=F32)
            bufs[buf][h] = s
            mx_ref[buf, h] = jnp.broadcast_to(jnp.max(s, axis=1, keepdims=True), (tq, LANES))

    def consume(j, buf, use_mask):
        ks = pl.multiple_of(j * tk, tk)
        v = v_ref[pl.ds(ks, tk), :]
        pv, alpha = [], []
        for h in range(2):
            t = bufs[buf][h]
            if use_mask:
                qpos = q0 + lax.broadcasted_iota(I32, (tq, tk), 0)
                kpos = j * tk + lax.broadcasted_iota(I32, (tq, tk), 1)
                if fox:
                    seen = kpos <= qpos
                else:
                    seen = kpos < jnp.minimum((qpos // CHUNK + 1) * CHUNK, s_valid)
                t = jnp.where(seen, t, NEG)
                mx = jnp.max(t, axis=1, keepdims=True)
            else:
                mx = mx_ref[buf, h]
            m_prev = m_ref[h]
            m_new = jnp.maximum(m_prev, mx)
            a = jnp.exp2(m_prev - m_new)
            p = jnp.exp2(t - jnp.concatenate([m_new] * (tk // LANES), axis=1))
            l_ref[h] = a * l_ref[h] + jnp.sum(p, axis=1, keepdims=True)
            m_ref[h] = m_new
            pv.append(jnp.dot(p.astype(BF16), v, preferred_element_type=F32))
            alpha.append(a)
        if fox:
            acc_ref[0] = acc_ref[0] * jnp.where(low, alpha[0], alpha[1]) + jnp.where(low, pv[0], pv[1])
        else:
            for h in range(2):
                acc_ref[h] = acc_ref[h] * alpha[h] + pv[h]

    scores(0, 0)

    def pair(jj, c):
        j = 2 * jj
        scores(j + 1, 1)
        consume(j, 0, False)
        scores(j + 2, 0)
        consume(j + 1, 1, False)
        return c

    lax.fori_loop(0, nb // 2, pair, 0)

    @pl.when(nb % 2 == 1)
    def _():
        scores(nb, 1)
        consume(nb - 1, 0, False)
        consume(nb, 1, True)

    @pl.when(nb % 2 == 0)
    def _():
        consume(nb, 0, True)

    if fox:
        o = acc_ref[0] / jnp.where(low, l_ref[0], l_ref[1])
    else:
        lam_v = lam_ref[...]
        lam = (jnp.exp(jnp.sum(lam_v[0:1] * lam_v[1:2], axis=1, keepdims=True))
               - jnp.exp(jnp.sum(lam_v[2:3] * lam_v[3:4], axis=1, keepdims=True)) + lam_init)
        o = acc_ref[0] / l_ref[0] - lam * (acc_ref[1] / l_ref[1])
        o = _rms(o, g_ref[...]) * (1.0 - lam_init)
    o_ref[...] = o.astype(o_ref.dtype)


def _self_attention(q, k, v, extra, *, b, t, s_pad, s_valid, past, tq, tk, fox, lam_init):
    assert tk % tq == 0 and past % tq == 0 and t % tq == 0 and tk % CHUNK == 0 and s_pad % tk == 0
    assert tq % CHUNK == 0 or (CHUNK % tq == 0 and past % CHUNK == 0)
    n_q = t // tq
    kv_spec = pl.BlockSpec((None, s_pad, LANES), lambda bi, p, i: (bi, 0, p))
    q_spec = pl.BlockSpec((tq, LANES), lambda bi, p, i: (bi * n_q + i, p))
    const = lambda bi, p, i: (0, 0)
    if fox:
        in_specs = [q_spec, pl.BlockSpec((tq, LANES), lambda bi, p, i: (bi * n_q + i, 0)), kv_spec,
                    pl.BlockSpec((None, s_pad, LANES), lambda bi, p, i: (bi, 0, 0)), kv_spec]
        args = (q, extra[0], k, extra[1], v)
    else:
        in_specs = [q_spec, kv_spec, kv_spec, pl.BlockSpec((4, HEAD_DIM), const), pl.BlockSpec((1, LANES), const)]
        args = (q, k, v, extra[0], extra[1].reshape(1, LANES))
    stat = pltpu.VMEM((2, tq, LANES), F32)
    return pl.pallas_call(
        functools.partial(_attn_kernel, tq=tq, tk=tk, past=past, s_valid=s_valid, fox=fox, lam_init=lam_init),
        grid=(b, FOX_W // LANES, n_q),
        in_specs=in_specs,
        out_specs=q_spec,
        out_shape=jax.ShapeDtypeStruct((b * t, FOX_W), BF16),
        scratch_shapes=[pltpu.VMEM((2, tq, tk), F32), pltpu.VMEM((2, tq, tk), F32),
                        pltpu.VMEM((2, 2, tq, LANES), F32), stat, stat, stat],
        compiler_params=_params("arbitrary", "arbitrary", "arbitrary"),
    )(*args)


def _outproj_kernel(x_ref, fo_ref, do_ref, wof_ref, wod_ref, g_ref, wq_ref, h_ref, q_ref, *, q_scale):
    h = (x_ref[...] + jnp.dot(fo_ref[...], wof_ref[...], preferred_element_type=F32)
         + jnp.dot(do_ref[...], wod_ref[...], preferred_element_type=F32))
    h_ref[...] = h
    n = _rms(h, g_ref[...]).astype(BF16)
    q_ref[...] = (jnp.dot(n, wq_ref[...], preferred_element_type=F32) * q_scale).astype(BF16)


def _outproj(x2, fox_o, diff_o, wo_f, wo_d, g_mem_q, w_mq, tm):
    n, d = x2.shape
    row = lambda i: (i, 0)
    const = lambda i: (0, 0)
    return pl.pallas_call(
        functools.partial(_outproj_kernel, q_scale=MEM_HEAD_DIM ** -0.5 * LOG2E),
        grid=(n // tm,),
        in_specs=[pl.BlockSpec((tm, d), row), pl.BlockSpec((tm, FOX_W), row),
                  pl.BlockSpec((tm, DIFF_W), row), pl.BlockSpec((FOX_W, d), const),
                  pl.BlockSpec((DIFF_W, d), const), pl.BlockSpec((1, d), const),
                  pl.BlockSpec((d, d), const)],
        out_specs=[pl.BlockSpec((tm, d), row), pl.BlockSpec((tm, d), row)],
        out_shape=[jax.ShapeDtypeStruct((n, d), F32), jax.ShapeDtypeStruct((n, d), BF16)],
        compiler_params=_params("arbitrary"),
    )(x2, fox_o, diff_o, wo_f, wo_d, g_mem_q.reshape(1, d), w_mq)


def _memkv_kernel(m_ref, g_ref, wk_ref, wv_ref, k_ref, v_ref, kb_ref, vb_ref):
    n = _rms(m_ref[...], g_ref[...]).astype(BF16)
    k = jnp.dot(n, wk_ref[...], preferred_element_type=F32)
    v = jnp.dot(n, wv_ref[...], preferred_element_type=F32)
    k_ref[...] = k
    v_ref[...] = v
    kb_ref[...] = k.astype(BF16)
    vb_ref[...] = v.astype(BF16)


def _memkv(mem2, g, wk, wv, tm):
    n, d = mem2.shape
    row = lambda i: (i, 0)
    const = lambda i: (0, 0)
    blk = pl.BlockSpec((tm, d), row)
    wspec = pl.BlockSpec((d, d), const)
    return pl.pallas_call(
        _memkv_kernel, grid=(n // tm,),
        in_specs=[blk, pl.BlockSpec((1, d), const), wspec, wspec],
        out_specs=[blk, blk, blk, blk],
        out_shape=[jax.ShapeDtypeStruct((n, d), F32)] * 2 + [jax.ShapeDtypeStruct((n, d), BF16)] * 2,
        compiler_params=_params("arbitrary"),
    )(mem2, g.reshape(1, d), wk, wv)


ROW_TILE = D_MODEL // LANES


def _load_tile_rows(ref):
    rows = ref.shape[0] // ROW_TILE
    return jnp.concatenate([ref[pl.ds(s, rows, stride=ROW_TILE), :] for s in range(ROW_TILE)], axis=1)


def _store_tile_rows(ref, val):
    rows = val.shape[0]
    for s in range(ROW_TILE):
        ref[pl.ds(s, rows, stride=ROW_TILE), :] = val[:, s * LANES:(s + 1) * LANES]


def _first_index(hit, lane):
    return jnp.min(jnp.where(hit, lane.astype(F32), float(LANES)), axis=1, keepdims=True).astype(I32)


def _memattn_kernel(q_ref, k_ref, v_ref, h_ref, wo_ref, g_ref, wr_ref, br_ref,
                    h2_ref, xn_ref, ri_ref, rg_ref, cnt_ref, carry_ref, *, tq):
    @pl.when((pl.program_id(0) == 0) & (pl.program_id(1) == 0))
    def _():
        carry_ref[...] = jnp.zeros(carry_ref.shape, F32)

    q = q_ref[...]
    outs = []
    for h in range(MEM_HEADS):
        sl = slice(h * MEM_HEAD_DIM, (h + 1) * MEM_HEAD_DIM)
        s = lax.dot_general(q[:, sl], k_ref[:, sl], (((1,), (1,)), ((), ())), preferred_element_type=F32)
        p = jnp.exp2(s - jnp.max(s, axis=1, keepdims=True))
        o = jnp.dot(p.astype(BF16), v_ref[:, sl], preferred_element_type=F32)
        outs.append((o / jnp.sum(p, axis=1, keepdims=True)).astype(BF16))
    h2 = h_ref[...] + jnp.dot(jnp.concatenate(outs, axis=1), wo_ref[...], preferred_element_type=F32)
    h2_ref[...] = h2
    xn_f32 = _rms(h2, g_ref[...])
    _store_tile_rows(xn_ref, xn_f32)
    xn = xn_f32.astype(BF16)

    logits = jnp.dot(xn, wr_ref[...], preferred_element_type=F32) + br_ref[...]
    lane = lax.broadcasted_iota(I32, (tq, LANES), 1)
    g_log = jnp.where((lane >= GROUP_LANE0) & (lane < GROUP_LANE0 + N_GROUPS), logits, NEG)
    g_max = jnp.max(g_log, axis=1, keepdims=True)
    g_idx = _first_index(g_log == g_max, lane) - GROUP_LANE0
    g_p = 1.0 / jnp.sum(jnp.exp(g_log - g_max), axis=1, keepdims=True)
    in_group = (lane >= g_idx * EXPERTS_PER_GROUP) & (lane < (g_idx + 1) * EXPERTS_PER_GROUP)
    e_log = jnp.where(in_group, logits, NEG)
    e_exp = jnp.exp(e_log - jnp.max(e_log, axis=1, keepdims=True))
    prob = jnp.where(in_group, e_exp / jnp.sum(e_exp, axis=1, keepdims=True), -1.0)
    p1 = jnp.max(prob, axis=1, keepdims=True)
    i1 = _first_index(prob == p1, lane)
    rest = jnp.where(lane == i1, -1.0, prob)
    p2 = jnp.max(rest, axis=1, keepdims=True)
    i2 = _first_index(rest == p2, lane)
    top_sum = p1 + p2

    hot1, hot2 = lane == i1, lane == i2
    onehot = jnp.where(hot1 | hot2, 1.0, 0.0)
    r_i = lax.broadcasted_iota(I32, (tq, tq), 0)
    c_i = lax.broadcasted_iota(I32, (tq, tq), 1)
    before = jnp.dot(jnp.where(c_i < r_i, 1.0, 0.0).astype(BF16), onehot.astype(BF16),
                     preferred_element_type=F32) + carry_ref[0:1, :]
    r1 = jnp.sum(jnp.where(hot1, before, 0.0), axis=1, keepdims=True)
    r2 = jnp.sum(jnp.where(hot2, before, 0.0), axis=1, keepdims=True)
    carry_ref[...] = carry_ref[...] + jnp.sum(onehot, axis=0, keepdims=True)
    cnt_ref[...] = carry_ref[...]

    l8 = lax.broadcasted_iota(I32, (tq, 8), 1)
    ri_ref[...] = jnp.where(l8 == 0, i1, jnp.where(l8 == 1, i2, jnp.where(
        l8 == 2, r1.astype(I32), jnp.where(l8 == 3, r2.astype(I32), 0))))
    rg_ref[...] = jnp.where(l8 == 0, g_p * p1 / top_sum, jnp.where(l8 == 1, g_p * p2 / top_sum, 0.0))


def _memattn(qm, mk, mv, h1, w_mo, g_ffn, w_router, b_router, *, b, t, tq):
    n, d = h1.shape
    n_q = t // tq
    row = lambda bi, i: (bi * n_q + i, 0)
    const = lambda bi, i: (0, 0)
    mem = pl.BlockSpec((None, MEM_LEN, d), lambda bi, i: (bi, 0, 0))
    blk = pl.BlockSpec((tq, d), row)
    tiled = pl.BlockSpec((tq * ROW_TILE, LANES), row)
    small = pl.BlockSpec((tq, 8), row)
    return pl.pallas_call(
        functools.partial(_memattn_kernel, tq=tq),
        grid=(b, n_q),
        in_specs=[blk, mem, mem, blk, pl.BlockSpec((d, d), const), pl.BlockSpec((1, d), const),
                  pl.BlockSpec((d, LANES), const), pl.BlockSpec((1, LANES), const)],
        out_specs=[blk, tiled, small, small, pl.BlockSpec((8, LANES), const)],
        out_shape=[jax.ShapeDtypeStruct((n, d), F32), jax.ShapeDtypeStruct((n * ROW_TILE, LANES), F32),
                   jax.ShapeDtypeStruct((n, 8), I32), jax.ShapeDtypeStruct((n, 8), F32),
                   jax.ShapeDtypeStruct((8, LANES), F32)],
        scratch_shapes=[pltpu.VMEM((8, LANES), F32)],
        compiler_params=_params("arbitrary", "arbitrary"),
    )(qm, mk, mv, h1, w_mo, g_ffn.reshape(1, d), w_router, b_router)


SC_CORES, SC_SUBCORES = 2, 16
SC_WINDOW = 128


def _sc_mesh():
    return plsc.VectorSubcoreMesh(core_axis_name="core", subcore_axis_name="subcore")


def _sc_split(n_windows):
    if n_windows % (SC_CORES * SC_SUBCORES) == 0:
        return ("core", "subcore")
    assert n_windows % SC_SUBCORES == 0
    return "subcore"


def _sc_scatter_rows(src, idx, n_out):
    rows = src.shape[0]
    win = SC_WINDOW
    n_src_blocks = rows // win

    @functools.partial(pl.kernel, out_type=jax.ShapeDtypeStruct((n_out, LANES), src.dtype),
                       mesh=_sc_mesh(), scratch_types=[])
    def scatter(x_hbm, i_hbm, o_hbm):
        def body(x_vmem, i_vmem):
            pltpu.sync_copy(x_vmem, o_hbm.at[i_vmem.at[0]])

        pltpu.emit_pipeline(
            body, grid=(n_out // win,),
            in_specs=[pl.BlockSpec((win, LANES), lambda i: (i % n_src_blocks, 0)),
                      pl.BlockSpec((1, win), lambda i: (0, i))],
            out_specs=[],
            core_axis_name=_sc_split(n_out // win), dimension_semantics=(pltpu.PARALLEL,),
        )(x_hbm, i_hbm)

    return scatter(src, idx.reshape(1, n_out))


def _sc_gather_rows(src, idx):
    n_out = idx.shape[0]
    win = SC_WINDOW

    @functools.partial(pl.kernel, out_type=jax.ShapeDtypeStruct((n_out, LANES), src.dtype),
                       mesh=_sc_mesh(), scratch_types=[])
    def gather(x_hbm, i_hbm, o_hbm):
        def body(i_vmem, o_vmem):
            pltpu.sync_copy(x_hbm.at[i_vmem.at[0]], o_vmem)

        pltpu.emit_pipeline(
            body, grid=(n_out // win,),
            in_specs=[pl.BlockSpec((1, win), lambda i: (0, i))],
            out_specs=[pl.BlockSpec((win, LANES), lambda i: (i, 0))],
            core_axis_name=_sc_split(n_out // win), dimension_semantics=(pltpu.PARALLEL,),
        )(i_hbm, o_hbm)

    return gather(src, idx.reshape(1, n_out))


def _expert_kernel(blk_ref, e_ref, lo_ref, hi_ref, x_ref, w1_ref, w3_ref, w2_ref, y_ref):
    w = pl.program_id(0)
    lo, hi = lo_ref[w], hi_ref[w]
    row0 = blk_ref[w] * MOE_ROWS

    def compute():
        x = _load_tile_rows(x_ref).astype(BF16)
        a = jnp.dot(x, w1_ref[...], preferred_element_type=F32)
        g = jnp.dot(x, w3_ref[...], preferred_element_type=F32)
        mid = ((a / (1.0 + jnp.exp(-a))) * g).astype(BF16)
        y = jnp.dot(mid, w2_ref[...], preferred_element_type=F32)
        rows = row0 + lax.broadcasted_iota(I32, y.shape, 0)
        return y, (rows >= lo) & (rows < hi)

    @pl.when((hi > lo) & (lo == row0))
    def _():
        y, mine = compute()
        _store_tile_rows(y_ref, jnp.where(mine, y, 0.0))

    @pl.when((hi > lo) & (lo != row0))
    def _():
        y, mine = compute()
        _store_tile_rows(y_ref, jnp.where(mine, y, _load_tile_rows(y_ref)))


def _experts(items, xs, w1, w3, w2):
    d = D_MODEL
    n_items = items[0].shape[0]
    xmap = lambda w, blk, e, lo, hi: (blk[w], 0)
    wmap = lambda w, blk, e, lo, hi: (e[w], 0, 0)
    return pl.pallas_call(
        _expert_kernel,
        grid_spec=pltpu.PrefetchScalarGridSpec(
            num_scalar_prefetch=4, grid=(n_items,),
            in_specs=[pl.BlockSpec((MOE_ROWS * ROW_TILE, LANES), xmap),
                      pl.BlockSpec((None, d, D_EXPERT), wmap), pl.BlockSpec((None, d, D_EXPERT), wmap),
                      pl.BlockSpec((None, D_EXPERT, d), wmap)],
            out_specs=pl.BlockSpec((MOE_ROWS * ROW_TILE, LANES), xmap)),
        out_shape=jax.ShapeDtypeStruct(xs.shape, F32),
        compiler_params=_params("arbitrary"),
    )(*items, xs, w1, w3, w2)


def _work_items(counts, n_rows):
    n_blocks = n_rows // MOE_ROWS
    n_items = n_blocks + N_EXPERTS - 1
    ends = jnp.cumsum(counts)
    starts = ends - counts
    first_blk = starts // MOE_ROWS
    n_blk = jnp.where(counts > 0, (ends - 1) // MOE_ROWS - first_blk + 1, 0)
    item_end = jnp.cumsum(n_blk)
    total = item_end[-1]
    w = jnp.arange(n_items, dtype=I32)
    wc = jnp.minimum(w, total - 1)
    e = jnp.sum(item_end[None, :] <= wc[:, None], axis=1).astype(I32)
    blk = (first_blk[e] + wc - (item_end[e] - n_blk[e])).astype(I32)
    lo = jnp.maximum(starts[e], blk * MOE_ROWS).astype(I32)
    hi = jnp.minimum(ends[e], (blk + 1) * MOE_ROWS).astype(I32)
    live = w < total
    return blk, e, jnp.where(live, lo, 0), jnp.where(live, hi, 0), starts


def _combine_kernel(h_ref, z0_ref, z1_ref, rg_ref, g_ref, y_ref):
    rg = rg_ref[...]
    moe = _load_tile_rows(z0_ref) * rg[:, 0:1] + _load_tile_rows(z1_ref) * rg[:, 1:2]
    y_ref[...] = _rms(h_ref[...] + moe, g_ref[...])


def _combine(h2, z, rg, g_final, tm):
    n, d = h2.shape
    n_t = n // tm
    row = lambda i: (i, 0)
    return pl.pallas_call(
        _combine_kernel, grid=(n_t,),
        in_specs=[pl.BlockSpec((tm, d), row), pl.BlockSpec((tm * ROW_TILE, LANES), row),
                  pl.BlockSpec((tm * ROW_TILE, LANES), lambda i: (n_t + i, 0)), pl.BlockSpec((tm, 8), row),
                  pl.BlockSpec((1, d), lambda i: (0, 0))],
        out_specs=pl.BlockSpec((tm, d), row),
        out_shape=jax.ShapeDtypeStruct((n, d), F32),
        compiler_params=_params("arbitrary"),
    )(h2, z, z, rg, g_final.reshape(1, d))


def _moe_and_final(h2, xn, ri, rg, counts, w1, w3, w2, g_final, *, tm):
    n = h2.shape[0]
    blk, e, lo, hi, starts = _work_items(counts, 2 * n)
    pos = (starts[ri[:, 0:2]] + ri[:, 2:4]).astype(I32)
    sub = (pos.T[:, :, None] * ROW_TILE + jnp.arange(ROW_TILE, dtype=I32)).reshape(2 * n * ROW_TILE)
    xs = _sc_scatter_rows(xn, sub, 2 * n * ROW_TILE)
    ys = _experts((blk, e, lo, hi), xs, w1, w3, w2)
    z = _sc_gather_rows(ys, sub)
    return _combine(h2, z, rg, g_final, tm)


def _pad_time(a, s_pad):
    return jnp.pad(a, ((0, 0), (0, s_pad - a.shape[1])) + ((0, 0),) * (a.ndim - 2))


def _layer(x, past, mem_k, mem_v, wts, lam_init, g_final, *, tm, tq, tk):
    b, t, d = x.shape
    n = b * t
    p_len = 0 if past is None else past[0].shape[1]
    s_valid = p_len + t
    s_pad = -(-s_valid // tk) * tk
    pos = p_len + jnp.arange(t)

    fk, fv, logf, dk, dv, qf, kf, vf, qd, kd, vd = _inproj(
        x, wts['g_mix'], wts['w_main'], wts['w_ff'], wts['b_forget'], pos, tm)
    new = (fk.reshape(b, t, FOX_HEADS, HEAD_DIM), fv.reshape(b, t, FOX_HEADS, HEAD_DIM),
           logf.reshape(b, t, FOX_HEADS), dk.reshape(b, t, DIFF_HEADS, 2, HEAD_DIM),
           dv.reshape(b, t, DIFF_HEADS, 2 * HEAD_DIM))

    def with_past(new_b16, old):
        cur = new_b16.reshape(b, t, FOX_W)
        if old is not None:
            cur = jnp.concatenate([old.reshape(b, p_len, FOX_W).astype(BF16), cur], axis=1)
        return _pad_time(cur, s_pad)

    logf_all = logf.reshape(b, t, FOX_HEADS)
    if past is not None:
        logf_all = jnp.concatenate([past[2].astype(F32), logf_all], axis=1)
    c2 = _cumsum_time(_pad_time(jnp.swapaxes(logf_all, 1, 2).reshape(b * FOX_HEADS, s_valid), s_pad)
                      .reshape(b, FOX_HEADS, s_pad))
    c_tm = jnp.swapaxes(c2, 1, 2)
    k_bias = _fox_bias_lanes(c_tm, q_side=False)
    q_bias = _fox_bias_lanes(c_tm[:, p_len:p_len + t], q_side=True).reshape(n, LANES)

    old = (None,) * 5 if past is None else past
    geom = dict(b=b, t=t, s_pad=s_pad, s_valid=s_valid, past=p_len, tq=tq, tk=tk, lam_init=lam_init)
    fox_o = _self_attention(qf, with_past(kf, old[0]), with_past(vf, old[1]), (q_bias, k_bias),
                            fox=True, **geom)
    diff_o = _self_attention(qd, with_past(kd, old[3]), with_past(vd, old[4]),
                             (wts['lam_vecs'], wts['g_diff']), fox=False, **geom)
    h1, qm = _outproj(x.reshape(n, d), fox_o, diff_o, wts['wo_f'], wts['wo_d'], wts['g_mem_q'],
                      wts['w_mq'], tm)
    h2, xn, ri, rg, cnt = _memattn(qm, mem_k, mem_v, h1, wts['w_mo'], wts['g_ffn'], wts['w_router'],
                                   wts['b_router'], b=b, t=t, tq=tq)
    counts = cnt[0, :N_EXPERTS].astype(I32)
    y = _moe_and_final(h2, xn, ri, rg, counts, wts['w1'], wts['w3'], wts['w2'], g_final, tm=tm)
    return y.reshape(b, t, d), new


def kernel(x_prompt, x_sample, cache_fox_k, cache_fox_v, cache_fox_logf, cache_diff_k, cache_diff_v, cache_mem_k, cache_mem_v, mem_prompt, g_mix, w_in, b_forget, lam_q1, lam_k1, lam_q2, lam_k2, g_diff, w_o, g_mem_q, g_mem_kv, w_mq, w_mk, w_mv, w_mo, g_ffn, w_group, b_group, w_erouter, b_erouter, w1, w3, w2, g_final):
    depth = w_in.shape[0]
    assert depth == 1
    l = 0
    lam_init = 0.8 - 0.6 * math.exp(-0.3 * l)
    d = D_MODEL
    w = w_in[l]
    ff0 = 3 * FOX_W
    w_router = jnp.zeros((d, LANES), F32).at[:, :N_EXPERTS].set(w_erouter[l])
    w_router = w_router.at[:, GROUP_LANE0:GROUP_LANE0 + N_GROUPS].set(w_group[l])
    b_router = jnp.zeros((1, LANES), F32).at[0, :N_EXPERTS].set(b_erouter[l])
    b_router = b_router.at[0, GROUP_LANE0:GROUP_LANE0 + N_GROUPS].set(b_group[l])
    wts = {
        'g_mix': g_mix[l],
        'w_main': jnp.concatenate([w[:, :ff0], w[:, ff0 + FOX_HEADS:]], axis=1).astype(BF16),
        'w_ff': jnp.pad(w[:, ff0:ff0 + FOX_HEADS], ((0, 0), (0, LANES - FOX_HEADS))).astype(BF16),
        'b_forget': b_forget[l],
        'lam_vecs': jnp.stack([lam_q1[l], lam_k1[l], lam_q2[l], lam_k2[l]]),
        'g_diff': g_diff[l],
        'wo_f': w_o[l][:FOX_W].astype(BF16), 'wo_d': w_o[l][FOX_W:].astype(BF16),
        'g_mem_q': g_mem_q[l], 'w_mq': w_mq[l].astype(BF16), 'w_mo': w_mo[l].astype(BF16),
        'g_ffn': g_ffn[l], 'w_router': w_router.astype(BF16), 'b_router': b_router,
        'w1': w1[l].astype(BF16), 'w3': w3[l].astype(BF16), 'w2': w2[l].astype(BF16),
    }

    bp, tp, _ = x_prompt.shape
    mk, mv, mk_b, mv_b = _memkv(mem_prompt.reshape(bp * MEM_LEN, d), g_mem_kv[l],
                                w_mk[l].astype(BF16), w_mv[l].astype(BF16), 512)
    yp, new_p = _layer(x_prompt, None, mk_b.reshape(bp, MEM_LEN, d), mv_b.reshape(bp, MEM_LEN, d),
                       wts, lam_init, g_final, tm=512, tq=512, tk=512)

    bs, ts, _ = x_sample.shape
    past = (cache_fox_k[l], cache_fox_v[l], cache_fox_logf[l], cache_diff_k[l], cache_diff_v[l])
    ys, new_s = _layer(x_sample, past, cache_mem_k[l].reshape(bs, MEM_LEN, d).astype(BF16),
                       cache_mem_v[l].reshape(bs, MEM_LEN, d).astype(BF16),
                       wts, lam_init, g_final, tm=bs * ts, tq=ts, tk=512)

    mem_shape = (1, bp, MEM_LEN, MEM_HEADS, MEM_HEAD_DIM)
    return (yp, ys) + tuple(a[None] for a in new_p) + (mk.reshape(mem_shape), mv.reshape(mem_shape)) \
        + tuple(a[None] for a in new_s)
```

```python
import functools
import math

import jax
import jax.numpy as jnp
from jax import lax
from jax.experimental import pallas as pl
from jax.experimental.pallas import tpu as pltpu
from jax.experimental.pallas import tpu_sc as plsc

F32 = jnp.float32
BF16 = jnp.bfloat16
I32 = jnp.int32

D_MODEL = 1024
HEAD_DIM = 64
FOX_HEADS = 8
DIFF_HEADS = 4
FOX_W = FOX_HEADS * HEAD_DIM
DIFF_W = DIFF_HEADS * 2 * HEAD_DIM
ROT_DIM = HEAD_DIM // 4
ROPE_THETA = 500000.0
CHUNK = 64
MEM_LEN = 256
MEM_HEADS = 4
MEM_HEAD_DIM = D_MODEL // MEM_HEADS
N_GROUPS = 4
EXPERTS_PER_GROUP = 8
N_EXPERTS = N_GROUPS * EXPERTS_PER_GROUP
D_EXPERT = 512
EPS = 1e-6

LANES = 128
LOG2E = 1.4426950408889634
NEG = -1e30
VMEM_LIMIT_BYTES = 56 * 1024 * 1024
MOE_ROWS = 256
GROUP_LANE0 = N_EXPERTS


def _params(*sem):
    return pltpu.CompilerParams(dimension_semantics=sem, vmem_limit_bytes=VMEM_LIMIT_BYTES)


def _rms(x, g):
    return (x * lax.rsqrt(jnp.mean(x * x, axis=-1, keepdims=True) + EPS)) * g


def _rope(x, cos, sin_lo, sin_hi):
    outs = []
    for c in range(x.shape[1] // LANES):
        xc = x[:, c * LANES:(c + 1) * LANES]
        up = pltpu.roll(xc, LANES - ROT_DIM // 2, axis=1)
        dn = pltpu.roll(xc, ROT_DIM // 2, axis=1)
        outs.append(xc * cos + up * sin_lo + dn * sin_hi)
    return jnp.concatenate(outs, axis=1)


def _inproj_kernel(x_ref, g_ref, w_ref, wf_ref, bf_ref, cos_ref, slo_ref, shi_ref,
                   fk_ref, fv_ref, logf_ref, dk_ref, dv_ref,
                   qf_ref, kf_ref, vf_ref, qd_ref, kd_ref, vd_ref, *, q_scale, time_major):
    n = _rms(x_ref[...], g_ref[...]).astype(BF16)
    turn = (lambda a: a.T) if time_major else (lambda a: a)

    def proj(c):
        return jnp.dot(n, w_ref[:, c * FOX_W:(c + 1) * FOX_W], preferred_element_type=F32)

    cos, slo, shi = cos_ref[...], slo_ref[...], shi_ref[...]
    qf_ref[...] = (proj(0) * q_scale).astype(BF16)
    fk = turn(proj(1))
    fk_ref[...] = fk
    kf_ref[...] = fk.astype(BF16)
    fv = proj(2)
    fv_ref[...] = turn(fv)
    vf_ref[...] = fv.astype(BF16)
    qd_ref[...] = (_rope(proj(3), cos, slo, shi) * q_scale).astype(BF16)
    dk = turn(_rope(proj(4), cos, slo, shi))
    dk_ref[...] = dk
    kd_ref[...] = dk.astype(BF16)
    dv = proj(5)
    dv_ref[...] = dv
    vd_ref[...] = dv.astype(BF16)
    z = jnp.dot(n, wf_ref[...], preferred_element_type=F32) + bf_ref[...]
    logf = jnp.minimum(z, 0.0) - jnp.log1p(jnp.exp(-jnp.abs(z)))
    logf_ref[...] = logf.T[:FOX_HEADS] if time_major else logf[:, :FOX_HEADS]


def _rope_tables(pos):
    half = ROT_DIM // 2
    inv = jnp.power(ROPE_THETA, -jnp.arange(half, dtype=F32) / half)
    ang = pos.astype(F32)[:, None] * inv[None, :]
    cos, sin = jnp.cos(ang), jnp.sin(ang)
    t = pos.shape[0]
    pad = jnp.zeros((t, HEAD_DIM - ROT_DIM), F32)
    zero = jnp.zeros((t, half), F32)
    cos64 = jnp.concatenate([cos, cos, pad + 1.0], axis=1)
    slo64 = jnp.concatenate([-sin, zero, pad], axis=1)
    shi64 = jnp.concatenate([zero, sin, pad], axis=1)
    return tuple(jnp.tile(a, (1, LANES // HEAD_DIM)) for a in (cos64, slo64, shi64))


def _inproj(x, g, w_main, w_ff, b_ff, pos, tm):
    b, t, d = x.shape
    n = b * t
    tables = _rope_tables(pos)
    time_major = tm <= t
    if time_major:
        assert t % tm == 0
        per = t // tm
        tab_map = lambda i: (i % per, 0)
        turned = pl.BlockSpec((None, FOX_W, tm), lambda i: (i // per, 0, i % per))
        gate = pl.BlockSpec((None, FOX_HEADS, tm), lambda i: (i // per, 0, i % per))
        turned_shape, gate_shape = (b, FOX_W, t), (b, FOX_HEADS, t)
    else:
        assert tm % t == 0
        tables = tuple(jnp.tile(a, (tm // t, 1)) for a in tables)
        tab_map = lambda i: (0, 0)
    assert n % tm == 0
    row = lambda i: (i, 0)
    const = lambda i: (0, 0)
    wide = pl.BlockSpec((tm, FOX_W), row)
    if not time_major:
        turned, gate = wide, pl.BlockSpec((tm, FOX_HEADS), row)
        turned_shape, gate_shape = (n, FOX_W), (n, FOX_HEADS)
    tab = pl.BlockSpec((tm, LANES), tab_map)
    sds = jax.ShapeDtypeStruct
    return pl.pallas_call(
        functools.partial(_inproj_kernel, q_scale=HEAD_DIM ** -0.5 * LOG2E, time_major=time_major),
        grid=(n // tm,),
        in_specs=[pl.BlockSpec((tm, d), row), pl.BlockSpec((1, d), const),
                  pl.BlockSpec(w_main.shape, const), pl.BlockSpec(w_ff.shape, const),
                  pl.BlockSpec((1, LANES), const), tab, tab, tab],
        out_specs=[turned, turned, gate, turned, wide,
                   wide, turned, wide, wide, turned, wide],
        out_shape=[sds(turned_shape, F32), sds(turned_shape, F32), sds(gate_shape, F32),
                   sds(turned_shape, F32), sds((n, FOX_W), F32),
                   sds((n, FOX_W), BF16), sds(turned_shape, BF16), sds((n, FOX_W), BF16),
                   sds((n, FOX_W), BF16), sds(turned_shape, BF16), sds((n, FOX_W), BF16)],
        compiler_params=_params("arbitrary"),
    )(x.reshape(n, d), g.reshape(1, d), w_main, w_ff,
      jnp.pad(b_ff.reshape(1, FOX_HEADS), ((0, 0), (0, LANES - FOX_HEADS))), *tables)


def _cumsum_kernel(x_ref, o_ref):
    x = x_ref[...]
    s_len = x.shape[1]
    lane = lax.broadcasted_iota(I32, x.shape, 1)
    shift = 1
    while shift < s_len:
        x = x + jnp.where(lane >= shift, pltpu.roll(x, shift, axis=1), 0.0)
        shift *= 2
    o_ref[...] = x * LOG2E


def _cumsum_time(logf_t):
    b, h, s = logf_t.shape
    spec = pl.BlockSpec((None, h, s), lambda i: (i, 0, 0))
    return pl.pallas_call(
        _cumsum_kernel, grid=(b,), in_specs=[spec], out_specs=spec,
        out_shape=jax.ShapeDtypeStruct((b, h, s), F32),
        compiler_params=_params("arbitrary"),
    )(logf_t)


AUX_LANES = 8
MXU_DEPTH = 256


def _bf16_part(x):
    bits = lax.bitcast_convert_type(x, jnp.uint32) & jnp.uint32(0xFFFF0000)
    return lax.bitcast_convert_type(bits, F32)


def _split3(x):
    hi = _bf16_part(x)
    mid = _bf16_part(x - hi)
    lo = _bf16_part(x - hi - mid)
    return hi.astype(BF16), mid.astype(BF16), lo.astype(BF16)


def _fox_bias_lanes(c, q_side):
    parts = _split3(c if q_side else -c)
    ones = jnp.ones_like(parts[0])
    zeros = jnp.zeros_like(parts[0])
    slots = (parts + (ones,) * 3 if q_side else (ones,) * 3 + parts) + (zeros, zeros)
    used = FOX_HEADS * AUX_LANES
    if q_side:
        b, t = c.shape[:2]
        return jnp.pad(jnp.stack(slots, axis=-1).reshape(b * t, used), ((0, 0), (0, LANES - used)))
    b, _, s_len = c.shape
    return jnp.pad(jnp.stack(slots, axis=2).reshape(b, used, s_len), ((0, 0), (0, LANES - used), (0, 0)))


def _attn_kernel(*refs, tq, tk, past, s_valid, fox, lam_init):
    if fox:
        q_ref, qa_ref, k_ref, ka_ref, v_ref, o_ref, sa_ref, sb_ref, mx_ref, m_ref, l_ref, acc_ref = refs
    else:
        q_ref, k_ref, v_ref, lam_ref, g_ref, o_ref, sa_ref, sb_ref, mx_ref, m_ref, l_ref, acc_ref = refs
    i = pl.program_id(2)
    q = q_ref[...]
    lane = lax.broadcasted_iota(I32, (tq, LANES), 1)
    low = lane < HEAD_DIM
    zero = jnp.zeros_like(q)
    q_part = [jnp.where(low, q, zero), jnp.where(low, zero, q)]
    if fox:
        qa = qa_ref[...]
        for h in range(2):
            lane0 = (2 * pl.program_id(1) + h) * AUX_LANES
            aux = jnp.where((lane >= lane0) & (lane < lane0 + AUX_LANES), qa, zero)
            q_part[h] = jnp.concatenate([q_part[h], aux], axis=1)
    m_ref[...] = jnp.full(m_ref.shape, NEG, F32)
    l_ref[...] = jnp.zeros(l_ref.shape, F32)
    acc_ref[...] = jnp.zeros(acc_ref.shape, F32)
    q0 = past + i * tq
    nb = q0 // tk
    bufs = (sa_ref, sb_ref)

    def scores(j, buf):
        ks = pl.multiple_of(j * tk, tk)
        k = k_ref[:, pl.ds(ks, tk)]
        if fox:
            k = jnp.concatenate([k, ka_ref[:, pl.ds(ks, tk)]], axis=0)
        for h in range(2):
            s = jnp.dot(q_part[h], k, preferred_element_type=F32)
            bufs[buf][h] = s
            mx_ref[buf, h] = jnp.broadcast_to(jnp.max(s, axis=1, keepdims=True), (tq, LANES))

    def consume(j, buf, use_mask):
        ks = pl.multiple_of(j * tk, tk)
        v = v_ref[pl.ds(ks, tk), :]
        pv, alpha = [], []
        for h in range(2):
            t = bufs[buf][h]
            if use_mask:
                qpos = q0 + lax.broadcasted_iota(I32, (tq, tk), 0)
                kpos = j * tk + lax.broadcasted_iota(I32, (tq, tk), 1)
                if fox:
                    seen = kpos <= qpos
                else:
                    seen = kpos < jnp.minimum((qpos // CHUNK + 1) * CHUNK, s_valid)
                t = jnp.where(seen, t, NEG)
                mx = jnp.max(t, axis=1, keepdims=True)
            else:
                mx = mx_ref[buf, h]
            m_prev = m_ref[h]
            m_new = jnp.maximum(m_prev, mx)
            a = jnp.exp2(m_prev - m_new)
            p = jnp.exp2(t - jnp.concatenate([m_new] * (tk // LANES), axis=1))
            l_ref[h] = a * l_ref[h] + jnp.sum(p, axis=1, keepdims=True)
            m_ref[h] = m_new
            pv.append(jnp.dot(p.astype(BF16), v, preferred_element_type=F32))
            alpha.append(a)
        if fox:
            acc_ref[0] = acc_ref[0] * jnp.where(low, alpha[0], alpha[1]) + jnp.where(low, pv[0], pv[1])
        else:
            for h in range(2):
                acc_ref[h] = acc_ref[h] * alpha[h] + pv[h]

    scores(0, 0)

    def pair(jj, c):
        j = 2 * jj
        scores(j + 1, 1)
        consume(j, 0, False)
        scores(j + 2, 0)
        consume(j + 1, 1, False)
        return c

    lax.fori_loop(0, nb // 2, pair, 0)

    @pl.when(nb % 2 == 1)
    def _():
        scores(nb, 1)
        consume(nb - 1, 0, False)
        consume(nb, 1, True)

    @pl.when(nb % 2 == 0)
    def _():
        consume(nb, 0, True)

    if fox:
        o = acc_ref[0] / jnp.where(low, l_ref[0], l_ref[1])
    else:
        lam_v = lam_ref[...]
        lam = (jnp.exp(jnp.sum(lam_v[0:1] * lam_v[1:2], axis=1, keepdims=True))
               - jnp.exp(jnp.sum(lam_v[2:3] * lam_v[3:4], axis=1, keepdims=True)) + lam_init)
        o = acc_ref[0] / l_ref[0] - lam * (acc_ref[1] / l_ref[1])
        o = _rms(o, g_ref[...]) * (1.0 - lam_init)
    o_ref[...] = o.astype(o_ref.dtype)


def _self_attention(q, k, v, extra, *, b, t, s_pad, s_valid, past, tq, tk, fox, lam_init):
    assert tk % tq == 0 and past % tq == 0 and t % tq == 0 and tk % CHUNK == 0 and s_pad % tk == 0
    assert tq % CHUNK == 0 or (CHUNK % tq == 0 and past % CHUNK == 0)
    n_q = t // tq
    k_spec = pl.BlockSpec((None, LANES, s_pad), lambda bi, p, i: (bi, p, 0))
    v_spec = pl.BlockSpec((None, s_pad, LANES), lambda bi, p, i: (bi, 0, p))
    q_spec = pl.BlockSpec((tq, LANES), lambda bi, p, i: (bi * n_q + i, p))
    const = lambda bi, p, i: (0, 0)
    if fox:
        in_specs = [q_spec, pl.BlockSpec((tq, LANES), lambda bi, p, i: (bi * n_q + i, 0)), k_spec,
                    pl.BlockSpec((None, LANES, s_pad), lambda bi, p, i: (bi, 0, 0)), v_spec]
        args = (q, extra[0], k, extra[1], v)
    else:
        in_specs = [q_spec, k_spec, v_spec, pl.BlockSpec((4, HEAD_DIM), const), pl.BlockSpec((1, LANES), const)]
        args = (q, k, v, extra[0], extra[1].reshape(1, LANES))
    stat = pltpu.VMEM((2, tq, LANES), F32)
    return pl.pallas_call(
        functools.partial(_attn_kernel, tq=tq, tk=tk, past=past, s_valid=s_valid, fox=fox, lam_init=lam_init),
        grid=(b, FOX_W // LANES, n_q),
        in_specs=in_specs,
        out_specs=q_spec,
        out_shape=jax.ShapeDtypeStruct((b * t, FOX_W), BF16),
        scratch_shapes=[pltpu.VMEM((2, tq, tk), F32), pltpu.VMEM((2, tq, tk), F32),
                        pltpu.VMEM((2, 2, tq, LANES), F32), stat, stat, stat],
        compiler_params=_params("arbitrary", "arbitrary", "arbitrary"),
    )(*args)


def _outproj_kernel(x_ref, fo_ref, do_ref, wof_ref, wod_ref, g_ref, wq_ref, h_ref, q_ref, *, q_scale):
    h = (x_ref[...] + jnp.dot(fo_ref[...], wof_ref[...], preferred_element_type=F32)
         + jnp.dot(do_ref[...], wod_ref[...], preferred_element_type=F32))
    h_ref[...] = h
    n = _rms(h, g_ref[...]).astype(BF16)
    q_ref[...] = (jnp.dot(n, wq_ref[...], preferred_element_type=F32) * q_scale).astype(BF16)


def _outproj(x2, fox_o, diff_o, wo_f, wo_d, g_mem_q, w_mq, tm):
    n, d = x2.shape
    row = lambda i: (i, 0)
    const = lambda i: (0, 0)
    return pl.pallas_call(
        functools.partial(_outproj_kernel, q_scale=MEM_HEAD_DIM ** -0.5 * LOG2E),
        grid=(n // tm,),
        in_specs=[pl.BlockSpec((tm, d), row), pl.BlockSpec((tm, FOX_W), row),
                  pl.BlockSpec((tm, DIFF_W), row), pl.BlockSpec((FOX_W, d), const),
                  pl.BlockSpec((DIFF_W, d), const), pl.BlockSpec((1, d), const),
                  pl.BlockSpec((d, d), const)],
        out_specs=[pl.BlockSpec((tm, d), row), pl.BlockSpec((tm, d), row)],
        out_shape=[jax.ShapeDtypeStruct((n, d), F32), jax.ShapeDtypeStruct((n, d), BF16)],
        compiler_params=_params("arbitrary"),
    )(x2, fox_o, diff_o, wo_f, wo_d, g_mem_q.reshape(1, d), w_mq)


def _memkv_kernel(m_ref, g_ref, wk_ref, wv_ref, k_ref, v_ref, kb_ref, vb_ref):
    n = _rms(m_ref[...], g_ref[...]).astype(BF16)
    k = jnp.dot(n, wk_ref[...], preferred_element_type=F32)
    v = jnp.dot(n, wv_ref[...], preferred_element_type=F32)
    k_ref[...] = k
    v_ref[...] = v
    kb_ref[...] = k.astype(BF16)
    vb_ref[...] = v.astype(BF16)


def _memkv(mem2, g, wk, wv, tm):
    n, d = mem2.shape
    row = lambda i: (i, 0)
    const = lambda i: (0, 0)
    blk = pl.BlockSpec((tm, d), row)
    wspec = pl.BlockSpec((d, d), const)
    return pl.pallas_call(
        _memkv_kernel, grid=(n // tm,),
        in_specs=[blk, pl.BlockSpec((1, d), const), wspec, wspec],
        out_specs=[blk, blk, blk, blk],
        out_shape=[jax.ShapeDtypeStruct((n, d), F32)] * 2 + [jax.ShapeDtypeStruct((n, d), BF16)] * 2,
        compiler_params=_params("arbitrary"),
    )(mem2, g.reshape(1, d), wk, wv)


ROW_TILE = D_MODEL // LANES


def _load_tile_rows(ref):
    rows = ref.shape[0] // ROW_TILE
    return jnp.concatenate([ref[pl.ds(s, rows, stride=ROW_TILE), :] for s in range(ROW_TILE)], axis=1)


def _store_tile_rows(ref, val):
    rows = val.shape[0]
    for s in range(ROW_TILE):
        ref[pl.ds(s, rows, stride=ROW_TILE), :] = val[:, s * LANES:(s + 1) * LANES]


def _first_index(hit, lane):
    return jnp.min(jnp.where(hit, lane.astype(F32), float(LANES)), axis=1, keepdims=True).astype(I32)


def _memattn_kernel(q_ref, k_ref, v_ref, h_ref, wo_ref, g_ref, wr_ref, br_ref,
                    h2_ref, xn_ref, ri_ref, rg_ref, cnt_ref, carry_ref, *, tq):
    @pl.when((pl.program_id(0) == 0) & (pl.program_id(1) == 0))
    def _():
        carry_ref[...] = jnp.zeros(carry_ref.shape, F32)

    q = q_ref[...]
    outs = []
    for h in range(MEM_HEADS):
        sl = slice(h * MEM_HEAD_DIM, (h + 1) * MEM_HEAD_DIM)
        s = lax.dot_general(q[:, sl], k_ref[:, sl], (((1,), (1,)), ((), ())), preferred_element_type=F32)
        p = jnp.exp2(s - jnp.max(s, axis=1, keepdims=True))
        o = jnp.dot(p.astype(BF16), v_ref[:, sl], preferred_element_type=F32)
        outs.append((o / jnp.sum(p, axis=1, keepdims=True)).astype(BF16))
    h2 = h_ref[...] + jnp.dot(jnp.concatenate(outs, axis=1), wo_ref[...], preferred_element_type=F32)
    h2_ref[...] = h2
    xn_f32 = _rms(h2, g_ref[...])
    _store_tile_rows(xn_ref, xn_f32)
    xn = xn_f32.astype(BF16)

    logits = jnp.dot(xn, wr_ref[...], preferred_element_type=F32) + br_ref[...]
    lane = lax.broadcasted_iota(I32, (tq, LANES), 1)
    g_log = jnp.where((lane >= GROUP_LANE0) & (lane < GROUP_LANE0 + N_GROUPS), logits, NEG)
    g_max = jnp.max(g_log, axis=1, keepdims=True)
    g_idx = _first_index(g_log == g_max, lane) - GROUP_LANE0
    g_p = 1.0 / jnp.sum(jnp.exp(g_log - g_max), axis=1, keepdims=True)
    in_group = (lane >= g_idx * EXPERTS_PER_GROUP) & (lane < (g_idx + 1) * EXPERTS_PER_GROUP)
    e_log = jnp.where(in_group, logits, NEG)
    e_exp = jnp.exp(e_log - jnp.max(e_log, axis=1, keepdims=True))
    prob = jnp.where(in_group, e_exp / jnp.sum(e_exp, axis=1, keepdims=True), -1.0)
    p1 = jnp.max(prob, axis=1, keepdims=True)
    i1 = _first_index(prob == p1, lane)
    rest = jnp.where(lane == i1, -1.0, prob)
    p2 = jnp.max(rest, axis=1, keepdims=True)
    i2 = _first_index(rest == p2, lane)
    top_sum = p1 + p2

    hot1, hot2 = lane == i1, lane == i2
    onehot = jnp.where(hot1 | hot2, 1.0, 0.0)
    r_i = lax.broadcasted_iota(I32, (tq, tq), 0)
    c_i = lax.broadcasted_iota(I32, (tq, tq), 1)
    before = jnp.dot(jnp.where(c_i < r_i, 1.0, 0.0).astype(BF16), onehot.astype(BF16),
                     preferred_element_type=F32) + carry_ref[0:1, :]
    r1 = jnp.sum(jnp.where(hot1, before, 0.0), axis=1, keepdims=True)
    r2 = jnp.sum(jnp.where(hot2, before, 0.0), axis=1, keepdims=True)
    carry_ref[...] = carry_ref[...] + jnp.sum(onehot, axis=0, keepdims=True)
    cnt_ref[...] = carry_ref[...]

    l8 = lax.broadcasted_iota(I32, (tq, 8), 1)
    ri_ref[...] = jnp.where(l8 == 0, i1, jnp.where(l8 == 1, i2, jnp.where(
        l8 == 2, r1.astype(I32), jnp.where(l8 == 3, r2.astype(I32), 0))))
    rg_ref[...] = jnp.where(l8 == 0, g_p * p1 / top_sum, jnp.where(l8 == 1, g_p * p2 / top_sum, 0.0))


def _memattn(qm, mk, mv, h1, w_mo, g_ffn, w_router, b_router, *, b0, b, t, tq):
    d = h1.shape[1]
    n = b * t
    n_q = t // tq
    row = lambda bi, i: (bi * n_q + i, 0)
    src_row = lambda bi, i: ((b0 + bi) * n_q + i, 0)
    const = lambda bi, i: (0, 0)
    mem = pl.BlockSpec((None, MEM_LEN, d), lambda bi, i: (b0 + bi, 0, 0))
    src = pl.BlockSpec((tq, d), src_row)
    blk = pl.BlockSpec((tq, d), row)
    tiled = pl.BlockSpec((tq * ROW_TILE, LANES), row)
    small = pl.BlockSpec((tq, 8), row)
    return pl.pallas_call(
        functools.partial(_memattn_kernel, tq=tq),
        grid=(b, n_q),
        in_specs=[src, mem, mem, src, pl.BlockSpec((d, d), const), pl.BlockSpec((1, d), const),
                  pl.BlockSpec((d, LANES), const), pl.BlockSpec((1, LANES), const)],
        out_specs=[blk, tiled, small, small, pl.BlockSpec((8, LANES), const)],
        out_shape=[jax.ShapeDtypeStruct((n, d), F32), jax.ShapeDtypeStruct((n * ROW_TILE, LANES), F32),
                   jax.ShapeDtypeStruct((n, 8), I32), jax.ShapeDtypeStruct((n, 8), F32),
                   jax.ShapeDtypeStruct((8, LANES), F32)],
        scratch_shapes=[pltpu.VMEM((8, LANES), F32)],
        compiler_params=_params("arbitrary", "arbitrary"),
    )(qm, mk, mv, h1, w_mo, g_ffn.reshape(1, d), w_router, b_router)


SC_CORES, SC_SUBCORES = 2, 16
SC_WINDOW = 128


def _sc_mesh():
    return plsc.VectorSubcoreMesh(core_axis_name="core", subcore_axis_name="subcore")


def _sc_split(n_windows):
    if n_windows % (SC_CORES * SC_SUBCORES) == 0:
        return ("core", "subcore")
    assert n_windows % SC_SUBCORES == 0
    return "subcore"


def _sc_scatter_rows(src, idx, n_out):
    rows = src.shape[0]
    win = SC_WINDOW
    n_src_blocks = rows // win

    @functools.partial(pl.kernel, out_type=jax.ShapeDtypeStruct((n_out, LANES), src.dtype),
                       mesh=_sc_mesh(), scratch_types=[])
    def scatter(x_hbm, i_hbm, o_hbm):
        def body(x_vmem, i_vmem):
            pltpu.sync_copy(x_vmem, o_hbm.at[i_vmem.at[0]])

        pltpu.emit_pipeline(
            body, grid=(n_out // win,),
            in_specs=[pl.BlockSpec((win, LANES), lambda i: (i % n_src_blocks, 0)),
                      pl.BlockSpec((1, win), lambda i: (0, i))],
            out_specs=[],
            core_axis_name=_sc_split(n_out // win), dimension_semantics=(pltpu.PARALLEL,),
        )(x_hbm, i_hbm)

    return scatter(src, idx.reshape(1, n_out))


def _sc_gather_rows(src, idx):
    n_out = idx.shape[0]
    win = SC_WINDOW

    @functools.partial(pl.kernel, out_type=jax.ShapeDtypeStruct((n_out, LANES), src.dtype),
                       mesh=_sc_mesh(), scratch_types=[])
    def gather(x_hbm, i_hbm, o_hbm):
        def body(i_vmem, o_vmem):
            pltpu.sync_copy(x_hbm.at[i_vmem.at[0]], o_vmem)

        pltpu.emit_pipeline(
            body, grid=(n_out // win,),
            in_specs=[pl.BlockSpec((1, win), lambda i: (0, i))],
            out_specs=[pl.BlockSpec((win, LANES), lambda i: (i, 0))],
            core_axis_name=_sc_split(n_out // win), dimension_semantics=(pltpu.PARALLEL,),
        )(i_hbm, o_hbm)

    return gather(src, idx.reshape(1, n_out))


def _expert_kernel(blk_ref, e_ref, lo_ref, hi_ref, x_ref, w1_ref, w3_ref, w2_ref, y_ref):
    w = pl.program_id(0)
    lo, hi = lo_ref[w], hi_ref[w]
    row0 = blk_ref[w] * MOE_ROWS

    def compute():
        x = _load_tile_rows(x_ref).astype(BF16)
        a = jnp.dot(x, w1_ref[...], preferred_element_type=F32)
        g = jnp.dot(x, w3_ref[...], preferred_element_type=F32)
        mid = ((a / (1.0 + jnp.exp(-a))) * g).astype(BF16)
        y = jnp.dot(mid, w2_ref[...], preferred_element_type=F32)
        rows = row0 + lax.broadcasted_iota(I32, y.shape, 0)
        return y, (rows >= lo) & (rows < hi)

    @pl.when((hi > lo) & (lo == row0))
    def _():
        y, mine = compute()
        _store_tile_rows(y_ref, jnp.where(mine, y, 0.0))

    @pl.when((hi > lo) & (lo != row0))
    def _():
        y, mine = compute()
        _store_tile_rows(y_ref, jnp.where(mine, y, _load_tile_rows(y_ref)))


def _experts(items, xs, w1, w3, w2):
    d = D_MODEL
    n_items = items[0].shape[0]
    xmap = lambda w, blk, e, lo, hi: (blk[w], 0)
    wmap = lambda w, blk, e, lo, hi: (e[w], 0, 0)
    return pl.pallas_call(
        _expert_kernel,
        grid_spec=pltpu.PrefetchScalarGridSpec(
            num_scalar_prefetch=4, grid=(n_items,),
            in_specs=[pl.BlockSpec((MOE_ROWS * ROW_TILE, LANES), xmap),
                      pl.BlockSpec((None, d, D_EXPERT), wmap), pl.BlockSpec((None, d, D_EXPERT), wmap),
                      pl.BlockSpec((None, D_EXPERT, d), wmap)],
            out_specs=pl.BlockSpec((MOE_ROWS * ROW_TILE, LANES), xmap)),
        out_shape=jax.ShapeDtypeStruct(xs.shape, F32),
        compiler_params=_params("arbitrary"),
    )(*items, xs, w1, w3, w2)


def _work_items(counts, n_rows):
    n_blocks = n_rows // MOE_ROWS
    n_items = n_blocks + N_EXPERTS - 1
    ends = jnp.cumsum(counts)
    starts = ends - counts
    first_blk = starts // MOE_ROWS
    n_blk = jnp.where(counts > 0, (ends - 1) // MOE_ROWS - first_blk + 1, 0)
    item_end = jnp.cumsum(n_blk)
    total = item_end[-1]
    w = jnp.arange(n_items, dtype=I32)
    wc = jnp.minimum(w, total - 1)
    e = jnp.sum(item_end[None, :] <= wc[:, None], axis=1).astype(I32)
    blk = (first_blk[e] + wc - (item_end[e] - n_blk[e])).astype(I32)
    lo = jnp.maximum(starts[e], blk * MOE_ROWS).astype(I32)
    hi = jnp.minimum(ends[e], (blk + 1) * MOE_ROWS).astype(I32)
    live = w < total
    return blk, e, jnp.where(live, lo, 0), jnp.where(live, hi, 0), starts


def _combine_kernel(h_ref, z0_ref, z1_ref, rg_ref, g_ref, *rest):
    y_ref = rest[-1]
    rg = rg_ref[...]
    moe = _load_tile_rows(z0_ref) * rg[:, 0:1] + _load_tile_rows(z1_ref) * rg[:, 1:2]
    y_ref[...] = _rms(h_ref[...] + moe, g_ref[...])


def _combine(h2, z, rg, g_final, tm, y_prev, row0, n_total):
    n, d = h2.shape
    n_t = n // tm
    row = lambda i: (i, 0)
    in_specs = [pl.BlockSpec((tm, d), row), pl.BlockSpec((tm * ROW_TILE, LANES), row),
                pl.BlockSpec((tm * ROW_TILE, LANES), lambda i: (n_t + i, 0)), pl.BlockSpec((tm, 8), row),
                pl.BlockSpec((1, d), lambda i: (0, 0))]
    args = [h2, z, z, rg, g_final.reshape(1, d)]
    aliases = {}
    if y_prev is not None:
        in_specs.append(pl.BlockSpec(memory_space=pl.ANY))
        args.append(y_prev)
        aliases = {len(args) - 1: 0}
    return pl.pallas_call(
        _combine_kernel, grid=(n_t,),
        in_specs=in_specs,
        out_specs=pl.BlockSpec((tm, d), lambda i: (row0 // tm + i, 0)),
        out_shape=jax.ShapeDtypeStruct((n_total, d), F32),
        input_output_aliases=aliases,
        compiler_params=_params("arbitrary"),
    )(*args)


def _moe_and_final(h2, xn, ri, rg, counts, w1, w3, w2, g_final, *, tm, y_prev, row0, n_total):
    n = h2.shape[0]
    blk, e, lo, hi, starts = _work_items(counts, 2 * n)
    pos = (starts[ri[:, 0:2]] + ri[:, 2:4]).astype(I32)
    sub = (pos.T[:, :, None] * ROW_TILE + jnp.arange(ROW_TILE, dtype=I32)).reshape(2 * n * ROW_TILE)
    xs = _sc_scatter_rows(xn, sub, 2 * n * ROW_TILE)
    ys = _experts((blk, e, lo, hi), xs, w1, w3, w2)
    z = _sc_gather_rows(ys, sub)
    return _combine(h2, z, rg, g_final, tm, y_prev, row0, n_total)


def _pad_time(a, s_pad):
    return jnp.pad(a, ((0, 0), (0, s_pad - a.shape[1])) + ((0, 0),) * (a.ndim - 2))


def _layer(x, past, mem_k, mem_v, wts, lam_init, g_final, *, tm, tq, tk, moe_parts):
    b, t, d = x.shape
    n = b * t
    p_len = 0 if past is None else past[0].shape[1]
    s_valid = p_len + t
    s_pad = -(-s_valid // tk) * tk
    pos = p_len + jnp.arange(t)

    fk, fv, logf, dk, dv, qf, kf, vf, qd, kd, vd = _inproj(
        x, wts['g_mix'], wts['w_main'], wts['w_ff'], wts['b_forget'], pos, tm)
    time_major = tm <= t

    def leaf(a, dims):
        if time_major:
            return jnp.moveaxis(a.reshape((b,) + dims + (t,)), -1, 1)
        return a.reshape((b, t) + dims)

    new = (leaf(fk, (FOX_HEADS, HEAD_DIM)), leaf(fv, (FOX_HEADS, HEAD_DIM)), leaf(logf, (FOX_HEADS,)),
           leaf(dk, (DIFF_HEADS, 2, HEAD_DIM)), dv.reshape(b, t, DIFF_HEADS, 2 * HEAD_DIM))

    def keys_t(cur, old_rows):
        if not time_major:
            cur = jnp.swapaxes(cur.reshape(b, t, FOX_W), 1, 2)
        if old_rows is not None:
            prev = jnp.swapaxes(old_rows.reshape(b, p_len, FOX_W).astype(BF16), 1, 2)
            cur = jnp.concatenate([prev, cur], axis=2)
        return jnp.pad(cur, ((0, 0), (0, 0), (0, s_pad - s_valid)))

    def with_past(new_b16, old_rows):
        cur = new_b16.reshape(b, t, FOX_W)
        if old_rows is not None:
            cur = jnp.concatenate([old_rows.reshape(b, p_len, FOX_W).astype(BF16), cur], axis=1)
        return _pad_time(cur, s_pad)

    logf_t = logf if time_major else jnp.swapaxes(logf.reshape(b, t, FOX_HEADS), 1, 2)
    if past is not None:
        logf_t = jnp.concatenate([jnp.swapaxes(past[2].astype(F32), 1, 2), logf_t], axis=2)
    c2 = _cumsum_time(jnp.pad(logf_t, ((0, 0), (0, 0), (0, s_pad - s_valid))))
    k_bias = _fox_bias_lanes(c2, q_side=False)
    q_bias = _fox_bias_lanes(jnp.swapaxes(c2[:, :, p_len:p_len + t], 1, 2), q_side=True)

    old = (None,) * 5 if past is None else past
    geom = dict(b=b, t=t, s_pad=s_pad, s_valid=s_valid, past=p_len, tq=tq, tk=tk, lam_init=lam_init)
    fox_o = _self_attention(qf, keys_t(kf, old[0]), with_past(vf, old[1]), (q_bias, k_bias),
                            fox=True, **geom)
    diff_o = _self_attention(qd, keys_t(kd, old[3]), with_past(vd, old[4]),
                             (wts['lam_vecs'], wts['g_diff']), fox=False, **geom)
    h1, qm = _outproj(x.reshape(n, d), fox_o, diff_o, wts['wo_f'], wts['wo_d'], wts['g_mem_q'],
                      wts['w_mq'], tm)
    assert b % moe_parts == 0
    bp = b // moe_parts
    y = None
    for part in range(moe_parts):
        h2, xn, ri, rg, cnt = _memattn(qm, mem_k, mem_v, h1, wts['w_mo'], wts['g_ffn'], wts['w_router'],
                                       wts['b_router'], b0=part * bp, b=bp, t=t, tq=tq)
        counts = cnt[0, :N_EXPERTS].astype(I32)
        y = _moe_and_final(h2, xn, ri, rg, counts, wts['w1'], wts['w3'], wts['w2'], g_final, tm=tm,
                           y_prev=y, row0=part * bp * t, n_total=n)
    return y.reshape(b, t, d), new


def kernel(x_prompt, x_sample, cache_fox_k, cache_fox_v, cache_fox_logf, cache_diff_k, cache_diff_v, cache_mem_k, cache_mem_v, mem_prompt, g_mix, w_in, b_forget, lam_q1, lam_k1, lam_q2, lam_k2, g_diff, w_o, g_mem_q, g_mem_kv, w_mq, w_mk, w_mv, w_mo, g_ffn, w_group, b_group, w_erouter, b_erouter, w1, w3, w2, g_final):
    depth = w_in.shape[0]
    assert depth == 1
    l = 0
    lam_init = 0.8 - 0.6 * math.exp(-0.3 * l)
    d = D_MODEL
    w = w_in[l]
    ff0 = 3 * FOX_W
    w_router = jnp.zeros((d, LANES), F32).at[:, :N_EXPERTS].set(w_erouter[l])
    w_router = w_router.at[:, GROUP_LANE0:GROUP_LANE0 + N_GROUPS].set(w_group[l])
    b_router = jnp.zeros((1, LANES), F32).at[0, :N_EXPERTS].set(b_erouter[l])
    b_router = b_router.at[0, GROUP_LANE0:GROUP_LANE0 + N_GROUPS].set(b_group[l])
    wts = {
        'g_mix': g_mix[l],
        'w_main': jnp.concatenate([w[:, :ff0], w[:, ff0 + FOX_HEADS:]], axis=1).astype(BF16),
        'w_ff': jnp.pad(w[:, ff0:ff0 + FOX_HEADS], ((0, 0), (0, LANES - FOX_HEADS))).astype(BF16),
        'b_forget': b_forget[l],
        'lam_vecs': jnp.stack([lam_q1[l], lam_k1[l], lam_q2[l], lam_k2[l]]),
        'g_diff': g_diff[l],
        'wo_f': w_o[l][:FOX_W].astype(BF16), 'wo_d': w_o[l][FOX_W:].astype(BF16),
        'g_mem_q': g_mem_q[l], 'w_mq': w_mq[l].astype(BF16), 'w_mo': w_mo[l].astype(BF16),
        'g_ffn': g_ffn[l], 'w_router': w_router.astype(BF16), 'b_router': b_router,
        'w1': w1[l].astype(BF16), 'w3': w3[l].astype(BF16), 'w2': w2[l].astype(BF16),
    }

    bp, tp, _ = x_prompt.shape
    mk, mv, mk_b, mv_b = _memkv(mem_prompt.reshape(bp * MEM_LEN, d), g_mem_kv[l],
                                w_mk[l].astype(BF16), w_mv[l].astype(BF16), 512)
    yp, new_p = _layer(x_prompt, None, mk_b.reshape(bp, MEM_LEN, d), mv_b.reshape(bp, MEM_LEN, d),
                       wts, lam_init, g_final, tm=512, tq=512, tk=512, moe_parts=2)

    bs, ts, _ = x_sample.shape
    past = (cache_fox_k[l], cache_fox_v[l], cache_fox_logf[l], cache_diff_k[l], cache_diff_v[l])
    ys, new_s = _layer(x_sample, past, cache_mem_k[l].reshape(bs, MEM_LEN, d).astype(BF16),
                       cache_mem_v[l].reshape(bs, MEM_LEN, d).astype(BF16),
                       wts, lam_init, g_final, tm=bs * ts, tq=ts, tk=512, moe_parts=1)

    mem_shape = (1, bp, MEM_LEN, MEM_HEADS, MEM_HEAD_DIM)
    return (yp, ys) + tuple(a[None] for a in new_p) + (mk.reshape(mem_shape), mv.reshape(mem_shape)) \
        + tuple(a[None] for a in new_s)
```

```python
import functools
import math

import jax
import jax.numpy as jnp
from jax import lax
from jax.experimental import pallas as pl
from jax.experimental.pallas import tpu as pltpu
from jax.experimental.pallas import tpu_sc as plsc

F32 = jnp.float32
BF16 = jnp.bfloat16
I32 = jnp.int32

D_MODEL = 1024
HEAD_DIM = 64
FOX_HEADS = 8
DIFF_HEADS = 4
FOX_W = FOX_HEADS * HEAD_DIM
DIFF_W = DIFF_HEADS * 2 * HEAD_DIM
ROT_DIM = HEAD_DIM // 4
ROPE_THETA = 500000.0
CHUNK = 64
MEM_LEN = 256
MEM_HEADS = 4
MEM_HEAD_DIM = D_MODEL // MEM_HEADS
N_GROUPS = 4
EXPERTS_PER_GROUP = 8
N_EXPERTS = N_GROUPS * EXPERTS_PER_GROUP
D_EXPERT = 512
EPS = 1e-6

LANES = 128
LOG2E = 1.4426950408889634
NEG = -1e30
VMEM_LIMIT_BYTES = 56 * 1024 * 1024
MOE_ROWS = 256
GROUP_LANE0 = N_EXPERTS


def _params(*sem):
    return pltpu.CompilerParams(dimension_semantics=sem, vmem_limit_bytes=VMEM_LIMIT_BYTES)


def _rms(x, g):
    return (x * lax.rsqrt(jnp.mean(x * x, axis=-1, keepdims=True) + EPS)) * g


def _rope(x, cos, sin_lo, sin_hi):
    outs = []
    for c in range(x.shape[1] // LANES):
        xc = x[:, c * LANES:(c + 1) * LANES]
        up = pltpu.roll(xc, LANES - ROT_DIM // 2, axis=1)
        dn = pltpu.roll(xc, ROT_DIM // 2, axis=1)
        outs.append(xc * cos + up * sin_lo + dn * sin_hi)
    return jnp.concatenate(outs, axis=1)


def _inproj_kernel(x_ref, g_ref, w_ref, wf_ref, bf_ref, cos_ref, slo_ref, shi_ref,
                   fk_ref, fv_ref, logf_ref, dk_ref, dv_ref,
                   qf_ref, kf_ref, vf_ref, qd_ref, kd_ref, vd_ref, *, q_scale, time_major):
    n = _rms(x_ref[...], g_ref[...]).astype(BF16)
    turn = (lambda a: a.T) if time_major else (lambda a: a)

    def proj(c):
        return jnp.dot(n, w_ref[:, c * FOX_W:(c + 1) * FOX_W], preferred_element_type=F32)

    cos, slo, shi = cos_ref[...], slo_ref[...], shi_ref[...]
    qf_ref[...] = (proj(0) * q_scale).astype(BF16)
    fk = turn(proj(1))
    fk_ref[...] = fk
    kf_ref[...] = fk.astype(BF16)
    fv = proj(2)
    fv_ref[...] = turn(fv)
    vf_ref[...] = fv.astype(BF16)
    qd_ref[...] = (_rope(proj(3), cos, slo, shi) * q_scale).astype(BF16)
    dk = turn(_rope(proj(4), cos, slo, shi))
    dk_ref[...] = dk
    kd_ref[...] = dk.astype(BF16)
    dv = proj(5)
    for h in range(DIFF_HEADS):
        dv_ref[pl.ds(h, dv.shape[0], stride=DIFF_HEADS), :] = dv[:, h * LANES:(h + 1) * LANES]
    vd_ref[...] = dv.astype(BF16)
    z = jnp.dot(n, wf_ref[...], preferred_element_type=F32) + bf_ref[...]
    logf = jnp.minimum(z, 0.0) - jnp.log1p(jnp.exp(-jnp.abs(z)))
    logf_ref[...] = logf.T[:FOX_HEADS] if time_major else logf[:, :FOX_HEADS]


def _rope_tables(pos):
    half = ROT_DIM // 2
    inv = jnp.power(ROPE_THETA, -jnp.arange(half, dtype=F32) / half)
    ang = pos.astype(F32)[:, None] * inv[None, :]
    cos, sin = jnp.cos(ang), jnp.sin(ang)
    t = pos.shape[0]
    pad = jnp.zeros((t, HEAD_DIM - ROT_DIM), F32)
    zero = jnp.zeros((t, half), F32)
    cos64 = jnp.concatenate([cos, cos, pad + 1.0], axis=1)
    slo64 = jnp.concatenate([-sin, zero, pad], axis=1)
    shi64 = jnp.concatenate([zero, sin, pad], axis=1)
    return tuple(jnp.tile(a, (1, LANES // HEAD_DIM)) for a in (cos64, slo64, shi64))


def _inproj(x, g, w_main, w_ff, b_ff, pos, tm):
    b, t, d = x.shape
    n = b * t
    tables = _rope_tables(pos)
    time_major = tm <= t
    if time_major:
        assert t % tm == 0
        per = t // tm
        tab_map = lambda i: (i % per, 0)
        turned = pl.BlockSpec((None, FOX_W, tm), lambda i: (i // per, 0, i % per))
        gate = pl.BlockSpec((None, FOX_HEADS, tm), lambda i: (i // per, 0, i % per))
        turned_shape, gate_shape = (b, FOX_W, t), (b, FOX_HEADS, t)
    else:
        assert tm % t == 0
        tables = tuple(jnp.tile(a, (tm // t, 1)) for a in tables)
        tab_map = lambda i: (0, 0)
    assert n % tm == 0
    row = lambda i: (i, 0)
    const = lambda i: (0, 0)
    wide = pl.BlockSpec((tm, FOX_W), row)
    if not time_major:
        turned, gate = wide, pl.BlockSpec((tm, FOX_HEADS), row)
        turned_shape, gate_shape = (n, FOX_W), (n, FOX_HEADS)
    tab = pl.BlockSpec((tm, LANES), tab_map)
    sds = jax.ShapeDtypeStruct
    return pl.pallas_call(
        functools.partial(_inproj_kernel, q_scale=HEAD_DIM ** -0.5 * LOG2E, time_major=time_major),
        grid=(n // tm,),
        in_specs=[pl.BlockSpec((tm, d), row), pl.BlockSpec((1, d), const),
                  pl.BlockSpec(w_main.shape, const), pl.BlockSpec(w_ff.shape, const),
                  pl.BlockSpec((1, LANES), const), tab, tab, tab],
        out_specs=[turned, turned, gate, turned, pl.BlockSpec((tm * DIFF_HEADS, LANES), row),
                   wide, turned, wide, wide, turned, wide],
        out_shape=[sds(turned_shape, F32), sds(turned_shape, F32), sds(gate_shape, F32),
                   sds(turned_shape, F32), sds((n * DIFF_HEADS, LANES), F32),
                   sds((n, FOX_W), BF16), sds(turned_shape, BF16), sds((n, FOX_W), BF16),
                   sds((n, FOX_W), BF16), sds(turned_shape, BF16), sds((n, FOX_W), BF16)],
        compiler_params=_params("arbitrary"),
    )(x.reshape(n, d), g.reshape(1, d), w_main, w_ff,
      jnp.pad(b_ff.reshape(1, FOX_HEADS), ((0, 0), (0, LANES - FOX_HEADS))), *tables)


def _cumsum_kernel(x_ref, o_ref):
    x = x_ref[...]
    s_len = x.shape[1]
    lane = lax.broadcasted_iota(I32, x.shape, 1)
    shift = 1
    while shift < s_len:
        x = x + jnp.where(lane >= shift, pltpu.roll(x, shift, axis=1), 0.0)
        shift *= 2
    o_ref[...] = x * LOG2E


def _cumsum_time(logf_t):
    b, h, s = logf_t.shape
    spec = pl.BlockSpec((None, h, s), lambda i: (i, 0, 0))
    return pl.pallas_call(
        _cumsum_kernel, grid=(b,), in_specs=[spec], out_specs=spec,
        out_shape=jax.ShapeDtypeStruct((b, h, s), F32),
        compiler_params=_params("arbitrary"),
    )(logf_t)


AUX_LANES = 8
MXU_DEPTH = 256


def _bf16_part(x):
    bits = lax.bitcast_convert_type(x, jnp.uint32) & jnp.uint32(0xFFFF0000)
    return lax.bitcast_convert_type(bits, F32)


def _split3(x):
    hi = _bf16_part(x)
    mid = _bf16_part(x - hi)
    lo = _bf16_part(x - hi - mid)
    return hi.astype(BF16), mid.astype(BF16), lo.astype(BF16)


def _fox_bias_lanes(c, q_side):
    parts = _split3(c if q_side else -c)
    ones = jnp.ones_like(parts[0])
    zeros = jnp.zeros_like(parts[0])
    slots = (parts + (ones,) * 3 if q_side else (ones,) * 3 + parts) + (zeros, zeros)
    used = FOX_HEADS * AUX_LANES
    if q_side:
        b, t = c.shape[:2]
        return jnp.pad(jnp.stack(slots, axis=-1).reshape(b * t, used), ((0, 0), (0, LANES - used)))
    b, _, s_len = c.shape
    return jnp.pad(jnp.stack(slots, axis=2).reshape(b, used, s_len), ((0, 0), (0, LANES - used), (0, 0)))


def _attn_kernel(*refs, tq, tk, past, s_valid, fox, lam_init):
    if fox:
        q_ref, qa_ref, k_ref, ka_ref, v_ref, o_ref, sa_ref, sb_ref, mx_ref, m_ref, l_ref, acc_ref = refs
    else:
        q_ref, k_ref, v_ref, lam_ref, g_ref, o_ref, sa_ref, sb_ref, mx_ref, m_ref, l_ref, acc_ref = refs
    i = pl.program_id(2)
    q = q_ref[...]
    lane = lax.broadcasted_iota(I32, (tq, LANES), 1)
    low = lane < HEAD_DIM
    zero = jnp.zeros_like(q)
    q_part = [jnp.where(low, q, zero), jnp.where(low, zero, q)]
    if fox:
        qa = qa_ref[...]
        for h in range(2):
            lane0 = (2 * pl.program_id(1) + h) * AUX_LANES
            aux = jnp.where((lane >= lane0) & (lane < lane0 + AUX_LANES), qa, zero)
            q_part[h] = jnp.concatenate([q_part[h], aux], axis=1)
    m_ref[...] = jnp.full(m_ref.shape, NEG, F32)
    l_ref[...] = jnp.zeros(l_ref.shape, F32)
    acc_ref[...] = jnp.zeros(acc_ref.shape, F32)
    q0 = past + i * tq
    nb = q0 // tk
    bufs = (sa_ref, sb_ref)

    def scores(j, buf):
        ks = pl.multiple_of(j * tk, tk)
        k = k_ref[:, pl.ds(ks, tk)]
        if fox:
            k = jnp.concatenate([k, ka_ref[:, pl.ds(ks, tk)]], axis=0)
        for h in range(2):
            s = jnp.dot(q_part[h], k, preferred_element_type=F32)
            bufs[buf][h] = s
            mx_ref[buf, h] = jnp.broadcast_to(jnp.max(s, axis=1, keepdims=True), (tq, LANES))

    def consume(j, buf, use_mask):
        ks = pl.multiple_of(j * tk, tk)
        v = jnp.concatenate([v_ref[pl.ds(ks, tk), :], jnp.ones((tk, LANES), BF16)], axis=1)
        pv, alpha = [], []
        for h in range(2):
            t = bufs[buf][h]
            if use_mask:
                qpos = q0 + lax.broadcasted_iota(I32, (tq, tk), 0)
                kpos = j * tk + lax.broadcasted_iota(I32, (tq, tk), 1)
                if fox:
                    seen = kpos <= qpos
                else:
                    seen = kpos < jnp.minimum((qpos // CHUNK + 1) * CHUNK, s_valid)
                t = jnp.where(seen, t, NEG)
                mx = jnp.max(t, axis=1, keepdims=True)
            else:
                mx = mx_ref[buf, h]
            m_prev = m_ref[h]
            m_new = jnp.maximum(m_prev, mx)
            a = jnp.exp2(m_prev - m_new)
            p = jnp.exp2(t - jnp.concatenate([m_new] * (tk // LANES), axis=1))
            m_ref[h] = m_new
            both = jnp.dot(p.astype(BF16), v, preferred_element_type=F32)
            pv.append(both[:, :LANES])
            l_ref[h] = a * l_ref[h] + both[:, LANES:]
            alpha.append(a)
        if fox:
            acc_ref[0] = acc_ref[0] * jnp.where(low, alpha[0], alpha[1]) + jnp.where(low, pv[0], pv[1])
        else:
            for h in range(2):
                acc_ref[h] = acc_ref[h] * alpha[h] + pv[h]

    scores(0, 0)

    def pair(jj, c):
        j = 2 * jj
        scores(j + 1, 1)
        consume(j, 0, False)
        scores(j + 2, 0)
        consume(j + 1, 1, False)
        return c

    lax.fori_loop(0, nb // 2, pair, 0)

    @pl.when(nb % 2 == 1)
    def _():
        scores(nb, 1)
        consume(nb - 1, 0, False)
        consume(nb, 1, True)

    @pl.when(nb % 2 == 0)
    def _():
        consume(nb, 0, True)

    if fox:
        o = acc_ref[0] / jnp.where(low, l_ref[0], l_ref[1])
    else:
        lam_v = lam_ref[...]
        lam = (jnp.exp(jnp.sum(lam_v[0:1] * lam_v[1:2], axis=1, keepdims=True))
               - jnp.exp(jnp.sum(lam_v[2:3] * lam_v[3:4], axis=1, keepdims=True)) + lam_init)
        o = acc_ref[0] / l_ref[0] - lam * (acc_ref[1] / l_ref[1])
        o = _rms(o, g_ref[...]) * (1.0 - lam_init)
    o_ref[...] = o.astype(o_ref.dtype)


def _self_attention(q, k, v, extra, *, b, t, s_pad, s_valid, past, tq, tk, fox, lam_init):
    assert tk % tq == 0 and past % tq == 0 and t % tq == 0 and tk % CHUNK == 0 and s_pad % tk == 0
    assert tq % CHUNK == 0 or (CHUNK % tq == 0 and past % CHUNK == 0)
    n_q = t // tq
    k_spec = pl.BlockSpec((None, LANES, s_pad), lambda bi, p, i: (bi, p, 0))
    v_spec = pl.BlockSpec((None, s_pad, LANES), lambda bi, p, i: (bi, 0, p))
    q_spec = pl.BlockSpec((tq, LANES), lambda bi, p, i: (bi * n_q + i, p))
    const = lambda bi, p, i: (0, 0)
    if fox:
        in_specs = [q_spec, pl.BlockSpec((tq, LANES), lambda bi, p, i: (bi * n_q + i, 0)), k_spec,
                    pl.BlockSpec((None, LANES, s_pad), lambda bi, p, i: (bi, 0, 0)), v_spec]
        args = (q, extra[0], k, extra[1], v)
    else:
        in_specs = [q_spec, k_spec, v_spec, pl.BlockSpec((4, HEAD_DIM), const), pl.BlockSpec((1, LANES), const)]
        args = (q, k, v, extra[0], extra[1].reshape(1, LANES))
    stat = pltpu.VMEM((2, tq, LANES), F32)
    return pl.pallas_call(
        functools.partial(_attn_kernel, tq=tq, tk=tk, past=past, s_valid=s_valid, fox=fox, lam_init=lam_init),
        grid=(b, FOX_W // LANES, n_q),
        in_specs=in_specs,
        out_specs=q_spec,
        out_shape=jax.ShapeDtypeStruct((b * t, FOX_W), BF16),
        scratch_shapes=[pltpu.VMEM((2, tq, tk), F32), pltpu.VMEM((2, tq, tk), F32),
                        pltpu.VMEM((2, 2, tq, LANES), F32), stat, stat, stat],
        compiler_params=_params("arbitrary", "arbitrary", "arbitrary"),
    )(*args)


def _outproj_kernel(x_ref, fo_ref, do_ref, wof_ref, wod_ref, g_ref, wq_ref, h_ref, q_ref, *, q_scale):
    h = (x_ref[...] + jnp.dot(fo_ref[...], wof_ref[...], preferred_element_type=F32)
         + jnp.dot(do_ref[...], wod_ref[...], preferred_element_type=F32))
    h_ref[...] = h
    n = _rms(h, g_ref[...]).astype(BF16)
    q_ref[...] = (jnp.dot(n, wq_ref[...], preferred_element_type=F32) * q_scale).astype(BF16)


def _outproj(x2, fox_o, diff_o, wo_f, wo_d, g_mem_q, w_mq, tm):
    n, d = x2.shape
    row = lambda i: (i, 0)
    const = lambda i: (0, 0)
    return pl.pallas_call(
        functools.partial(_outproj_kernel, q_scale=MEM_HEAD_DIM ** -0.5 * LOG2E),
        grid=(n // tm,),
        in_specs=[pl.BlockSpec((tm, d), row), pl.BlockSpec((tm, FOX_W), row),
                  pl.BlockSpec((tm, DIFF_W), row), pl.BlockSpec((FOX_W, d), const),
                  pl.BlockSpec((DIFF_W, d), const), pl.BlockSpec((1, d), const),
                  pl.BlockSpec((d, d), const)],
        out_specs=[pl.BlockSpec((tm, d), row), pl.BlockSpec((tm, d), row)],
        out_shape=[jax.ShapeDtypeStruct((n, d), F32), jax.ShapeDtypeStruct((n, d), BF16)],
        compiler_params=_params("arbitrary"),
    )(x2, fox_o, diff_o, wo_f, wo_d, g_mem_q.reshape(1, d), w_mq)


def _memkv_kernel(m_ref, g_ref, wk_ref, wv_ref, k_ref, v_ref, kb_ref, vb_ref):
    n = _rms(m_ref[...], g_ref[...]).astype(BF16)
    k = jnp.dot(n, wk_ref[...], preferred_element_type=F32)
    v = jnp.dot(n, wv_ref[...], preferred_element_type=F32)
    k_ref[...] = k
    v_ref[...] = v
    kb_ref[...] = k.astype(BF16)
    vb_ref[...] = v.astype(BF16)


def _memkv(mem2, g, wk, wv, tm):
    n, d = mem2.shape
    row = lambda i: (i, 0)
    const = lambda i: (0, 0)
    blk = pl.BlockSpec((tm, d), row)
    wspec = pl.BlockSpec((d, d), const)
    return pl.pallas_call(
        _memkv_kernel, grid=(n // tm,),
        in_specs=[blk, pl.BlockSpec((1, d), const), wspec, wspec],
        out_specs=[blk, blk, blk, blk],
        out_shape=[jax.ShapeDtypeStruct((n, d), F32)] * 2 + [jax.ShapeDtypeStruct((n, d), BF16)] * 2,
        compiler_params=_params("arbitrary"),
    )(mem2, g.reshape(1, d), wk, wv)


ROW_TILE = D_MODEL // LANES


def _load_tile_rows(ref):
    rows = ref.shape[0] // ROW_TILE
    return jnp.concatenate([ref[pl.ds(s, rows, stride=ROW_TILE), :] for s in range(ROW_TILE)], axis=1)


def _store_tile_rows(ref, val):
    rows = val.shape[0]
    for s in range(ROW_TILE):
        ref[pl.ds(s, rows, stride=ROW_TILE), :] = val[:, s * LANES:(s + 1) * LANES]


def _first_index(hit, lane):
    return jnp.min(jnp.where(hit, lane.astype(F32), float(LANES)), axis=1, keepdims=True).astype(I32)


def _memattn_kernel(q_ref, k_ref, v_ref, h_ref, wo_ref, g_ref, wr_ref, br_ref,
                    h2_ref, xn_ref, ri_ref, rg_ref, cnt_ref, carry_ref, *, tq):
    @pl.when((pl.program_id(0) == 0) & (pl.program_id(1) == 0))
    def _():
        carry_ref[...] = jnp.zeros(carry_ref.shape, F32)

    q = q_ref[...]
    outs = []
    for h in range(MEM_HEADS):
        sl = slice(h * MEM_HEAD_DIM, (h + 1) * MEM_HEAD_DIM)
        s = lax.dot_general(q[:, sl], k_ref[:, sl], (((1,), (1,)), ((), ())), preferred_element_type=F32)
        p = jnp.exp2(s - jnp.max(s, axis=1, keepdims=True))
        o = jnp.dot(p.astype(BF16), v_ref[:, sl], preferred_element_type=F32)
        outs.append((o / jnp.sum(p, axis=1, keepdims=True)).astype(BF16))
    h2 = h_ref[...] + jnp.dot(jnp.concatenate(outs, axis=1), wo_ref[...], preferred_element_type=F32)
    h2_ref[...] = h2
    xn_f32 = _rms(h2, g_ref[...])
    _store_tile_rows(xn_ref, xn_f32)
    xn = xn_f32.astype(BF16)

    logits = jnp.dot(xn, wr_ref[...], preferred_element_type=F32) + br_ref[...]
    lane = lax.broadcasted_iota(I32, (tq, LANES), 1)
    g_log = jnp.where((lane >= GROUP_LANE0) & (lane < GROUP_LANE0 + N_GROUPS), logits, NEG)
    g_max = jnp.max(g_log, axis=1, keepdims=True)
    g_idx = _first_index(g_log == g_max, lane) - GROUP_LANE0
    g_p = 1.0 / jnp.sum(jnp.exp(g_log - g_max), axis=1, keepdims=True)
    in_group = (lane >= g_idx * EXPERTS_PER_GROUP) & (lane < (g_idx + 1) * EXPERTS_PER_GROUP)
    e_log = jnp.where(in_group, logits, NEG)
    e_exp = jnp.exp(e_log - jnp.max(e_log, axis=1, keepdims=True))
    prob = jnp.where(in_group, e_exp / jnp.sum(e_exp, axis=1, keepdims=True), -1.0)
    p1 = jnp.max(prob, axis=1, keepdims=True)
    i1 = _first_index(prob == p1, lane)
    rest = jnp.where(lane == i1, -1.0, prob)
    p2 = jnp.max(rest, axis=1, keepdims=True)
    i2 = _first_index(rest == p2, lane)
    top_sum = p1 + p2

    hot1, hot2 = lane == i1, lane == i2
    onehot = jnp.where(hot1 | hot2, 1.0, 0.0)
    r_i = lax.broadcasted_iota(I32, (tq, tq), 0)
    c_i = lax.broadcasted_iota(I32, (tq, tq), 1)
    before = jnp.dot(jnp.where(c_i < r_i, 1.0, 0.0).astype(BF16), onehot.astype(BF16),
                     preferred_element_type=F32) + carry_ref[0:1, :]
    r1 = jnp.sum(jnp.where(hot1, before, 0.0), axis=1, keepdims=True)
    r2 = jnp.sum(jnp.where(hot2, before, 0.0), axis=1, keepdims=True)
    carry_ref[...] = carry_ref[...] + jnp.sum(onehot, axis=0, keepdims=True)
    cnt_ref[...] = carry_ref[...]

    l8 = lax.broadcasted_iota(I32, (tq, 8), 1)
    ri_ref[...] = jnp.where(l8 == 0, i1, jnp.where(l8 == 1, i2, jnp.where(
        l8 == 2, r1.astype(I32), jnp.where(l8 == 3, r2.astype(I32), 0))))
    rg_ref[...] = jnp.where(l8 == 0, g_p * p1 / top_sum, jnp.where(l8 == 1, g_p * p2 / top_sum, 0.0))


def _memattn(qm, mk, mv, h1, w_mo, g_ffn, w_router, b_router, *, b0, b, t, tq):
    d = h1.shape[1]
    n = b * t
    n_q = t // tq
    row = lambda bi, i: (bi * n_q + i, 0)
    src_row = lambda bi, i: ((b0 + bi) * n_q + i, 0)
    const = lambda bi, i: (0, 0)
    mem = pl.BlockSpec((None, MEM_LEN, d), lambda bi, i: (b0 + bi, 0, 0))
    src = pl.BlockSpec((tq, d), src_row)
    blk = pl.BlockSpec((tq, d), row)
    tiled = pl.BlockSpec((tq * ROW_TILE, LANES), row)
    small = pl.BlockSpec((tq, 8), row)
    return pl.pallas_call(
        functools.partial(_memattn_kernel, tq=tq),
        grid=(b, n_q),
        in_specs=[src, mem, mem, src, pl.BlockSpec((d, d), const), pl.BlockSpec((1, d), const),
                  pl.BlockSpec((d, LANES), const), pl.BlockSpec((1, LANES), const)],
        out_specs=[blk, tiled, small, small, pl.BlockSpec((8, LANES), const)],
        out_shape=[jax.ShapeDtypeStruct((n, d), F32), jax.ShapeDtypeStruct((n * ROW_TILE, LANES), F32),
                   jax.ShapeDtypeStruct((n, 8), I32), jax.ShapeDtypeStruct((n, 8), F32),
                   jax.ShapeDtypeStruct((8, LANES), F32)],
        scratch_shapes=[pltpu.VMEM((8, LANES), F32)],
        compiler_params=_params("arbitrary", "arbitrary"),
    )(qm, mk, mv, h1, w_mo, g_ffn.reshape(1, d), w_router, b_router)


SC_CORES, SC_SUBCORES = 2, 16
SC_WINDOW = 128


def _sc_mesh():
    return plsc.VectorSubcoreMesh(core_axis_name="core", subcore_axis_name="subcore")


def _sc_split(n_windows):
    if n_windows % (SC_CORES * SC_SUBCORES) == 0:
        return ("core", "subcore")
    assert n_windows % SC_SUBCORES == 0
    return "subcore"


def _sc_scatter_rows(src, idx, n_out):
    rows = src.shape[0]
    win = SC_WINDOW
    n_src_blocks = rows // win

    @functools.partial(pl.kernel, out_type=jax.ShapeDtypeStruct((n_out, LANES), src.dtype),
                       mesh=_sc_mesh(), scratch_types=[])
    def scatter(x_hbm, i_hbm, o_hbm):
        def body(x_vmem, i_vmem):
            pltpu.sync_copy(x_vmem, o_hbm.at[i_vmem.at[0]])

        pltpu.emit_pipeline(
            body, grid=(n_out // win,),
            in_specs=[pl.BlockSpec((win, LANES), lambda i: (i % n_src_blocks, 0)),
                      pl.BlockSpec((1, win), lambda i: (0, i))],
            out_specs=[],
            core_axis_name=_sc_split(n_out // win), dimension_semantics=(pltpu.PARALLEL,),
        )(x_hbm, i_hbm)

    return scatter(src, idx.reshape(1, n_out))


def _sc_gather_rows(src, idx):
    n_out = idx.shape[0]
    win = SC_WINDOW

    @functools.partial(pl.kernel, out_type=jax.ShapeDtypeStruct((n_out, LANES), src.dtype),
                       mesh=_sc_mesh(), scratch_types=[])
    def gather(x_hbm, i_hbm, o_hbm):
        def body(i_vmem, o_vmem):
            pltpu.sync_copy(x_hbm.at[i_vmem.at[0]], o_vmem)

        pltpu.emit_pipeline(
            body, grid=(n_out // win,),
            in_specs=[pl.BlockSpec((1, win), lambda i: (0, i))],
            out_specs=[pl.BlockSpec((win, LANES), lambda i: (i, 0))],
            core_axis_name=_sc_split(n_out // win), dimension_semantics=(pltpu.PARALLEL,),
        )(i_hbm, o_hbm)

    return gather(src, idx.reshape(1, n_out))


def _expert_kernel(blk_ref, e_ref, lo_ref, hi_ref, x_ref, w1_ref, w3_ref, w2_ref, y_ref):
    w = pl.program_id(0)
    lo, hi = lo_ref[w], hi_ref[w]
    row0 = blk_ref[w] * MOE_ROWS

    def compute():
        x = _load_tile_rows(x_ref).astype(BF16)
        a = jnp.dot(x, w1_ref[...], preferred_element_type=F32)
        g = jnp.dot(x, w3_ref[...], preferred_element_type=F32)
        mid = ((a / (1.0 + jnp.exp(-a))) * g).astype(BF16)
        y = jnp.dot(mid, w2_ref[...], preferred_element_type=F32)
        rows = row0 + lax.broadcasted_iota(I32, y.shape, 0)
        return y, (rows >= lo) & (rows < hi)

    @pl.when((hi > lo) & (lo == row0))
    def _():
        y, mine = compute()
        _store_tile_rows(y_ref, jnp.where(mine, y, 0.0))

    @pl.when((hi > lo) & (lo != row0))
    def _():
        y, mine = compute()
        _store_tile_rows(y_ref, jnp.where(mine, y, _load_tile_rows(y_ref)))


def _experts(items, xs, w1, w3, w2):
    d = D_MODEL
    n_items = items[0].shape[0]
    xmap = lambda w, blk, e, lo, hi: (blk[w], 0)
    wmap = lambda w, blk, e, lo, hi: (e[w], 0, 0)
    return pl.pallas_call(
        _expert_kernel,
        grid_spec=pltpu.PrefetchScalarGridSpec(
            num_scalar_prefetch=4, grid=(n_items,),
            in_specs=[pl.BlockSpec((MOE_ROWS * ROW_TILE, LANES), xmap),
                      pl.BlockSpec((None, d, D_EXPERT), wmap), pl.BlockSpec((None, d, D_EXPERT), wmap),
                      pl.BlockSpec((None, D_EXPERT, d), wmap)],
            out_specs=pl.BlockSpec((MOE_ROWS * ROW_TILE, LANES), xmap)),
        out_shape=jax.ShapeDtypeStruct(xs.shape, F32),
        compiler_params=_params("arbitrary"),
    )(*items, xs, w1, w3, w2)


def _work_items(counts, n_rows):
    n_blocks = n_rows // MOE_ROWS
    n_items = n_blocks + N_EXPERTS - 1
    ends = jnp.cumsum(counts)
    starts = ends - counts
    first_blk = starts // MOE_ROWS
    n_blk = jnp.where(counts > 0, (ends - 1) // MOE_ROWS - first_blk + 1, 0)
    item_end = jnp.cumsum(n_blk)
    total = item_end[-1]
    w = jnp.arange(n_items, dtype=I32)
    wc = jnp.minimum(w, total - 1)
    e = jnp.sum(item_end[None, :] <= wc[:, None], axis=1).astype(I32)
    blk = (first_blk[e] + wc - (item_end[e] - n_blk[e])).astype(I32)
    lo = jnp.maximum(starts[e], blk * MOE_ROWS).astype(I32)
    hi = jnp.minimum(ends[e], (blk + 1) * MOE_ROWS).astype(I32)
    live = w < total
    return blk, e, jnp.where(live, lo, 0), jnp.where(live, hi, 0), starts


def _combine_kernel(h_ref, z0_ref, z1_ref, rg_ref, g_ref, *rest):
    y_ref = rest[-1]
    rg = rg_ref[...]
    moe = _load_tile_rows(z0_ref) * rg[:, 0:1] + _load_tile_rows(z1_ref) * rg[:, 1:2]
    y_ref[...] = _rms(h_ref[...] + moe, g_ref[...])


def _combine(h2, z, rg, g_final, tm, y_prev, row0, n_total):
    n, d = h2.shape
    n_t = n // tm
    row = lambda i: (i, 0)
    in_specs = [pl.BlockSpec((tm, d), row), pl.BlockSpec((tm * ROW_TILE, LANES), row),
                pl.BlockSpec((tm * ROW_TILE, LANES), lambda i: (n_t + i, 0)), pl.BlockSpec((tm, 8), row),
                pl.BlockSpec((1, d), lambda i: (0, 0))]
    args = [h2, z, z, rg, g_final.reshape(1, d)]
    aliases = {}
    if y_prev is not None:
        in_specs.append(pl.BlockSpec(memory_space=pl.ANY))
        args.append(y_prev)
        aliases = {len(args) - 1: 0}
    return pl.pallas_call(
        _combine_kernel, grid=(n_t,),
        in_specs=in_specs,
        out_specs=pl.BlockSpec((tm, d), lambda i: (row0 // tm + i, 0)),
        out_shape=jax.ShapeDtypeStruct((n_total, d), F32),
        input_output_aliases=aliases,
        compiler_params=_params("arbitrary"),
    )(*args)


def _moe_and_final(h2, xn, ri, rg, counts, w1, w3, w2, g_final, *, tm, y_prev, row0, n_total):
    n = h2.shape[0]
    blk, e, lo, hi, starts = _work_items(counts, 2 * n)
    pos = (starts[ri[:, 0:2]] + ri[:, 2:4]).astype(I32)
    sub = (pos.T[:, :, None] * ROW_TILE + jnp.arange(ROW_TILE, dtype=I32)).reshape(2 * n * ROW_TILE)
    xs = _sc_scatter_rows(xn, sub, 2 * n * ROW_TILE)
    ys = _experts((blk, e, lo, hi), xs, w1, w3, w2)
    z = _sc_gather_rows(ys, sub)
    return _combine(h2, z, rg, g_final, tm, y_prev, row0, n_total)


def _pad_time(a, s_pad):
    return jnp.pad(a, ((0, 0), (0, s_pad - a.shape[1])) + ((0, 0),) * (a.ndim - 2))


def _layer(x, past, mem_k, mem_v, wts, lam_init, g_final, *, tm, tq, tk, moe_parts):
    b, t, d = x.shape
    n = b * t
    p_len = 0 if past is None else past[0].shape[1]
    s_valid = p_len + t
    s_pad = -(-s_valid // tk) * tk
    pos = p_len + jnp.arange(t)

    fk, fv, logf, dk, dv, qf, kf, vf, qd, kd, vd = _inproj(
        x, wts['g_mix'], wts['w_main'], wts['w_ff'], wts['b_forget'], pos, tm)
    time_major = tm <= t

    def leaf(a, dims):
        if time_major:
            return jnp.moveaxis(a.reshape((b,) + dims + (t,)), -1, 1)
        return a.reshape((b, t) + dims)

    new = (leaf(fk, (FOX_HEADS, HEAD_DIM)), leaf(fv, (FOX_HEADS, HEAD_DIM)), leaf(logf, (FOX_HEADS,)),
           leaf(dk, (DIFF_HEADS, 2, HEAD_DIM)), dv.reshape(b, t, DIFF_HEADS, 2 * HEAD_DIM))

    def keys_t(cur, old_rows):
        if not time_major:
            cur = jnp.swapaxes(cur.reshape(b, t, FOX_W), 1, 2)
        if old_rows is not None:
            prev = jnp.swapaxes(old_rows.reshape(b, p_len, FOX_W).astype(BF16), 1, 2)
            cur = jnp.concatenate([prev, cur], axis=2)
        return jnp.pad(cur, ((0, 0), (0, 0), (0, s_pad - s_valid)))

    def with_past(new_b16, old_rows):
        cur = new_b16.reshape(b, t, FOX_W)
        if old_rows is not None:
            cur = jnp.concatenate([old_rows.reshape(b, p_len, FOX_W).astype(BF16), cur], axis=1)
        return _pad_time(cur, s_pad)

    logf_t = logf if time_major else jnp.swapaxes(logf.reshape(b, t, FOX_HEADS), 1, 2)
    if past is not None:
        logf_t = jnp.concatenate([jnp.swapaxes(past[2].astype(F32), 1, 2), logf_t], axis=2)
    c2 = _cumsum_time(jnp.pad(logf_t, ((0, 0), (0, 0), (0, s_pad - s_valid))))
    k_bias = _fox_bias_lanes(c2, q_side=False)
    q_bias = _fox_bias_lanes(jnp.swapaxes(c2[:, :, p_len:p_len + t], 1, 2), q_side=True)

    old = (None,) * 5 if past is None else past
    geom = dict(b=b, t=t, s_pad=s_pad, s_valid=s_valid, past=p_len, tq=tq, tk=tk, lam_init=lam_init)
    fox_o = _self_attention(qf, keys_t(kf, old[0]), with_past(vf, old[1]), (q_bias, k_bias),
                            fox=True, **geom)
    diff_o = _self_attention(qd, keys_t(kd, old[3]), with_past(vd, old[4]),
                             (wts['lam_vecs'], wts['g_diff']), fox=False, **geom)
    h1, qm = _outproj(x.reshape(n, d), fox_o, diff_o, wts['wo_f'], wts['wo_d'], wts['g_mem_q'],
                      wts['w_mq'], tm)
    assert b % moe_parts == 0
    bp = b // moe_parts
    y = None
    for part in range(moe_parts):
        h2, xn, ri, rg, cnt = _memattn(qm, mem_k, mem_v, h1, wts['w_mo'], wts['g_ffn'], wts['w_router'],
                                       wts['b_router'], b0=part * bp, b=bp, t=t, tq=tq)
        counts = cnt[0, :N_EXPERTS].astype(I32)
        y = _moe_and_final(h2, xn, ri, rg, counts, wts['w1'], wts['w3'], wts['w2'], g_final, tm=tm,
                           y_prev=y, row0=part * bp * t, n_total=n)
    return y.reshape(b, t, d), new


def kernel(x_prompt, x_sample, cache_fox_k, cache_fox_v, cache_fox_logf, cache_diff_k, cache_diff_v, cache_mem_k, cache_mem_v, mem_prompt, g_mix, w_in, b_forget, lam_q1, lam_k1, lam_q2, lam_k2, g_diff, w_o, g_mem_q, g_mem_kv, w_mq, w_mk, w_mv, w_mo, g_ffn, w_group, b_group, w_erouter, b_erouter, w1, w3, w2, g_final):
    depth = w_in.shape[0]
    assert depth == 1
    l = 0
    lam_init = 0.8 - 0.6 * math.exp(-0.3 * l)
    d = D_MODEL
    w = w_in[l]
    ff0 = 3 * FOX_W
    w_router = jnp.zeros((d, LANES), F32).at[:, :N_EXPERTS].set(w_erouter[l])
    w_router = w_router.at[:, GROUP_LANE0:GROUP_LANE0 + N_GROUPS].set(w_group[l])
    b_router = jnp.zeros((1, LANES), F32).at[0, :N_EXPERTS].set(b_erouter[l])
    b_router = b_router.at[0, GROUP_LANE0:GROUP_LANE0 + N_GROUPS].set(b_group[l])
    wts = {
        'g_mix': g_mix[l],
        'w_main': jnp.concatenate([w[:, :ff0], w[:, ff0 + FOX_HEADS:]], axis=1).astype(BF16),
        'w_ff': jnp.pad(w[:, ff0:ff0 + FOX_HEADS], ((0, 0), (0, LANES - FOX_HEADS))).astype(BF16),
        'b_forget': b_forget[l],
        'lam_vecs': jnp.stack([lam_q1[l], lam_k1[l], lam_q2[l], lam_k2[l]]),
        'g_diff': g_diff[l],
        'wo_f': w_o[l][:FOX_W].astype(BF16), 'wo_d': w_o[l][FOX_W:].astype(BF16),
        'g_mem_q': g_mem_q[l], 'w_mq': w_mq[l].astype(BF16), 'w_mo': w_mo[l].astype(BF16),
        'g_ffn': g_ffn[l], 'w_router': w_router.astype(BF16), 'b_router': b_router,
        'w1': w1[l].astype(BF16), 'w3': w3[l].astype(BF16), 'w2': w2[l].astype(BF16),
    }

    bp, tp, _ = x_prompt.shape
    mk, mv, mk_b, mv_b = _memkv(mem_prompt.reshape(bp * MEM_LEN, d), g_mem_kv[l],
                                w_mk[l].astype(BF16), w_mv[l].astype(BF16), 512)
    yp, new_p = _layer(x_prompt, None, mk_b.reshape(bp, MEM_LEN, d), mv_b.reshape(bp, MEM_LEN, d),
                       wts, lam_init, g_final, tm=512, tq=512, tk=512, moe_parts=2)

    bs, ts, _ = x_sample.shape
    past = (cache_fox_k[l], cache_fox_v[l], cache_fox_logf[l], cache_diff_k[l], cache_diff_v[l])
    ys, new_s = _layer(x_sample, past, cache_mem_k[l].reshape(bs, MEM_LEN, d).astype(BF16),
                       cache_mem_v[l].reshape(bs, MEM_LEN, d).astype(BF16),
                       wts, lam_init, g_final, tm=bs * ts, tq=ts, tk=512, moe_parts=1)

    mem_shape = (1, bp, MEM_LEN, MEM_HEADS, MEM_HEAD_DIM)
    return (yp, ys) + tuple(a[None] for a in new_p) + (mk.reshape(mem_shape), mv.reshape(mem_shape)) \
        + tuple(a[None] for a in new_s)
```

```python
import functools
import math

import jax
import jax.numpy as jnp
from jax import lax
from jax.experimental import pallas as pl
from jax.experimental.pallas import tpu as pltpu
from jax.experimental.pallas import tpu_sc as plsc

F32 = jnp.float32
BF16 = jnp.bfloat16
I32 = jnp.int32

D_MODEL = 1024
HEAD_DIM = 64
FOX_HEADS = 8
DIFF_HEADS = 4
FOX_W = FOX_HEADS * HEAD_DIM
DIFF_W = DIFF_HEADS * 2 * HEAD_DIM
ROT_DIM = HEAD_DIM // 4
ROPE_THETA = 500000.0
CHUNK = 64
MEM_LEN = 256
MEM_HEADS = 4
MEM_HEAD_DIM = D_MODEL // MEM_HEADS
N_GROUPS = 4
EXPERTS_PER_GROUP = 8
N_EXPERTS = N_GROUPS * EXPERTS_PER_GROUP
D_EXPERT = 512
EPS = 1e-6

LANES = 128
LOG2E = 1.4426950408889634
NEG = -1e30
VMEM_LIMIT_BYTES = 56 * 1024 * 1024
MOE_ROWS = 256
GROUP_LANE0 = N_EXPERTS


def _params(*sem):
    return pltpu.CompilerParams(dimension_semantics=sem, vmem_limit_bytes=VMEM_LIMIT_BYTES)


def _rms(x, g):
    return (x * lax.rsqrt(jnp.mean(x * x, axis=-1, keepdims=True) + EPS)) * g


def _rope(x, cos, sin_lo, sin_hi):
    outs = []
    for c in range(x.shape[1] // LANES):
        xc = x[:, c * LANES:(c + 1) * LANES]
        up = pltpu.roll(xc, LANES - ROT_DIM // 2, axis=1)
        dn = pltpu.roll(xc, ROT_DIM // 2, axis=1)
        outs.append(xc * cos + up * sin_lo + dn * sin_hi)
    return jnp.concatenate(outs, axis=1)


def _inproj_kernel(x_ref, g_ref, w_ref, wf_ref, bf_ref, cos_ref, slo_ref, shi_ref,
                   fk_ref, fv_ref, logf_ref, dk_ref, dv_ref,
                   qf_ref, kf_ref, vf_ref, qd_ref, kd_ref, vd_ref, *, q_scale, time_major):
    n = _rms(x_ref[...], g_ref[...]).astype(BF16)
    turn = (lambda a: a.T) if time_major else (lambda a: a)

    def proj(c):
        return jnp.dot(n, w_ref[:, c * FOX_W:(c + 1) * FOX_W], preferred_element_type=F32)

    cos, slo, shi = cos_ref[...], slo_ref[...], shi_ref[...]
    qf_ref[...] = (proj(0) * q_scale).astype(BF16)
    fk = turn(proj(1))
    fk_ref[...] = fk
    kf_ref[...] = fk.astype(BF16)
    fv = proj(2)
    fv_ref[...] = turn(fv)
    vf_ref[...] = fv.astype(BF16)
    qd_ref[...] = (_rope(proj(3), cos, slo, shi) * q_scale).astype(BF16)
    dk = turn(_rope(proj(4), cos, slo, shi))
    dk_ref[...] = dk
    kd_ref[...] = dk.astype(BF16)
    dv = proj(5)
    for h in range(DIFF_HEADS):
        dv_ref[pl.ds(h, dv.shape[0], stride=DIFF_HEADS), :] = dv[:, h * LANES:(h + 1) * LANES]
    vd_ref[...] = dv.astype(BF16)
    z = jnp.dot(n, wf_ref[...], preferred_element_type=F32) + bf_ref[...]
    logf = jnp.minimum(z, 0.0) - jnp.log1p(jnp.exp(-jnp.abs(z)))
    logf_ref[...] = logf.T[:FOX_HEADS] if time_major else logf[:, :FOX_HEADS]


def _rope_tables(pos):
    half = ROT_DIM // 2
    inv = jnp.power(ROPE_THETA, -jnp.arange(half, dtype=F32) / half)
    ang = pos.astype(F32)[:, None] * inv[None, :]
    cos, sin = jnp.cos(ang), jnp.sin(ang)
    t = pos.shape[0]
    pad = jnp.zeros((t, HEAD_DIM - ROT_DIM), F32)
    zero = jnp.zeros((t, half), F32)
    cos64 = jnp.concatenate([cos, cos, pad + 1.0], axis=1)
    slo64 = jnp.concatenate([-sin, zero, pad], axis=1)
    shi64 = jnp.concatenate([zero, sin, pad], axis=1)
    return tuple(jnp.tile(a, (1, LANES // HEAD_DIM)) for a in (cos64, slo64, shi64))


def _inproj(x, g, w_main, w_ff, b_ff, pos, tm):
    b, t, d = x.shape
    n = b * t
    tables = _rope_tables(pos)
    time_major = tm <= t
    if time_major:
        assert t % tm == 0
        per = t // tm
        tab_map = lambda i: (i % per, 0)
        turned = pl.BlockSpec((None, FOX_W, tm), lambda i: (i // per, 0, i % per))
        gate = pl.BlockSpec((None, FOX_HEADS, tm), lambda i: (i // per, 0, i % per))
        turned_shape, gate_shape = (b, FOX_W, t), (b, FOX_HEADS, t)
    else:
        assert tm % t == 0
        tables = tuple(jnp.tile(a, (tm // t, 1)) for a in tables)
        tab_map = lambda i: (0, 0)
    assert n % tm == 0
    row = lambda i: (i, 0)
    const = lambda i: (0, 0)
    wide = pl.BlockSpec((tm, FOX_W), row)
    if not time_major:
        turned, gate = wide, pl.BlockSpec((tm, FOX_HEADS), row)
        turned_shape, gate_shape = (n, FOX_W), (n, FOX_HEADS)
    tab = pl.BlockSpec((tm, LANES), tab_map)
    sds = jax.ShapeDtypeStruct
    return pl.pallas_call(
        functools.partial(_inproj_kernel, q_scale=HEAD_DIM ** -0.5 * LOG2E, time_major=time_major),
        grid=(n // tm,),
        in_specs=[pl.BlockSpec((tm, d), row), pl.BlockSpec((1, d), const),
                  pl.BlockSpec(w_main.shape, const), pl.BlockSpec(w_ff.shape, const),
                  pl.BlockSpec((1, LANES), const), tab, tab, tab],
        out_specs=[turned, turned, gate, turned, pl.BlockSpec((tm * DIFF_HEADS, LANES), row),
                   wide, turned, wide, wide, turned, wide],
        out_shape=[sds(turned_shape, F32), sds(turned_shape, F32), sds(gate_shape, F32),
                   sds(turned_shape, F32), sds((n * DIFF_HEADS, LANES), F32),
                   sds((n, FOX_W), BF16), sds(turned_shape, BF16), sds((n, FOX_W), BF16),
                   sds((n, FOX_W), BF16), sds(turned_shape, BF16), sds((n, FOX_W), BF16)],
        compiler_params=_params("arbitrary"),
    )(x.reshape(n, d), g.reshape(1, d), w_main, w_ff,
      jnp.pad(b_ff.reshape(1, FOX_HEADS), ((0, 0), (0, LANES - FOX_HEADS))), *tables)


AUX_LANES = 8


def _bf16_part(x):
    return pltpu.bitcast(pltpu.bitcast(x, jnp.uint32) & jnp.uint32(0xFFFF0000), F32)


def _cumsum_kernel(x_ref, kb_ref, qb_ref):
    x = x_ref[...]
    s_len = x.shape[1]
    lane = lax.broadcasted_iota(I32, x.shape, 1)
    shift = 1
    while shift < s_len:
        x = x + jnp.where(lane >= shift, pltpu.roll(x, shift, axis=1), 0.0)
        shift *= 2
    c = x * LOG2E
    hi = _bf16_part(c)
    mid = _bf16_part(c - hi)
    lo = _bf16_part(c - hi - mid)
    slot = lax.broadcasted_iota(I32, (LANES, x.shape[0]), 0)
    head = lax.broadcasted_iota(I32, (LANES, x.shape[0]), 1)

    def place(first_slot):
        out = jnp.zeros((LANES, s_len), F32)
        for i, part in enumerate((hi, mid, lo)):
            sel = jnp.where(slot == head * AUX_LANES + first_slot + i, 1.0, 0.0).astype(BF16)
            out = out + jnp.dot(sel, part.astype(BF16), preferred_element_type=F32)
        return out

    slot_col = lax.broadcasted_iota(I32, (LANES, s_len), 0)
    used = slot_col < FOX_HEADS * AUX_LANES
    within = slot_col % AUX_LANES
    kb_ref[...] = (jnp.where(used & (within < 3), 1.0, 0.0) - place(3)).astype(BF16)
    qb_ref[...] = (jnp.where(used & (within >= 3) & (within < 6), 1.0, 0.0) + place(0)).astype(BF16)


def _fox_bias(logf_t):
    b, h, s = logf_t.shape
    x = jnp.pad(logf_t, ((0, 0), (0, 16 - h), (0, 0)))
    out = pl.BlockSpec((None, LANES, s), lambda i: (i, 0, 0))
    return pl.pallas_call(
        _cumsum_kernel, grid=(b,), in_specs=[pl.BlockSpec((None, 16, s), lambda i: (i, 0, 0))],
        out_specs=[out, out],
        out_shape=[jax.ShapeDtypeStruct((b, LANES, s), BF16)] * 2,
        compiler_params=_params("arbitrary"),
    )(x)


def _attn_kernel(*refs, tq, tk, past, s_valid, fox, lam_init):
    if fox:
        q_ref, qa_ref, k_ref, ka_ref, v_ref, o_ref, sa_ref, sb_ref, mx_ref, m_ref, l_ref, acc_ref = refs
    else:
        q_ref, k_ref, v_ref, lam_ref, g_ref, o_ref, sa_ref, sb_ref, mx_ref, m_ref, l_ref, acc_ref = refs
    i = pl.program_id(2)
    q = q_ref[...]
    lane = lax.broadcasted_iota(I32, (tq, LANES), 1)
    low = lane < HEAD_DIM
    zero = jnp.zeros_like(q)
    q_part = [jnp.where(low, q, zero), jnp.where(low, zero, q)]
    if fox:
        qa = qa_ref[...]
        for h in range(2):
            lane0 = (2 * pl.program_id(1) + h) * AUX_LANES
            aux = jnp.where((lane >= lane0) & (lane < lane0 + AUX_LANES), qa, zero)
            q_part[h] = jnp.concatenate([q_part[h], aux], axis=1)
    m_ref[...] = jnp.full(m_ref.shape, NEG, F32)
    l_ref[...] = jnp.zeros(l_ref.shape, F32)
    acc_ref[...] = jnp.zeros(acc_ref.shape, F32)
    q0 = past + i * tq
    nb = q0 // tk
    bufs = (sa_ref, sb_ref)

    def scores(j, buf):
        ks = pl.multiple_of(j * tk, tk)
        k = k_ref[:, pl.ds(ks, tk)]
        if fox:
            k = jnp.concatenate([k, ka_ref[:, pl.ds(ks, tk)]], axis=0)
        for h in range(2):
            s = jnp.dot(q_part[h], k, preferred_element_type=F32)
            bufs[buf][h] = s
            mx_ref[buf, h] = jnp.broadcast_to(jnp.max(s, axis=1, keepdims=True), (tq, LANES))

    def consume(j, buf, use_mask):
        ks = pl.multiple_of(j * tk, tk)
        v = jnp.concatenate([v_ref[pl.ds(ks, tk), :], jnp.ones((tk, LANES), BF16)], axis=1)
        pv, alpha = [], []
        for h in range(2):
            t = bufs[buf][h]
            if use_mask:
                qpos = q0 + lax.broadcasted_iota(I32, (tq, tk), 0)
                kpos = j * tk + lax.broadcasted_iota(I32, (tq, tk), 1)
                if fox:
                    seen = kpos <= qpos
                else:
                    seen = kpos < jnp.minimum((qpos // CHUNK + 1) * CHUNK, s_valid)
                t = jnp.where(seen, t, NEG)
                mx = jnp.max(t, axis=1, keepdims=True)
            else:
                mx = mx_ref[buf, h]
            m_prev = m_ref[h]
            m_new = jnp.maximum(m_prev, mx)
            a = jnp.exp2(m_prev - m_new)
            p = jnp.exp2(t - jnp.concatenate([m_new] * (tk // LANES), axis=1))
            m_ref[h] = m_new
            both = jnp.dot(p.astype(BF16), v, preferred_element_type=F32)
            pv.append(both[:, :LANES])
            l_ref[h] = a * l_ref[h] + both[:, LANES:]
            alpha.append(a)
        if fox:
            acc_ref[0] = acc_ref[0] * jnp.where(low, alpha[0], alpha[1]) + jnp.where(low, pv[0], pv[1])
        else:
            for h in range(2):
                acc_ref[h] = acc_ref[h] * alpha[h] + pv[h]

    scores(0, 0)

    def pair(jj, c):
        j = 2 * jj
        scores(j + 1, 1)
        consume(j, 0, False)
        scores(j + 2, 0)
        consume(j + 1, 1, False)
        return c

    lax.fori_loop(0, nb // 2, pair, 0)

    @pl.when(nb % 2 == 1)
    def _():
        scores(nb, 1)
        consume(nb - 1, 0, False)
        consume(nb, 1, True)

    @pl.when(nb % 2 == 0)
    def _():
        consume(nb, 0, True)

    if fox:
        o = acc_ref[0] / jnp.where(low, l_ref[0], l_ref[1])
    else:
        lam_v = lam_ref[...]
        lam = (jnp.exp(jnp.sum(lam_v[0:1] * lam_v[1:2], axis=1, keepdims=True))
               - jnp.exp(jnp.sum(lam_v[2:3] * lam_v[3:4], axis=1, keepdims=True)) + lam_init)
        o = acc_ref[0] / l_ref[0] - lam * (acc_ref[1] / l_ref[1])
        o = _rms(o, g_ref[...]) * (1.0 - lam_init)
    o_ref[...] = o.astype(o_ref.dtype)


def _self_attention(q, k, v, extra, *, b, t, s_pad, s_valid, past, tq, tk, fox, lam_init):
    assert tk % tq == 0 and past % tq == 0 and t % tq == 0 and tk % CHUNK == 0 and s_pad % tk == 0
    assert tq % CHUNK == 0 or (CHUNK % tq == 0 and past % CHUNK == 0)
    n_q = t // tq
    k_spec = pl.BlockSpec((None, LANES, s_pad), lambda bi, p, i: (bi, p, 0))
    v_spec = pl.BlockSpec((None, s_pad, LANES), lambda bi, p, i: (bi, 0, p))
    q_spec = pl.BlockSpec((tq, LANES), lambda bi, p, i: (bi * n_q + i, p))
    const = lambda bi, p, i: (0, 0)
    if fox:
        in_specs = [q_spec, pl.BlockSpec((tq, LANES), lambda bi, p, i: (bi * n_q + i, 0)), k_spec,
                    pl.BlockSpec((None, LANES, s_pad), lambda bi, p, i: (bi, 0, 0)), v_spec]
        args = (q, extra[0], k, extra[1], v)
    else:
        in_specs = [q_spec, k_spec, v_spec, pl.BlockSpec((4, HEAD_DIM), const), pl.BlockSpec((1, LANES), const)]
        args = (q, k, v, extra[0], extra[1].reshape(1, LANES))
    stat = pltpu.VMEM((2, tq, LANES), F32)
    return pl.pallas_call(
        functools.partial(_attn_kernel, tq=tq, tk=tk, past=past, s_valid=s_valid, fox=fox, lam_init=lam_init),
        grid=(b, FOX_W // LANES, n_q),
        in_specs=in_specs,
        out_specs=q_spec,
        out_shape=jax.ShapeDtypeStruct((b * t, FOX_W), BF16),
        scratch_shapes=[pltpu.VMEM((2, tq, tk), F32), pltpu.VMEM((2, tq, tk), F32),
                        pltpu.VMEM((2, 2, tq, LANES), F32), stat, stat, stat],
        compiler_params=_params("arbitrary", "arbitrary", "arbitrary"),
    )(*args)


def _outproj_kernel(x_ref, fo_ref, do_ref, wof_ref, wod_ref, g_ref, wq_ref, h_ref, q_ref, *, q_scale):
    h = (x_ref[...] + jnp.dot(fo_ref[...], wof_ref[...], preferred_element_type=F32)
         + jnp.dot(do_ref[...], wod_ref[...], preferred_element_type=F32))
    h_ref[...] = h
    n = _rms(h, g_ref[...]).astype(BF16)
    q_ref[...] = (jnp.dot(n, wq_ref[...], preferred_element_type=F32) * q_scale).astype(BF16)


def _outproj(x2, fox_o, diff_o, wo_f, wo_d, g_mem_q, w_mq, tm):
    n, d = x2.shape
    row = lambda i: (i, 0)
    const = lambda i: (0, 0)
    return pl.pallas_call(
        functools.partial(_outproj_kernel, q_scale=MEM_HEAD_DIM ** -0.5 * LOG2E),
        grid=(n // tm,),
        in_specs=[pl.BlockSpec((tm, d), row), pl.BlockSpec((tm, FOX_W), row),
                  pl.BlockSpec((tm, DIFF_W), row), pl.BlockSpec((FOX_W, d), const),
                  pl.BlockSpec((DIFF_W, d), const), pl.BlockSpec((1, d), const),
                  pl.BlockSpec((d, d), const)],
        out_specs=[pl.BlockSpec((tm, d), row), pl.BlockSpec((tm, d), row)],
        out_shape=[jax.ShapeDtypeStruct((n, d), F32), jax.ShapeDtypeStruct((n, d), BF16)],
        compiler_params=_params("arbitrary"),
    )(x2, fox_o, diff_o, wo_f, wo_d, g_mem_q.reshape(1, d), w_mq)


def _memkv_kernel(m_ref, g_ref, wk_ref, wv_ref, k_ref, v_ref, kb_ref, vb_ref):
    n = _rms(m_ref[...], g_ref[...]).astype(BF16)
    k = jnp.dot(n, wk_ref[...], preferred_element_type=F32)
    v = jnp.dot(n, wv_ref[...], preferred_element_type=F32)
    k_ref[...] = k
    v_ref[...] = v
    kb_ref[...] = k.astype(BF16)
    vb_ref[...] = v.astype(BF16)


def _memkv(mem2, g, wk, wv, tm):
    n, d = mem2.shape
    row = lambda i: (i, 0)
    const = lambda i: (0, 0)
    blk = pl.BlockSpec((tm, d), row)
    wspec = pl.BlockSpec((d, d), const)
    return pl.pallas_call(
        _memkv_kernel, grid=(n // tm,),
        in_specs=[blk, pl.BlockSpec((1, d), const), wspec, wspec],
        out_specs=[blk, blk, blk, blk],
        out_shape=[jax.ShapeDtypeStruct((n, d), F32)] * 2 + [jax.ShapeDtypeStruct((n, d), BF16)] * 2,
        compiler_params=_params("arbitrary"),
    )(mem2, g.reshape(1, d), wk, wv)


ROW_TILE = D_MODEL // LANES


def _load_tile_rows(ref):
    rows = ref.shape[0] // ROW_TILE
    return jnp.concatenate([ref[pl.ds(s, rows, stride=ROW_TILE), :] for s in range(ROW_TILE)], axis=1)


def _store_tile_rows(ref, val):
    rows = val.shape[0]
    for s in range(ROW_TILE):
        ref[pl.ds(s, rows, stride=ROW_TILE), :] = val[:, s * LANES:(s + 1) * LANES]


def _memattn_kernel(q_ref, k_ref, v_ref, h_ref, wo_ref, g_ref, wr_ref, br_ref,
                    h2_ref, xn_ref, logit_ref):
    q = q_ref[...]
    outs = []
    for h in range(MEM_HEADS):
        sl = slice(h * MEM_HEAD_DIM, (h + 1) * MEM_HEAD_DIM)
        s = lax.dot_general(q[:, sl], k_ref[:, sl], (((1,), (1,)), ((), ())), preferred_element_type=F32)
        p = jnp.exp2(s - jnp.max(s, axis=1, keepdims=True))
        o = jnp.dot(p.astype(BF16), v_ref[:, sl], preferred_element_type=F32)
        outs.append((o / jnp.sum(p, axis=1, keepdims=True)).astype(BF16))
    h2 = h_ref[...] + jnp.dot(jnp.concatenate(outs, axis=1), wo_ref[...], preferred_element_type=F32)
    h2_ref[...] = h2
    xn_f32 = _rms(h2, g_ref[...])
    _store_tile_rows(xn_ref, xn_f32)
    logit_ref[...] = jnp.dot(xn_f32.astype(BF16), wr_ref[...], preferred_element_type=F32) + br_ref[...]


def _memattn(qm, mk, mv, h1, w_mo, g_ffn, w_router, b_router, *, b0, b, t, tq):
    d = h1.shape[1]
    n = b * t
    n_q = t // tq
    row = lambda bi, i: (bi * n_q + i, 0)
    src_row = lambda bi, i: ((b0 + bi) * n_q + i, 0)
    const = lambda bi, i: (0, 0)
    mem = pl.BlockSpec((None, MEM_LEN, d), lambda bi, i: (b0 + bi, 0, 0))
    src = pl.BlockSpec((tq, d), src_row)
    blk = pl.BlockSpec((tq, d), row)
    tiled = pl.BlockSpec((tq * ROW_TILE, LANES), row)
    return pl.pallas_call(
        _memattn_kernel,
        grid=(b, n_q),
        in_specs=[src, mem, mem, src, pl.BlockSpec((d, d), const), pl.BlockSpec((1, d), const),
                  pl.BlockSpec((d, LANES), const), pl.BlockSpec((1, LANES), const)],
        out_specs=[blk, tiled, pl.BlockSpec((tq, LANES), row)],
        out_shape=[jax.ShapeDtypeStruct((n, d), F32), jax.ShapeDtypeStruct((n * ROW_TILE, LANES), F32),
                   jax.ShapeDtypeStruct((n, LANES), F32)],
        compiler_params=_params("arbitrary", "arbitrary"),
    )(qm, mk, mv, h1, w_mo, g_ffn.reshape(1, d), w_router, b_router)


ROUTER_ROWS = 48


def _first_slot(hit, slot_f32):
    return jnp.min(jnp.where(hit, slot_f32, float(LANES)), axis=0, keepdims=True).astype(I32)


def _router_kernel(lt_ref, ri_ref, rg_ref, cnt_ref, carry_ref, *, rt):
    @pl.when(pl.program_id(0) == 0)
    def _():
        carry_ref[...] = jnp.zeros(carry_ref.shape, F32)

    lt = lt_ref[0:ROUTER_ROWS, :]
    slot = lax.broadcasted_iota(I32, (ROUTER_ROWS, rt), 0)
    slot_f = slot.astype(F32)
    g_log = jnp.where((slot >= GROUP_LANE0) & (slot < GROUP_LANE0 + N_GROUPS), lt, NEG)
    g_max = jnp.max(g_log, axis=0, keepdims=True)
    g_idx = _first_slot(g_log == g_max, slot_f) - GROUP_LANE0
    g_p = 1.0 / jnp.sum(jnp.exp(g_log - g_max), axis=0, keepdims=True)
    in_group = (slot >= g_idx * EXPERTS_PER_GROUP) & (slot < (g_idx + 1) * EXPERTS_PER_GROUP)
    e_log = jnp.where(in_group, lt, NEG)
    e_exp = jnp.exp(e_log - jnp.max(e_log, axis=0, keepdims=True))
    prob = jnp.where(in_group, e_exp / jnp.sum(e_exp, axis=0, keepdims=True), -1.0)
    p1 = jnp.max(prob, axis=0, keepdims=True)
    i1 = _first_slot(prob == p1, slot_f)
    rest = jnp.where(slot == i1, -1.0, prob)
    p2 = jnp.max(rest, axis=0, keepdims=True)
    i2 = _first_slot(rest == p2, slot_f)
    top_sum = p1 + p2

    hot1, hot2 = slot == i1, slot == i2
    onehot = jnp.where(hot1 | hot2, 1.0, 0.0)
    earlier = lax.broadcasted_iota(I32, (rt, rt), 0) < lax.broadcasted_iota(I32, (rt, rt), 1)
    before = jnp.dot(onehot.astype(BF16), jnp.where(earlier, 1.0, 0.0).astype(BF16),
                     preferred_element_type=F32) + carry_ref[0:ROUTER_ROWS, 0:1]
    r1 = jnp.sum(jnp.where(hot1, before, 0.0), axis=0, keepdims=True)
    r2 = jnp.sum(jnp.where(hot2, before, 0.0), axis=0, keepdims=True)
    carry_ref[0:ROUTER_ROWS, :] = carry_ref[0:ROUTER_ROWS, :] + jnp.sum(onehot, axis=1, keepdims=True)
    cnt_ref[...] = carry_ref[...]

    ri_ref[...] = jnp.concatenate([i1, i2, r1.astype(I32), r2.astype(I32), jnp.zeros((4, rt), I32)], axis=0)
    rg_ref[...] = jnp.concatenate([g_p * p1 / top_sum, g_p * p2 / top_sum, jnp.zeros((6, rt), F32)], axis=0)


def _router(logits_t, rt):
    n = logits_t.shape[1]
    col = lambda i: (0, i)
    return pl.pallas_call(
        functools.partial(_router_kernel, rt=rt),
        grid=(n // rt,),
        in_specs=[pl.BlockSpec((LANES, rt), col)],
        out_specs=[pl.BlockSpec((8, rt), col), pl.BlockSpec((8, rt), col),
                   pl.BlockSpec((LANES, LANES), lambda i: (0, 0))],
        out_shape=[jax.ShapeDtypeStruct((8, n), I32), jax.ShapeDtypeStruct((8, n), F32),
                   jax.ShapeDtypeStruct((LANES, LANES), F32)],
        scratch_shapes=[pltpu.VMEM((LANES, LANES), F32)],
        compiler_params=_params("arbitrary"),
    )(logits_t)


SC_CORES, SC_SUBCORES = 2, 16
SC_WINDOW = 128


def _sc_mesh():
    return plsc.VectorSubcoreMesh(core_axis_name="core", subcore_axis_name="subcore")


def _sc_split(n_windows):
    if n_windows % (SC_CORES * SC_SUBCORES) == 0:
        return ("core", "subcore")
    assert n_windows % SC_SUBCORES == 0
    return "subcore"


def _sc_scatter_rows(src, idx, n_out):
    rows = src.shape[0]
    win = SC_WINDOW
    n_src_blocks = rows // win

    @functools.partial(pl.kernel, out_type=jax.ShapeDtypeStruct((n_out, LANES), src.dtype),
                       mesh=_sc_mesh(), scratch_types=[])
    def scatter(x_hbm, i_hbm, o_hbm):
        def body(x_vmem, i_vmem):
            pltpu.sync_copy(x_vmem, o_hbm.at[i_vmem.at[0]])

        pltpu.emit_pipeline(
            body, grid=(n_out // win,),
            in_specs=[pl.BlockSpec((win, LANES), lambda i: (i % n_src_blocks, 0)),
                      pl.BlockSpec((1, win), lambda i: (0, i))],
            out_specs=[],
            core_axis_name=_sc_split(n_out // win), dimension_semantics=(pltpu.PARALLEL,),
        )(x_hbm, i_hbm)

    return scatter(src, idx.reshape(1, n_out))


def _sc_gather_rows(src, idx):
    n_out = idx.shape[0]
    win = SC_WINDOW

    @functools.partial(pl.kernel, out_type=jax.ShapeDtypeStruct((n_out, LANES), src.dtype),
                       mesh=_sc_mesh(), scratch_types=[])
    def gather(x_hbm, i_hbm, o_hbm):
        def body(i_vmem, o_vmem):
            pltpu.sync_copy(x_hbm.at[i_vmem.at[0]], o_vmem)

        pltpu.emit_pipeline(
            body, grid=(n_out // win,),
            in_specs=[pl.BlockSpec((1, win), lambda i: (0, i))],
            out_specs=[pl.BlockSpec((win, LANES), lambda i: (i, 0))],
            core_axis_name=_sc_split(n_out // win), dimension_semantics=(pltpu.PARALLEL,),
        )(i_hbm, o_hbm)

    return gather(src, idx.reshape(1, n_out))


def _expert_kernel(blk_ref, e_ref, lo_ref, hi_ref, x_ref, w1_ref, w3_ref, w2_ref, y_ref):
    w = pl.program_id(0)
    lo, hi = lo_ref[w], hi_ref[w]
    row0 = blk_ref[w] * MOE_ROWS

    def compute():
        x = _load_tile_rows(x_ref).astype(BF16)
        a = jnp.dot(x, w1_ref[...], preferred_element_type=F32)
        g = jnp.dot(x, w3_ref[...], preferred_element_type=F32)
        mid = ((a / (1.0 + jnp.exp(-a))) * g).astype(BF16)
        y = jnp.dot(mid, w2_ref[...], preferred_element_type=F32)
        rows = row0 + lax.broadcasted_iota(I32, y.shape, 0)
        return y, (rows >= lo) & (rows < hi)

    @pl.when((hi > lo) & (lo == row0))
    def _():
        y, mine = compute()
        _store_tile_rows(y_ref, jnp.where(mine, y, 0.0))

    @pl.when((hi > lo) & (lo != row0))
    def _():
        y, mine = compute()
        _store_tile_rows(y_ref, jnp.where(mine, y, _load_tile_rows(y_ref)))


def _experts(items, xs, w1, w3, w2):
    d = D_MODEL
    n_items = items[0].shape[0]
    xmap = lambda w, blk, e, lo, hi: (blk[w], 0)
    wmap = lambda w, blk, e, lo, hi: (e[w], 0, 0)
    return pl.pallas_call(
        _expert_kernel,
        grid_spec=pltpu.PrefetchScalarGridSpec(
            num_scalar_prefetch=4, grid=(n_items,),
            in_specs=[pl.BlockSpec((MOE_ROWS * ROW_TILE, LANES), xmap),
                      pl.BlockSpec((None, d, D_EXPERT), wmap), pl.BlockSpec((None, d, D_EXPERT), wmap),
                      pl.BlockSpec((None, D_EXPERT, d), wmap)],
            out_specs=pl.BlockSpec((MOE_ROWS * ROW_TILE, LANES), xmap)),
        out_shape=jax.ShapeDtypeStruct(xs.shape, F32),
        compiler_params=_params("arbitrary"),
    )(*items, xs, w1, w3, w2)


def _work_items(counts, n_rows):
    n_blocks = n_rows // MOE_ROWS
    n_items = n_blocks + N_EXPERTS - 1
    ends = jnp.cumsum(counts)
    starts = ends - counts
    first_blk = starts // MOE_ROWS
    n_blk = jnp.where(counts > 0, (ends - 1) // MOE_ROWS - first_blk + 1, 0)
    item_end = jnp.cumsum(n_blk)
    total = item_end[-1]
    w = jnp.arange(n_items, dtype=I32)
    wc = jnp.minimum(w, total - 1)
    e = jnp.sum(item_end[None, :] <= wc[:, None], axis=1).astype(I32)
    blk = (first_blk[e] + wc - (item_end[e] - n_blk[e])).astype(I32)
    lo = jnp.maximum(starts[e], blk * MOE_ROWS).astype(I32)
    hi = jnp.minimum(ends[e], (blk + 1) * MOE_ROWS).astype(I32)
    live = w < total
    return blk, e, jnp.where(live, lo, 0), jnp.where(live, hi, 0), starts


def _combine_kernel(h_ref, z0_ref, z1_ref, rg_ref, g_ref, *rest):
    y_ref = rest[-1]
    rg = rg_ref[...]
    moe = _load_tile_rows(z0_ref) * rg[:, 0:1] + _load_tile_rows(z1_ref) * rg[:, 1:2]
    y_ref[...] = _rms(h_ref[...] + moe, g_ref[...])


def _combine(h2, z, rg, g_final, tm, y_prev, row0, n_total):
    n, d = h2.shape
    n_t = n // tm
    row = lambda i: (i, 0)
    in_specs = [pl.BlockSpec((tm, d), row), pl.BlockSpec((tm * ROW_TILE, LANES), row),
                pl.BlockSpec((tm * ROW_TILE, LANES), lambda i: (n_t + i, 0)), pl.BlockSpec((tm, 8), row),
                pl.BlockSpec((1, d), lambda i: (0, 0))]
    args = [h2, z, z, rg, g_final.reshape(1, d)]
    aliases = {}
    if y_prev is not None:
        in_specs.append(pl.BlockSpec(memory_space=pl.ANY))
        args.append(y_prev)
        aliases = {len(args) - 1: 0}
    return pl.pallas_call(
        _combine_kernel, grid=(n_t,),
        in_specs=in_specs,
        out_specs=pl.BlockSpec((tm, d), lambda i: (row0 // tm + i, 0)),
        out_shape=jax.ShapeDtypeStruct((n_total, d), F32),
        input_output_aliases=aliases,
        compiler_params=_params("arbitrary"),
    )(*args)


def _moe_and_final(h2, xn, ri, rg, counts, w1, w3, w2, g_final, *, tm, y_prev, row0, n_total):
    n = h2.shape[0]
    blk, e, lo, hi, starts = _work_items(counts, 2 * n)
    pos = (starts[ri[0:2]] + ri[2:4]).astype(I32)
    sub = (pos[:, :, None] * ROW_TILE + jnp.arange(ROW_TILE, dtype=I32)).reshape(2 * n * ROW_TILE)
    xs = _sc_scatter_rows(xn, sub, 2 * n * ROW_TILE)
    ys = _experts((blk, e, lo, hi), xs, w1, w3, w2)
    z = _sc_gather_rows(ys, sub)
    return _combine(h2, z, rg.T, g_final, tm, y_prev, row0, n_total)


def _pad_time(a, s_pad):
    return jnp.pad(a, ((0, 0), (0, s_pad - a.shape[1])) + ((0, 0),) * (a.ndim - 2))


def _layer(x, past, mem_k, mem_v, wts, lam_init, g_final, *, tm, tq, tk, moe_parts):
    b, t, d = x.shape
    n = b * t
    p_len = 0 if past is None else past[0].shape[1]
    s_valid = p_len + t
    s_pad = -(-s_valid // tk) * tk
    pos = p_len + jnp.arange(t)

    fk, fv, logf, dk, dv, qf, kf, vf, qd, kd, vd = _inproj(
        x, wts['g_mix'], wts['w_main'], wts['w_ff'], wts['b_forget'], pos, tm)
    time_major = tm <= t

    def leaf(a, dims):
        if time_major:
            return jnp.moveaxis(a.reshape((b,) + dims + (t,)), -1, 1)
        return a.reshape((b, t) + dims)

    new = (leaf(fk, (FOX_HEADS, HEAD_DIM)), leaf(fv, (FOX_HEADS, HEAD_DIM)), leaf(logf, (FOX_HEADS,)),
           leaf(dk, (DIFF_HEADS, 2, HEAD_DIM)), dv.reshape(b, t, DIFF_HEADS, 2 * HEAD_DIM))

    def keys_t(cur, old_rows):
        if not time_major:
            cur = jnp.swapaxes(cur.reshape(b, t, FOX_W), 1, 2)
        if old_rows is not None:
            prev = jnp.swapaxes(old_rows.reshape(b, p_len, FOX_W).astype(BF16), 1, 2)
            cur = jnp.concatenate([prev, cur], axis=2)
        return jnp.pad(cur, ((0, 0), (0, 0), (0, s_pad - s_valid)))

    def with_past(new_b16, old_rows):
        cur = new_b16.reshape(b, t, FOX_W)
        if old_rows is not None:
            cur = jnp.concatenate([old_rows.reshape(b, p_len, FOX_W).astype(BF16), cur], axis=1)
        return _pad_time(cur, s_pad)

    logf_t = logf if time_major else jnp.swapaxes(logf.reshape(b, t, FOX_HEADS), 1, 2)
    if past is not None:
        logf_t = jnp.concatenate([jnp.swapaxes(past[2].astype(F32), 1, 2), logf_t], axis=2)
    k_bias, q_bias_t = _fox_bias(jnp.pad(logf_t, ((0, 0), (0, 0), (0, s_pad - s_valid))))
    q_bias = jnp.swapaxes(q_bias_t[:, :, p_len:p_len + t], 1, 2).reshape(n, LANES)

    old = (None,) * 5 if past is None else past
    geom = dict(b=b, t=t, s_pad=s_pad, s_valid=s_valid, past=p_len, tq=tq, tk=tk, lam_init=lam_init)
    fox_o = _self_attention(qf, keys_t(kf, old[0]), with_past(vf, old[1]), (q_bias, k_bias),
                            fox=True, **geom)
    diff_o = _self_attention(qd, keys_t(kd, old[3]), with_past(vd, old[4]),
                             (wts['lam_vecs'], wts['g_diff']), fox=False, **geom)
    h1, qm = _outproj(x.reshape(n, d), fox_o, diff_o, wts['wo_f'], wts['wo_d'], wts['g_mem_q'],
                      wts['w_mq'], tm)
    assert b % moe_parts == 0
    bp = b // moe_parts
    y = None
    for part in range(moe_parts):
        h2, xn, logits = _memattn(qm, mem_k, mem_v, h1, wts['w_mo'], wts['g_ffn'], wts['w_router'],
                                  wts['b_router'], b0=part * bp, b=bp, t=t, tq=tq)
        ri, rg, cnt = _router(logits.T, min(tm, bp * t))
        counts = cnt[:N_EXPERTS, 0].astype(I32)
        y = _moe_and_final(h2, xn, ri, rg, counts, wts['w1'], wts['w3'], wts['w2'], g_final, tm=tm,
                           y_prev=y, row0=part * bp * t, n_total=n)
    return y.reshape(b, t, d), new


def kernel(x_prompt, x_sample, cache_fox_k, cache_fox_v, cache_fox_logf, cache_diff_k, cache_diff_v, cache_mem_k, cache_mem_v, mem_prompt, g_mix, w_in, b_forget, lam_q1, lam_k1, lam_q2, lam_k2, g_diff, w_o, g_mem_q, g_mem_kv, w_mq, w_mk, w_mv, w_mo, g_ffn, w_group, b_group, w_erouter, b_erouter, w1, w3, w2, g_final):
    depth = w_in.shape[0]
    assert depth == 1
    l = 0
    lam_init = 0.8 - 0.6 * math.exp(-0.3 * l)
    d = D_MODEL
    w = w_in[l]
    ff0 = 3 * FOX_W
    w_router = jnp.zeros((d, LANES), F32).at[:, :N_EXPERTS].set(w_erouter[l])
    w_router = w_router.at[:, GROUP_LANE0:GROUP_LANE0 + N_GROUPS].set(w_group[l])
    b_router = jnp.zeros((1, LANES), F32).at[0, :N_EXPERTS].set(b_erouter[l])
    b_router = b_router.at[0, GROUP_LANE0:GROUP_LANE0 + N_GROUPS].set(b_group[l])
    wts = {
        'g_mix': g_mix[l],
        'w_main': jnp.concatenate([w[:, :ff0], w[:, ff0 + FOX_HEADS:]], axis=1).astype(BF16),
        'w_ff': jnp.pad(w[:, ff0:ff0 + FOX_HEADS], ((0, 0), (0, LANES - FOX_HEADS))).astype(BF16),
        'b_forget': b_forget[l],
        'lam_vecs': jnp.stack([lam_q1[l], lam_k1[l], lam_q2[l], lam_k2[l]]),
        'g_diff': g_diff[l],
        'wo_f': w_o[l][:FOX_W].astype(BF16), 'wo_d': w_o[l][FOX_W:].astype(BF16),
        'g_mem_q': g_mem_q[l], 'w_mq': w_mq[l].astype(BF16), 'w_mo': w_mo[l].astype(BF16),
        'g_ffn': g_ffn[l], 'w_router': w_router.astype(BF16), 'b_router': b_router,
        'w1': w1[l].astype(BF16), 'w3': w3[l].astype(BF16), 'w2': w2[l].astype(BF16),
    }

    bp, tp, _ = x_prompt.shape
    mk, mv, mk_b, mv_b = _memkv(mem_prompt.reshape(bp * MEM_LEN, d), g_mem_kv[l],
                                w_mk[l].astype(BF16), w_mv[l].astype(BF16), 512)
    yp, new_p = _layer(x_prompt, None, mk_b.reshape(bp, MEM_LEN, d), mv_b.reshape(bp, MEM_LEN, d),
                       wts, lam_init, g_final, tm=512, tq=512, tk=512, moe_parts=2)

    bs, ts, _ = x_sample.shape
    past = (cache_fox_k[l], cache_fox_v[l], cache_fox_logf[l], cache_diff_k[l], cache_diff_v[l])
    ys, new_s = _layer(x_sample, past, cache_mem_k[l].reshape(bs, MEM_LEN, d).astype(BF16),
                       cache_mem_v[l].reshape(bs, MEM_LEN, d).astype(BF16),
                       wts, lam_init, g_final, tm=bs * ts, tq=ts, tk=512, moe_parts=1)

    mem_shape = (1, bp, MEM_LEN, MEM_HEADS, MEM_HEAD_DIM)
    return (yp, ys) + tuple(a[None] for a in new_p) + (mk.reshape(mem_shape), mv.reshape(mem_shape)) \
        + tuple(a[None] for a in new_s)
```

```python
import functools
import math

import jax
import jax.numpy as jnp
from jax import lax
from jax.experimental import pallas as pl
from jax.experimental.pallas import tpu as pltpu
from jax.experimental.pallas import tpu_sc as plsc

F32 = jnp.float32
BF16 = jnp.bfloat16
I32 = jnp.int32

D_MODEL = 1024
HEAD_DIM = 64
FOX_HEADS = 8
DIFF_HEADS = 4
FOX_W = FOX_HEADS * HEAD_DIM
DIFF_W = DIFF_HEADS * 2 * HEAD_DIM
ROT_DIM = HEAD_DIM // 4
ROPE_THETA = 500000.0
CHUNK = 64
MEM_LEN = 256
MEM_HEADS = 4
MEM_HEAD_DIM = D_MODEL // MEM_HEADS
N_GROUPS = 4
EXPERTS_PER_GROUP = 8
N_EXPERTS = N_GROUPS * EXPERTS_PER_GROUP
D_EXPERT = 512
EPS = 1e-6

LANES = 128
LOG2E = 1.4426950408889634
NEG = -1e30
VMEM_LIMIT_BYTES = 56 * 1024 * 1024
MOE_ROWS = 256
GROUP_LANE0 = N_EXPERTS


def _params(*sem):
    return pltpu.CompilerParams(dimension_semantics=sem, vmem_limit_bytes=VMEM_LIMIT_BYTES)


def _rms(x, g):
    return (x * lax.rsqrt(jnp.mean(x * x, axis=-1, keepdims=True) + EPS)) * g


def _rope(x, cos, sin_lo, sin_hi):
    outs = []
    for c in range(x.shape[1] // LANES):
        xc = x[:, c * LANES:(c + 1) * LANES]
        up = pltpu.roll(xc, LANES - ROT_DIM // 2, axis=1)
        dn = pltpu.roll(xc, ROT_DIM // 2, axis=1)
        outs.append(xc * cos + up * sin_lo + dn * sin_hi)
    return jnp.concatenate(outs, axis=1)


def _inproj_kernel(x_ref, g_ref, w_ref, wf_ref, bf_ref, cos_ref, slo_ref, shi_ref,
                   fk_ref, fv_ref, logf_ref, dk_ref, dv_ref,
                   qf_ref, kf_ref, vf_ref, qd_ref, kd_ref, vd_ref, *, q_scale, time_major):
    n = _rms(x_ref[...], g_ref[...]).astype(BF16)
    turn = (lambda a: a.T) if time_major else (lambda a: a)

    def proj(c):
        return jnp.dot(n, w_ref[:, c * FOX_W:(c + 1) * FOX_W], preferred_element_type=F32)

    cos, slo, shi = cos_ref[...], slo_ref[...], shi_ref[...]
    qf_ref[...] = (proj(0) * q_scale).astype(BF16)
    fk = turn(proj(1))
    fk_ref[...] = fk
    kf_ref[...] = fk.astype(BF16)
    fv = proj(2)
    fv_ref[...] = turn(fv)
    vf_ref[...] = fv.astype(BF16)
    qd_ref[...] = (_rope(proj(3), cos, slo, shi) * q_scale).astype(BF16)
    dk = turn(_rope(proj(4), cos, slo, shi))
    dk_ref[...] = dk
    kd_ref[...] = dk.astype(BF16)
    dv = proj(5)
    for h in range(DIFF_HEADS):
        dv_ref[pl.ds(h, dv.shape[0], stride=DIFF_HEADS), :] = dv[:, h * LANES:(h + 1) * LANES]
    vd_ref[...] = dv.astype(BF16)
    z = jnp.dot(n, wf_ref[...], preferred_element_type=F32) + bf_ref[...]
    logf = jnp.minimum(z, 0.0) - jnp.log1p(jnp.exp(-jnp.abs(z)))
    logf_ref[...] = logf.T[:FOX_HEADS] if time_major else logf[:, :FOX_HEADS]


def _rope_tables(pos):
    half = ROT_DIM // 2
    inv = jnp.power(ROPE_THETA, -jnp.arange(half, dtype=F32) / half)
    ang = pos.astype(F32)[:, None] * inv[None, :]
    cos, sin = jnp.cos(ang), jnp.sin(ang)
    t = pos.shape[0]
    pad = jnp.zeros((t, HEAD_DIM - ROT_DIM), F32)
    zero = jnp.zeros((t, half), F32)
    cos64 = jnp.concatenate([cos, cos, pad + 1.0], axis=1)
    slo64 = jnp.concatenate([-sin, zero, pad], axis=1)
    shi64 = jnp.concatenate([zero, sin, pad], axis=1)
    return tuple(jnp.tile(a, (1, LANES // HEAD_DIM)) for a in (cos64, slo64, shi64))


def _inproj(x, g, w_main, w_ff, b_ff, pos, tm):
    b, t, d = x.shape
    n = b * t
    tables = _rope_tables(pos)
    time_major = tm <= t
    if time_major:
        assert t % tm == 0
        per = t // tm
        tab_map = lambda i: (i % per, 0)
        turned = pl.BlockSpec((None, FOX_W, tm), lambda i: (i // per, 0, i % per))
        gate = pl.BlockSpec((None, FOX_HEADS, tm), lambda i: (i // per, 0, i % per))
        turned_shape, gate_shape = (b, FOX_W, t), (b, FOX_HEADS, t)
    else:
        assert tm % t == 0
        tables = tuple(jnp.tile(a, (tm // t, 1)) for a in tables)
        tab_map = lambda i: (0, 0)
    assert n % tm == 0
    row = lambda i: (i, 0)
    const = lambda i: (0, 0)
    wide = pl.BlockSpec((tm, FOX_W), row)
    if not time_major:
        turned, gate = wide, pl.BlockSpec((tm, FOX_HEADS), row)
        turned_shape, gate_shape = (n, FOX_W), (n, FOX_HEADS)
    tab = pl.BlockSpec((tm, LANES), tab_map)
    sds = jax.ShapeDtypeStruct
    return pl.pallas_call(
        functools.partial(_inproj_kernel, q_scale=HEAD_DIM ** -0.5 * LOG2E, time_major=time_major),
        grid=(n // tm,),
        in_specs=[pl.BlockSpec((tm, d), row), pl.BlockSpec((1, d), const),
                  pl.BlockSpec(w_main.shape, const), pl.BlockSpec(w_ff.shape, const),
                  pl.BlockSpec((1, LANES), const), tab, tab, tab],
        out_specs=[turned, turned, gate, turned, pl.BlockSpec((tm * DIFF_HEADS, LANES), row),
                   wide, turned, wide, wide, turned, wide],
        out_shape=[sds(turned_shape, F32), sds(turned_shape, F32), sds(gate_shape, F32),
                   sds(turned_shape, F32), sds((n * DIFF_HEADS, LANES), F32),
                   sds((n, FOX_W), BF16), sds(turned_shape, BF16), sds((n, FOX_W), BF16),
                   sds((n, FOX_W), BF16), sds(turned_shape, BF16), sds((n, FOX_W), BF16)],
        compiler_params=_params("arbitrary"),
    )(x.reshape(n, d), g.reshape(1, d), w_main, w_ff,
      jnp.pad(b_ff.reshape(1, FOX_HEADS), ((0, 0), (0, LANES - FOX_HEADS))), *tables)


AUX_LANES = 8


def _bf16_part(x):
    return pltpu.bitcast(pltpu.bitcast(x, jnp.uint32) & jnp.uint32(0xFFFF0000), F32)


def _cumsum_kernel(x_ref, kb_ref, qb_ref):
    x = x_ref[...]
    s_len = x.shape[1]
    lane = lax.broadcasted_iota(I32, x.shape, 1)
    shift = 1
    while shift < s_len:
        x = x + jnp.where(lane >= shift, pltpu.roll(x, shift, axis=1), 0.0)
        shift *= 2
    c = x * LOG2E
    hi = _bf16_part(c)
    mid = _bf16_part(c - hi)
    lo = _bf16_part(c - hi - mid)
    slot = lax.broadcasted_iota(I32, (LANES, x.shape[0]), 0)
    head = lax.broadcasted_iota(I32, (LANES, x.shape[0]), 1)

    def place(first_slot):
        out = jnp.zeros((LANES, s_len), F32)
        for i, part in enumerate((hi, mid, lo)):
            sel = jnp.where(slot == head * AUX_LANES + first_slot + i, 1.0, 0.0).astype(BF16)
            out = out + jnp.dot(sel, part.astype(BF16), preferred_element_type=F32)
        return out

    slot_col = lax.broadcasted_iota(I32, (LANES, s_len), 0)
    used = slot_col < FOX_HEADS * AUX_LANES
    within = slot_col % AUX_LANES
    kb_ref[...] = (jnp.where(used & (within < 3), 1.0, 0.0) - place(3)).astype(BF16)
    qb_ref[...] = (jnp.where(used & (within >= 3) & (within < 6), 1.0, 0.0) + place(0)).astype(BF16)


def _fox_bias(logf_t):
    b, h, s = logf_t.shape
    x = jnp.pad(logf_t, ((0, 0), (0, 16 - h), (0, 0)))
    out = pl.BlockSpec((None, LANES, s), lambda i: (i, 0, 0))
    return pl.pallas_call(
        _cumsum_kernel, grid=(b,), in_specs=[pl.BlockSpec((None, 16, s), lambda i: (i, 0, 0))],
        out_specs=[out, out],
        out_shape=[jax.ShapeDtypeStruct((b, LANES, s), BF16)] * 2,
        compiler_params=_params("arbitrary"),
    )(x)


def _attn_kernel(*refs, tq, tk, past, s_valid, fox, lam_init):
    if fox:
        q_ref, qa_ref, k_ref, ka_ref, v_ref, o_ref, sa_ref, sb_ref, mx_ref, m_ref, l_ref, acc_ref = refs
    else:
        q_ref, k_ref, v_ref, lam_ref, g_ref, o_ref, sa_ref, sb_ref, mx_ref, m_ref, l_ref, acc_ref = refs
    i = pl.program_id(2)
    q = q_ref[...]
    lane = lax.broadcasted_iota(I32, (tq, LANES), 1)
    low = lane < HEAD_DIM
    zero = jnp.zeros_like(q)
    q_part = [jnp.where(low, q, zero), jnp.where(low, zero, q)]
    if fox:
        qa = qa_ref[...]
        for h in range(2):
            lane0 = (2 * pl.program_id(1) + h) * AUX_LANES
            aux = jnp.where((lane >= lane0) & (lane < lane0 + AUX_LANES), qa, zero)
            q_part[h] = jnp.concatenate([q_part[h], aux], axis=1)
    m_ref[...] = jnp.full(m_ref.shape, NEG, F32)
    l_ref[...] = jnp.zeros(l_ref.shape, F32)
    acc_ref[...] = jnp.zeros(acc_ref.shape, F32)
    q0 = past + i * tq
    nb = q0 // tk
    bufs = (sa_ref, sb_ref)

    def scores(j, buf):
        ks = pl.multiple_of(j * tk, tk)
        k = k_ref[:, pl.ds(ks, tk)]
        if fox:
            k = jnp.concatenate([k, ka_ref[:, pl.ds(ks, tk)]], axis=0)
        for h in range(2):
            s = jnp.dot(q_part[h], k, preferred_element_type=F32)
            bufs[buf][h] = s
            mx_ref[buf, h] = jnp.broadcast_to(jnp.max(s, axis=1, keepdims=True), (tq, LANES))

    def consume(j, buf, use_mask):
        ks = pl.multiple_of(j * tk, tk)
        v = jnp.concatenate([v_ref[pl.ds(ks, tk), :], jnp.ones((tk, LANES), BF16)], axis=1)
        pv, alpha = [], []
        for h in range(2):
            t = bufs[buf][h]
            if use_mask:
                qpos = q0 + lax.broadcasted_iota(I32, (tq, tk), 0)
                kpos = j * tk + lax.broadcasted_iota(I32, (tq, tk), 1)
                if fox:
                    seen = kpos <= qpos
                else:
                    seen = kpos < jnp.minimum((qpos // CHUNK + 1) * CHUNK, s_valid)
                t = jnp.where(seen, t, NEG)
                mx = jnp.max(t, axis=1, keepdims=True)
            else:
                mx = mx_ref[buf, h]
            m_prev = m_ref[h]
            m_new = jnp.maximum(m_prev, mx)
            a = jnp.exp2(m_prev - m_new)
            p = jnp.exp2(t - jnp.concatenate([m_new] * (tk // LANES), axis=1))
            m_ref[h] = m_new
            both = jnp.dot(p.astype(BF16), v, preferred_element_type=F32)
            pv.append(both[:, :LANES])
            l_ref[h] = a * l_ref[h] + both[:, LANES:]
            alpha.append(a)
        if fox:
            acc_ref[0] = acc_ref[0] * jnp.where(low, alpha[0], alpha[1]) + jnp.where(low, pv[0], pv[1])
        else:
            for h in range(2):
                acc_ref[h] = acc_ref[h] * alpha[h] + pv[h]

    scores(0, 0)

    def pair(jj, c):
        j = 2 * jj
        scores(j + 1, 1)
        consume(j, 0, False)
        scores(j + 2, 0)
        consume(j + 1, 1, False)
        return c

    lax.fori_loop(0, nb // 2, pair, 0)

    @pl.when(nb % 2 == 1)
    def _():
        scores(nb, 1)
        consume(nb - 1, 0, False)
        consume(nb, 1, True)

    @pl.when(nb % 2 == 0)
    def _():
        consume(nb, 0, True)

    if fox:
        o = acc_ref[0] / jnp.where(low, l_ref[0], l_ref[1])
    else:
        lam_v = lam_ref[...]
        lam = (jnp.exp(jnp.sum(lam_v[0:1] * lam_v[1:2], axis=1, keepdims=True))
               - jnp.exp(jnp.sum(lam_v[2:3] * lam_v[3:4], axis=1, keepdims=True)) + lam_init)
        o = acc_ref[0] / l_ref[0] - lam * (acc_ref[1] / l_ref[1])
        o = _rms(o, g_ref[...]) * (1.0 - lam_init)
    o_ref[...] = o.astype(o_ref.dtype)


def _self_attention(q, k, v, extra, *, b, t, s_pad, s_valid, past, tq, tk, fox, lam_init):
    assert tk % tq == 0 and past % tq == 0 and t % tq == 0 and tk % CHUNK == 0 and s_pad % tk == 0
    assert tq % CHUNK == 0 or (CHUNK % tq == 0 and past % CHUNK == 0)
    n_q = t // tq
    k_spec = pl.BlockSpec((None, LANES, s_pad), lambda bi, p, i: (bi, p, 0))
    v_spec = pl.BlockSpec((None, s_pad, LANES), lambda bi, p, i: (bi, 0, p))
    q_spec = pl.BlockSpec((tq, LANES), lambda bi, p, i: (bi * n_q + i, p))
    const = lambda bi, p, i: (0, 0)
    if fox:
        in_specs = [q_spec, pl.BlockSpec((tq, LANES), lambda bi, p, i: (bi * n_q + i, 0)), k_spec,
                    pl.BlockSpec((None, LANES, s_pad), lambda bi, p, i: (bi, 0, 0)), v_spec]
        args = (q, extra[0], k, extra[1], v)
    else:
        in_specs = [q_spec, k_spec, v_spec, pl.BlockSpec((4, HEAD_DIM), const), pl.BlockSpec((1, LANES), const)]
        args = (q, k, v, extra[0], extra[1].reshape(1, LANES))
    stat = pltpu.VMEM((2, tq, LANES), F32)
    return pl.pallas_call(
        functools.partial(_attn_kernel, tq=tq, tk=tk, past=past, s_valid=s_valid, fox=fox, lam_init=lam_init),
        grid=(b, FOX_W // LANES, n_q),
        in_specs=in_specs,
        out_specs=q_spec,
        out_shape=jax.ShapeDtypeStruct((b * t, FOX_W), BF16),
        scratch_shapes=[pltpu.VMEM((2, tq, tk), F32), pltpu.VMEM((2, tq, tk), F32),
                        pltpu.VMEM((2, 2, tq, LANES), F32), stat, stat, stat],
        compiler_params=_params("arbitrary", "arbitrary", "arbitrary"),
    )(*args)


def _outproj_kernel(x_ref, fo_ref, do_ref, wof_ref, wod_ref, g_ref, wq_ref, h_ref, q_ref, *, q_scale):
    h = (x_ref[...] + jnp.dot(fo_ref[...], wof_ref[...], preferred_element_type=F32)
         + jnp.dot(do_ref[...], wod_ref[...], preferred_element_type=F32))
    h_ref[...] = h
    n = _rms(h, g_ref[...]).astype(BF16)
    q_ref[...] = (jnp.dot(n, wq_ref[...], preferred_element_type=F32) * q_scale).astype(BF16)


def _outproj(x2, fox_o, diff_o, wo_f, wo_d, g_mem_q, w_mq, tm):
    n, d = x2.shape
    row = lambda i: (i, 0)
    const = lambda i: (0, 0)
    return pl.pallas_call(
        functools.partial(_outproj_kernel, q_scale=MEM_HEAD_DIM ** -0.5 * LOG2E),
        grid=(n // tm,),
        in_specs=[pl.BlockSpec((tm, d), row), pl.BlockSpec((tm, FOX_W), row),
                  pl.BlockSpec((tm, DIFF_W), row), pl.BlockSpec((FOX_W, d), const),
                  pl.BlockSpec((DIFF_W, d), const), pl.BlockSpec((1, d), const),
                  pl.BlockSpec((d, d), const)],
        out_specs=[pl.BlockSpec((tm, d), row), pl.BlockSpec((tm, d), row)],
        out_shape=[jax.ShapeDtypeStruct((n, d), F32), jax.ShapeDtypeStruct((n, d), BF16)],
        compiler_params=_params("arbitrary"),
    )(x2, fox_o, diff_o, wo_f, wo_d, g_mem_q.reshape(1, d), w_mq)


def _memkv_kernel(m_ref, g_ref, wk_ref, wv_ref, k_ref, v_ref, kb_ref, vb_ref):
    n = _rms(m_ref[...], g_ref[...]).astype(BF16)
    k = jnp.dot(n, wk_ref[...], preferred_element_type=F32)
    v = jnp.dot(n, wv_ref[...], preferred_element_type=F32)
    k_ref[...] = k
    v_ref[...] = v
    kb_ref[...] = k.astype(BF16)
    vb_ref[...] = v.astype(BF16)


def _memkv(mem2, g, wk, wv, tm):
    n, d = mem2.shape
    row = lambda i: (i, 0)
    const = lambda i: (0, 0)
    blk = pl.BlockSpec((tm, d), row)
    wspec = pl.BlockSpec((d, d), const)
    return pl.pallas_call(
        _memkv_kernel, grid=(n // tm,),
        in_specs=[blk, pl.BlockSpec((1, d), const), wspec, wspec],
        out_specs=[blk, blk, blk, blk],
        out_shape=[jax.ShapeDtypeStruct((n, d), F32)] * 2 + [jax.ShapeDtypeStruct((n, d), BF16)] * 2,
        compiler_params=_params("arbitrary"),
    )(mem2, g.reshape(1, d), wk, wv)


ROW_TILE = D_MODEL // LANES


def _load_tile_rows(ref):
    rows = ref.shape[0] // ROW_TILE
    return jnp.concatenate([ref[pl.ds(s, rows, stride=ROW_TILE), :] for s in range(ROW_TILE)], axis=1)


def _store_tile_rows(ref, val):
    rows = val.shape[0]
    for s in range(ROW_TILE):
        ref[pl.ds(s, rows, stride=ROW_TILE), :] = val[:, s * LANES:(s + 1) * LANES]


def _memattn_kernel(q_ref, k_ref, v_ref, h_ref, wo_ref, g_ref, wr_ref, br_ref,
                    h2_ref, xn_ref, logit_ref):
    q = q_ref[...]
    outs = []
    for h in range(MEM_HEADS):
        sl = slice(h * MEM_HEAD_DIM, (h + 1) * MEM_HEAD_DIM)
        s = lax.dot_general(q[:, sl], k_ref[:, sl], (((1,), (1,)), ((), ())), preferred_element_type=F32)
        p = jnp.exp2(s - jnp.max(s, axis=1, keepdims=True))
        o = jnp.dot(p.astype(BF16), v_ref[:, sl], preferred_element_type=F32)
        outs.append((o / jnp.sum(p, axis=1, keepdims=True)).astype(BF16))
    h2 = h_ref[...] + jnp.dot(jnp.concatenate(outs, axis=1), wo_ref[...], preferred_element_type=F32)
    h2_ref[...] = h2
    xn_f32 = _rms(h2, g_ref[...])
    _store_tile_rows(xn_ref, xn_f32)
    logit_ref[...] = jnp.dot(xn_f32.astype(BF16), wr_ref[...], preferred_element_type=F32) + br_ref[...]


def _memattn(qm, mk, mv, h1, w_mo, g_ffn, w_router, b_router, *, b0, b, t, tq):
    d = h1.shape[1]
    n = b * t
    n_q = t // tq
    row = lambda bi, i: (bi * n_q + i, 0)
    src_row = lambda bi, i: ((b0 + bi) * n_q + i, 0)
    const = lambda bi, i: (0, 0)
    mem = pl.BlockSpec((None, MEM_LEN, d), lambda bi, i: (b0 + bi, 0, 0))
    src = pl.BlockSpec((tq, d), src_row)
    blk = pl.BlockSpec((tq, d), row)
    tiled = pl.BlockSpec((tq * ROW_TILE, LANES), row)
    return pl.pallas_call(
        _memattn_kernel,
        grid=(b, n_q),
        in_specs=[src, mem, mem, src, pl.BlockSpec((d, d), const), pl.BlockSpec((1, d), const),
                  pl.BlockSpec((d, LANES), const), pl.BlockSpec((1, LANES), const)],
        out_specs=[blk, tiled, pl.BlockSpec((tq, LANES), row)],
        out_shape=[jax.ShapeDtypeStruct((n, d), F32), jax.ShapeDtypeStruct((n * ROW_TILE, LANES), F32),
                   jax.ShapeDtypeStruct((n, LANES), F32)],
        compiler_params=_params("arbitrary", "arbitrary"),
    )(qm, mk, mv, h1, w_mo, g_ffn.reshape(1, d), w_router, b_router)


ROUTER_ROWS = 48


def _first_slot(hit, slot_f32):
    return jnp.min(jnp.where(hit, slot_f32, float(LANES)), axis=0, keepdims=True).astype(I32)


def _router_kernel(lt_ref, ri_ref, rg_ref, cnt_ref, carry_ref, *, rt):
    @pl.when(pl.program_id(0) == 0)
    def _():
        carry_ref[...] = jnp.zeros(carry_ref.shape, F32)

    lt = lt_ref[0:ROUTER_ROWS, :]
    slot = lax.broadcasted_iota(I32, (ROUTER_ROWS, rt), 0)
    slot_f = slot.astype(F32)
    g_log = jnp.where((slot >= GROUP_LANE0) & (slot < GROUP_LANE0 + N_GROUPS), lt, NEG)
    g_max = jnp.max(g_log, axis=0, keepdims=True)
    g_idx = _first_slot(g_log == g_max, slot_f) - GROUP_LANE0
    g_p = 1.0 / jnp.sum(jnp.exp(g_log - g_max), axis=0, keepdims=True)
    in_group = (slot >= g_idx * EXPERTS_PER_GROUP) & (slot < (g_idx + 1) * EXPERTS_PER_GROUP)
    e_log = jnp.where(in_group, lt, NEG)
    e_exp = jnp.exp(e_log - jnp.max(e_log, axis=0, keepdims=True))
    prob = jnp.where(in_group, e_exp / jnp.sum(e_exp, axis=0, keepdims=True), -1.0)
    p1 = jnp.max(prob, axis=0, keepdims=True)
    i1 = _first_slot(prob == p1, slot_f)
    rest = jnp.where(slot == i1, -1.0, prob)
    p2 = jnp.max(rest, axis=0, keepdims=True)
    i2 = _first_slot(rest == p2, slot_f)
    top_sum = p1 + p2

    hot1, hot2 = slot == i1, slot == i2
    onehot = jnp.where(hot1 | hot2, 1.0, 0.0)
    earlier = lax.broadcasted_iota(I32, (rt, rt), 0) < lax.broadcasted_iota(I32, (rt, rt), 1)
    before = jnp.dot(onehot.astype(BF16), jnp.where(earlier, 1.0, 0.0).astype(BF16),
                     preferred_element_type=F32) + carry_ref[0:ROUTER_ROWS, 0:1]
    r1 = jnp.sum(jnp.where(hot1, before, 0.0), axis=0, keepdims=True)
    r2 = jnp.sum(jnp.where(hot2, before, 0.0), axis=0, keepdims=True)
    carry_ref[0:ROUTER_ROWS, :] = carry_ref[0:ROUTER_ROWS, :] + jnp.sum(onehot, axis=1, keepdims=True)
    cnt_ref[...] = carry_ref[...]

    ri_ref[...] = jnp.concatenate([i1, i2, r1.astype(I32), r2.astype(I32), jnp.zeros((4, rt), I32)], axis=0)
    rg_ref[...] = jnp.concatenate([g_p * p1 / top_sum, g_p * p2 / top_sum, jnp.zeros((6, rt), F32)], axis=0)


def _router(logits_t, rt):
    n = logits_t.shape[1]
    col = lambda i: (0, i)
    return pl.pallas_call(
        functools.partial(_router_kernel, rt=rt),
        grid=(n // rt,),
        in_specs=[pl.BlockSpec((LANES, rt), col)],
        out_specs=[pl.BlockSpec((8, rt), col), pl.BlockSpec((8, rt), col),
                   pl.BlockSpec((LANES, LANES), lambda i: (0, 0))],
        out_shape=[jax.ShapeDtypeStruct((8, n), I32), jax.ShapeDtypeStruct((8, n), F32),
                   jax.ShapeDtypeStruct((LANES, LANES), F32)],
        scratch_shapes=[pltpu.VMEM((LANES, LANES), F32)],
        compiler_params=_params("arbitrary"),
    )(logits_t)


SC_CORES, SC_SUBCORES = 2, 16
SC_WINDOW = 128


def _sc_mesh():
    return plsc.VectorSubcoreMesh(core_axis_name="core", subcore_axis_name="subcore")


def _sc_split(n_windows):
    if n_windows % (SC_CORES * SC_SUBCORES) == 0:
        return ("core", "subcore")
    assert n_windows % SC_SUBCORES == 0
    return "subcore"


def _sc_scatter_rows(src, idx, n_out):
    rows = src.shape[0]
    win = SC_WINDOW
    n_src_blocks = rows // win

    @functools.partial(pl.kernel, out_type=jax.ShapeDtypeStruct((n_out, LANES), src.dtype),
                       mesh=_sc_mesh(), scratch_types=[])
    def scatter(x_hbm, i_hbm, o_hbm):
        def body(x_vmem, i_vmem):
            pltpu.sync_copy(x_vmem, o_hbm.at[i_vmem.at[0]])

        pltpu.emit_pipeline(
            body, grid=(n_out // win,),
            in_specs=[pl.BlockSpec((win, LANES), lambda i: (i % n_src_blocks, 0)),
                      pl.BlockSpec((1, win), lambda i: (0, i))],
            out_specs=[],
            core_axis_name=_sc_split(n_out // win), dimension_semantics=(pltpu.PARALLEL,),
        )(x_hbm, i_hbm)

    return scatter(src, idx.reshape(1, n_out))


def _sc_gather_rows(src, idx):
    n_out = idx.shape[0]
    win = SC_WINDOW

    @functools.partial(pl.kernel, out_type=jax.ShapeDtypeStruct((n_out, LANES), src.dtype),
                       mesh=_sc_mesh(), scratch_types=[])
    def gather(x_hbm, i_hbm, o_hbm):
        def body(i_vmem, o_vmem):
            pltpu.sync_copy(x_hbm.at[i_vmem.at[0]], o_vmem)

        pltpu.emit_pipeline(
            body, grid=(n_out // win,),
            in_specs=[pl.BlockSpec((1, win), lambda i: (0, i))],
            out_specs=[pl.BlockSpec((win, LANES), lambda i: (i, 0))],
            core_axis_name=_sc_split(n_out // win), dimension_semantics=(pltpu.PARALLEL,),
        )(i_hbm, o_hbm)

    return gather(src, idx.reshape(1, n_out))


def _expert_kernel(blk_ref, e_ref, lo_ref, hi_ref, x_ref, w1_ref, w3_ref, w2_ref, y_ref):
    w = pl.program_id(0)
    lo, hi = lo_ref[w], hi_ref[w]
    row0 = blk_ref[w] * MOE_ROWS

    def compute():
        x = _load_tile_rows(x_ref).astype(BF16)
        a = jnp.dot(x, w1_ref[...], preferred_element_type=F32)
        g = jnp.dot(x, w3_ref[...], preferred_element_type=F32)
        mid = ((a / (1.0 + jnp.exp(-a))) * g).astype(BF16)
        y = jnp.dot(mid, w2_ref[...], preferred_element_type=F32)
        rows = row0 + lax.broadcasted_iota(I32, y.shape, 0)
        return y, (rows >= lo) & (rows < hi)

    @pl.when((hi > lo) & (lo == row0))
    def _():
        y, mine = compute()
        _store_tile_rows(y_ref, jnp.where(mine, y, 0.0))

    @pl.when((hi > lo) & (lo != row0))
    def _():
        y, mine = compute()
        _store_tile_rows(y_ref, jnp.where(mine, y, _load_tile_rows(y_ref)))


def _experts(items, xs, w1, w3, w2):
    d = D_MODEL
    n_items = items[0].shape[0]
    xmap = lambda w, blk, e, lo, hi: (blk[w], 0)
    wmap = lambda w, blk, e, lo, hi: (e[w], 0, 0)
    return pl.pallas_call(
        _expert_kernel,
        grid_spec=pltpu.PrefetchScalarGridSpec(
            num_scalar_prefetch=4, grid=(n_items,),
            in_specs=[pl.BlockSpec((MOE_ROWS * ROW_TILE, LANES), xmap),
                      pl.BlockSpec((None, d, D_EXPERT), wmap), pl.BlockSpec((None, d, D_EXPERT), wmap),
                      pl.BlockSpec((None, D_EXPERT, d), wmap)],
            out_specs=pl.BlockSpec((MOE_ROWS * ROW_TILE, LANES), xmap)),
        out_shape=jax.ShapeDtypeStruct(xs.shape, F32),
        compiler_params=_params("arbitrary"),
    )(*items, xs, w1, w3, w2)


def _work_items(counts, n_rows):
    n_blocks = n_rows // MOE_ROWS
    n_items = n_blocks + N_EXPERTS - 1
    ends = jnp.cumsum(counts)
    starts = ends - counts
    first_blk = starts // MOE_ROWS
    n_blk = jnp.where(counts > 0, (ends - 1) // MOE_ROWS - first_blk + 1, 0)
    item_end = jnp.cumsum(n_blk)
    total = item_end[-1]
    w = jnp.arange(n_items, dtype=I32)
    wc = jnp.minimum(w, total - 1)
    e = jnp.sum(item_end[None, :] <= wc[:, None], axis=1).astype(I32)
    blk = (first_blk[e] + wc - (item_end[e] - n_blk[e])).astype(I32)
    lo = jnp.maximum(starts[e], blk * MOE_ROWS).astype(I32)
    hi = jnp.minimum(ends[e], (blk + 1) * MOE_ROWS).astype(I32)
    live = w < total
    return blk, e, jnp.where(live, lo, 0), jnp.where(live, hi, 0), starts


def _combine_kernel(h_ref, z0_ref, z1_ref, rg_ref, g_ref, *rest):
    y_ref = rest[-1]
    rg = rg_ref[...]
    moe = _load_tile_rows(z0_ref) * rg[:, 0:1] + _load_tile_rows(z1_ref) * rg[:, 1:2]
    y_ref[...] = _rms(h_ref[...] + moe, g_ref[...])


def _combine(h2, z, rg, g_final, tm, y_prev, row0, n_total):
    n, d = h2.shape
    n_t = n // tm
    row = lambda i: (i, 0)
    in_specs = [pl.BlockSpec((tm, d), row), pl.BlockSpec((tm * ROW_TILE, LANES), row),
                pl.BlockSpec((tm * ROW_TILE, LANES), lambda i: (n_t + i, 0)), pl.BlockSpec((tm, 8), row),
                pl.BlockSpec((1, d), lambda i: (0, 0))]
    args = [h2, z, z, rg, g_final.reshape(1, d)]
    aliases = {}
    if y_prev is not None:
        in_specs.append(pl.BlockSpec(memory_space=pl.ANY))
        args.append(y_prev)
        aliases = {len(args) - 1: 0}
    return pl.pallas_call(
        _combine_kernel, grid=(n_t,),
        in_specs=in_specs,
        out_specs=pl.BlockSpec((tm, d), lambda i: (row0 // tm + i, 0)),
        out_shape=jax.ShapeDtypeStruct((n_total, d), F32),
        input_output_aliases=aliases,
        compiler_params=_params("arbitrary"),
    )(*args)


def _moe_and_final(h2, xn, ri, rg, counts, w1, w3, w2, g_final, *, tm, y_prev, row0, n_total):
    n = h2.shape[0]
    blk, e, lo, hi, starts = _work_items(counts, 2 * n)
    expert = jnp.arange(N_EXPERTS, dtype=I32)[:, None, None]
    first = jnp.sum(jnp.where(ri[None, 0:2] == expert, starts.astype(I32)[:, None, None], 0), axis=0)
    pos = (first + ri[2:4]).astype(I32)
    sub = (pos[:, :, None] * ROW_TILE + jnp.arange(ROW_TILE, dtype=I32)).reshape(2 * n * ROW_TILE)
    xs = _sc_scatter_rows(xn, sub, 2 * n * ROW_TILE)
    ys = _experts((blk, e, lo, hi), xs, w1, w3, w2)
    z = _sc_gather_rows(ys, sub)
    return _combine(h2, z, rg.T, g_final, tm, y_prev, row0, n_total)


def _pad_time(a, s_pad):
    return jnp.pad(a, ((0, 0), (0, s_pad - a.shape[1])) + ((0, 0),) * (a.ndim - 2))


def _layer(x, past, mem_k, mem_v, wts, lam_init, g_final, *, tm, tq, tk, moe_parts):
    b, t, d = x.shape
    n = b * t
    p_len = 0 if past is None else past[0].shape[1]
    s_valid = p_len + t
    s_pad = -(-s_valid // tk) * tk
    pos = p_len + jnp.arange(t)

    fk, fv, logf, dk, dv, qf, kf, vf, qd, kd, vd = _inproj(
        x, wts['g_mix'], wts['w_main'], wts['w_ff'], wts['b_forget'], pos, tm)
    time_major = tm <= t

    def leaf(a, dims):
        if time_major:
            return jnp.moveaxis(a.reshape((b,) + dims + (t,)), -1, 1)
        return a.reshape((b, t) + dims)

    new = (leaf(fk, (FOX_HEADS, HEAD_DIM)), leaf(fv, (FOX_HEADS, HEAD_DIM)), leaf(logf, (FOX_HEADS,)),
           leaf(dk, (DIFF_HEADS, 2, HEAD_DIM)), dv.reshape(b, t, DIFF_HEADS, 2 * HEAD_DIM))

    def keys_t(cur, old_rows):
        if not time_major:
            cur = jnp.swapaxes(cur.reshape(b, t, FOX_W), 1, 2)
        if old_rows is not None:
            prev = jnp.swapaxes(old_rows.reshape(b, p_len, FOX_W).astype(BF16), 1, 2)
            cur = jnp.concatenate([prev, cur], axis=2)
        return jnp.pad(cur, ((0, 0), (0, 0), (0, s_pad - s_valid)))

    def with_past(new_b16, old_rows):
        cur = new_b16.reshape(b, t, FOX_W)
        if old_rows is not None:
            cur = jnp.concatenate([old_rows.reshape(b, p_len, FOX_W).astype(BF16), cur], axis=1)
        return _pad_time(cur, s_pad)

    logf_t = logf if time_major else jnp.swapaxes(logf.reshape(b, t, FOX_HEADS), 1, 2)
    if past is not None:
        logf_t = jnp.concatenate([jnp.swapaxes(past[2].astype(F32), 1, 2), logf_t], axis=2)
    k_bias, q_bias_t = _fox_bias(jnp.pad(logf_t, ((0, 0), (0, 0), (0, s_pad - s_valid))))
    q_bias = jnp.swapaxes(q_bias_t[:, :, p_len:p_len + t], 1, 2).reshape(n, LANES)

    old = (None,) * 5 if past is None else past
    geom = dict(b=b, t=t, s_pad=s_pad, s_valid=s_valid, past=p_len, tq=tq, tk=tk, lam_init=lam_init)
    fox_o = _self_attention(qf, keys_t(kf, old[0]), with_past(vf, old[1]), (q_bias, k_bias),
                            fox=True, **geom)
    diff_o = _self_attention(qd, keys_t(kd, old[3]), with_past(vd, old[4]),
                             (wts['lam_vecs'], wts['g_diff']), fox=False, **geom)
    h1, qm = _outproj(x.reshape(n, d), fox_o, diff_o, wts['wo_f'], wts['wo_d'], wts['g_mem_q'],
                      wts['w_mq'], tm)
    assert b % moe_parts == 0
    bp = b // moe_parts
    y = None
    for part in range(moe_parts):
        h2, xn, logits = _memattn(qm, mem_k, mem_v, h1, wts['w_mo'], wts['g_ffn'], wts['w_router'],
                                  wts['b_router'], b0=part * bp, b=bp, t=t, tq=tq)
        ri, rg, cnt = _router(logits.T, min(tm, bp * t))
        counts = cnt[:N_EXPERTS, 0].astype(I32)
        y = _moe_and_final(h2, xn, ri, rg, counts, wts['w1'], wts['w3'], wts['w2'], g_final, tm=tm,
                           y_prev=y, row0=part * bp * t, n_total=n)
    return y.reshape(b, t, d), new


def kernel(x_prompt, x_sample, cache_fox_k, cache_fox_v, cache_fox_logf, cache_diff_k, cache_diff_v, cache_mem_k, cache_mem_v, mem_prompt, g_mix, w_in, b_forget, lam_q1, lam_k1, lam_q2, lam_k2, g_diff, w_o, g_mem_q, g_mem_kv, w_mq, w_mk, w_mv, w_mo, g_ffn, w_group, b_group, w_erouter, b_erouter, w1, w3, w2, g_final):
    depth = w_in.shape[0]
    assert depth == 1
    l = 0
    lam_init = 0.8 - 0.6 * math.exp(-0.3 * l)
    d = D_MODEL
    w = w_in[l]
    ff0 = 3 * FOX_W
    w_router = jnp.zeros((d, LANES), F32).at[:, :N_EXPERTS].set(w_erouter[l])
    w_router = w_router.at[:, GROUP_LANE0:GROUP_LANE0 + N_GROUPS].set(w_group[l])
    b_router = jnp.zeros((1, LANES), F32).at[0, :N_EXPERTS].set(b_erouter[l])
    b_router = b_router.at[0, GROUP_LANE0:GROUP_LANE0 + N_GROUPS].set(b_group[l])
    wts = {
        'g_mix': g_mix[l],
        'w_main': jnp.concatenate([w[:, :ff0], w[:, ff0 + FOX_HEADS:]], axis=1).astype(BF16),
        'w_ff': jnp.pad(w[:, ff0:ff0 + FOX_HEADS], ((0, 0), (0, LANES - FOX_HEADS))).astype(BF16),
        'b_forget': b_forget[l],
        'lam_vecs': jnp.stack([lam_q1[l], lam_k1[l], lam_q2[l], lam_k2[l]]),
        'g_diff': g_diff[l],
        'wo_f': w_o[l][:FOX_W].astype(BF16), 'wo_d': w_o[l][FOX_W:].astype(BF16),
        'g_mem_q': g_mem_q[l], 'w_mq': w_mq[l].astype(BF16), 'w_mo': w_mo[l].astype(BF16),
        'g_ffn': g_ffn[l], 'w_router': w_router.astype(BF16), 'b_router': b_router,
        'w1': w1[l].astype(BF16), 'w3': w3[l].astype(BF16), 'w2': w2[l].astype(BF16),
    }

    bp, tp, _ = x_prompt.shape
    mk, mv, mk_b, mv_b = _memkv(mem_prompt.reshape(bp * MEM_LEN, d), g_mem_kv[l],
                                w_mk[l].astype(BF16), w_mv[l].astype(BF16), 512)
    yp, new_p = _layer(x_prompt, None, mk_b.reshape(bp, MEM_LEN, d), mv_b.reshape(bp, MEM_LEN, d),
                       wts, lam_init, g_final, tm=512, tq=512, tk=512, moe_parts=2)

    bs, ts, _ = x_sample.shape
    past = (cache_fox_k[l], cache_fox_v[l], cache_fox_logf[l], cache_diff_k[l], cache_diff_v[l])
    ys, new_s = _layer(x_sample, past, cache_mem_k[l].reshape(bs, MEM_LEN, d).astype(BF16),
                       cache_mem_v[l].reshape(bs, MEM_LEN, d).astype(BF16),
                       wts, lam_init, g_final, tm=bs * ts, tq=ts, tk=512, moe_parts=1)

    mem_shape = (1, bp, MEM_LEN, MEM_HEADS, MEM_HEAD_DIM)
    return (yp, ys) + tuple(a[None] for a in new_p) + (mk.reshape(mem_shape), mv.reshape(mem_shape)) \
        + tuple(a[None] for a in new_s)
```

```python
import functools
import math

import jax
import jax.numpy as jnp
from jax import lax
from jax.experimental import pallas as pl
from jax.experimental.pallas import tpu as pltpu
from jax.experimental.pallas import tpu_sc as plsc

F32 = jnp.float32
BF16 = jnp.bfloat16
I32 = jnp.int32

D_MODEL = 1024
HEAD_DIM = 64
FOX_HEADS = 8
DIFF_HEADS = 4
FOX_W = FOX_HEADS * HEAD_DIM
DIFF_W = DIFF_HEADS * 2 * HEAD_DIM
ROT_DIM = HEAD_DIM // 4
ROPE_THETA = 500000.0
CHUNK = 64
MEM_LEN = 256
MEM_HEADS = 4
MEM_HEAD_DIM = D_MODEL // MEM_HEADS
N_GROUPS = 4
EXPERTS_PER_GROUP = 8
N_EXPERTS = N_GROUPS * EXPERTS_PER_GROUP
D_EXPERT = 512
EPS = 1e-6

LANES = 128
LOG2E = 1.4426950408889634
NEG = -1e30
VMEM_LIMIT_BYTES = 56 * 1024 * 1024
MOE_ROWS = 256
GROUP_LANE0 = N_EXPERTS


def _params(*sem):
    return pltpu.CompilerParams(dimension_semantics=sem, vmem_limit_bytes=VMEM_LIMIT_BYTES)


def _rms(x, g):
    return (x * lax.rsqrt(jnp.mean(x * x, axis=-1, keepdims=True) + EPS)) * g


def _rope(x, cos, sin_lo, sin_hi):
    outs = []
    for c in range(x.shape[1] // LANES):
        xc = x[:, c * LANES:(c + 1) * LANES]
        up = pltpu.roll(xc, LANES - ROT_DIM // 2, axis=1)
        dn = pltpu.roll(xc, ROT_DIM // 2, axis=1)
        outs.append(xc * cos + up * sin_lo + dn * sin_hi)
    return jnp.concatenate(outs, axis=1)


def _inproj_kernel(x_ref, g_ref, w_ref, wf_ref, bf_ref, cos_ref, slo_ref, shi_ref,
                   fk_ref, fv_ref, logf_ref, dk_ref, dv_ref,
                   qf_ref, kf_ref, vf_ref, qd_ref, kd_ref, vd_ref, *, q_scale, time_major):
    n = _rms(x_ref[...], g_ref[...]).astype(BF16)
    turn = (lambda a: a.T) if time_major else (lambda a: a)

    def proj(c):
        return jnp.dot(n, w_ref[:, c * FOX_W:(c + 1) * FOX_W], preferred_element_type=F32)

    cos, slo, shi = cos_ref[...], slo_ref[...], shi_ref[...]
    qf_ref[...] = (proj(0) * q_scale).astype(BF16)
    fk = turn(proj(1))
    fk_ref[...] = fk
    kf_ref[...] = fk.astype(BF16)
    fv = proj(2)
    fv_ref[...] = turn(fv)
    vf_ref[...] = fv.astype(BF16)
    qd_ref[...] = (_rope(proj(3), cos, slo, shi) * q_scale).astype(BF16)
    dk = turn(_rope(proj(4), cos, slo, shi))
    dk_ref[...] = dk
    kd_ref[...] = dk.astype(BF16)
    dv = proj(5)
    for h in range(DIFF_HEADS):
        dv_ref[pl.ds(h, dv.shape[0], stride=DIFF_HEADS), :] = dv[:, h * LANES:(h + 1) * LANES]
    vd_ref[...] = dv.astype(BF16)
    z = jnp.dot(n, wf_ref[...], preferred_element_type=F32) + bf_ref[...]
    logf = jnp.minimum(z, 0.0) - jnp.log1p(jnp.exp(-jnp.abs(z)))
    logf_ref[...] = logf.T[:FOX_HEADS] if time_major else logf[:, :FOX_HEADS]


def _rope_tables(pos):
    half = ROT_DIM // 2
    inv = jnp.power(ROPE_THETA, -jnp.arange(half, dtype=F32) / half)
    ang = pos.astype(F32)[:, None] * inv[None, :]
    cos, sin = jnp.cos(ang), jnp.sin(ang)
    t = pos.shape[0]
    pad = jnp.zeros((t, HEAD_DIM - ROT_DIM), F32)
    zero = jnp.zeros((t, half), F32)
    cos64 = jnp.concatenate([cos, cos, pad + 1.0], axis=1)
    slo64 = jnp.concatenate([-sin, zero, pad], axis=1)
    shi64 = jnp.concatenate([zero, sin, pad], axis=1)
    return tuple(jnp.tile(a, (1, LANES // HEAD_DIM)) for a in (cos64, slo64, shi64))


def _inproj(x, g, w_main, w_ff, b_ff, pos, tm):
    b, t, d = x.shape
    n = b * t
    tables = _rope_tables(pos)
    time_major = tm <= t
    if time_major:
        assert t % tm == 0
        per = t // tm
        tab_map = lambda i: (i % per, 0)
        turned = pl.BlockSpec((None, FOX_W, tm), lambda i: (i // per, 0, i % per))
        gate = pl.BlockSpec((None, FOX_HEADS, tm), lambda i: (i // per, 0, i % per))
        turned_shape, gate_shape = (b, FOX_W, t), (b, FOX_HEADS, t)
    else:
        assert tm % t == 0
        tables = tuple(jnp.tile(a, (tm // t, 1)) for a in tables)
        tab_map = lambda i: (0, 0)
    assert n % tm == 0
    row = lambda i: (i, 0)
    const = lambda i: (0, 0)
    wide = pl.BlockSpec((tm, FOX_W), row)
    if not time_major:
        turned, gate = wide, pl.BlockSpec((tm, FOX_HEADS), row)
        turned_shape, gate_shape = (n, FOX_W), (n, FOX_HEADS)
    tab = pl.BlockSpec((tm, LANES), tab_map)
    sds = jax.ShapeDtypeStruct
    return pl.pallas_call(
        functools.partial(_inproj_kernel, q_scale=HEAD_DIM ** -0.5 * LOG2E, time_major=time_major),
        grid=(n // tm,),
        in_specs=[pl.BlockSpec((tm, d), row), pl.BlockSpec((1, d), const),
                  pl.BlockSpec(w_main.shape, const), pl.BlockSpec(w_ff.shape, const),
                  pl.BlockSpec((1, LANES), const), tab, tab, tab],
        out_specs=[turned, turned, gate, turned, pl.BlockSpec((tm * DIFF_HEADS, LANES), row),
                   wide, turned, wide, wide, turned, wide],
        out_shape=[sds(turned_shape, F32), sds(turned_shape, F32), sds(gate_shape, F32),
                   sds(turned_shape, F32), sds((n * DIFF_HEADS, LANES), F32),
                   sds((n, FOX_W), BF16), sds(turned_shape, BF16), sds((n, FOX_W), BF16),
                   sds((n, FOX_W), BF16), sds(turned_shape, BF16), sds((n, FOX_W), BF16)],
        compiler_params=_params("arbitrary"),
    )(x.reshape(n, d), g.reshape(1, d), w_main, w_ff,
      jnp.pad(b_ff.reshape(1, FOX_HEADS), ((0, 0), (0, LANES - FOX_HEADS))), *tables)


AUX_LANES = 8


def _bf16_part(x):
    return pltpu.bitcast(pltpu.bitcast(x, jnp.uint32) & jnp.uint32(0xFFFF0000), F32)


def _cumsum_kernel(x_ref, kb_ref, qb_ref):
    x = x_ref[...]
    s_len = x.shape[1]
    lane = lax.broadcasted_iota(I32, x.shape, 1)
    shift = 1
    while shift < s_len:
        x = x + jnp.where(lane >= shift, pltpu.roll(x, shift, axis=1), 0.0)
        shift *= 2
    c = x * LOG2E
    hi = _bf16_part(c)
    mid = _bf16_part(c - hi)
    lo = _bf16_part(c - hi - mid)
    slot = lax.broadcasted_iota(I32, (LANES, x.shape[0]), 0)
    head = lax.broadcasted_iota(I32, (LANES, x.shape[0]), 1)

    def place(first_slot):
        out = jnp.zeros((LANES, s_len), F32)
        for i, part in enumerate((hi, mid, lo)):
            sel = jnp.where(slot == head * AUX_LANES + first_slot + i, 1.0, 0.0).astype(BF16)
            out = out + jnp.dot(sel, part.astype(BF16), preferred_element_type=F32)
        return out

    slot_col = lax.broadcasted_iota(I32, (LANES, s_len), 0)
    used = slot_col < FOX_HEADS * AUX_LANES
    within = slot_col % AUX_LANES
    kb_ref[...] = (jnp.where(used & (within < 3), 1.0, 0.0) - place(3)).astype(BF16)
    qb_ref[...] = (jnp.where(used & (within >= 3) & (within < 6), 1.0, 0.0) + place(0)).astype(BF16)


def _fox_bias(logf_t):
    b, h, s = logf_t.shape
    x = jnp.pad(logf_t, ((0, 0), (0, 16 - h), (0, 0)))
    out = pl.BlockSpec((None, LANES, s), lambda i: (i, 0, 0))
    return pl.pallas_call(
        _cumsum_kernel, grid=(b,), in_specs=[pl.BlockSpec((None, 16, s), lambda i: (i, 0, 0))],
        out_specs=[out, out],
        out_shape=[jax.ShapeDtypeStruct((b, LANES, s), BF16)] * 2,
        compiler_params=_params("arbitrary"),
    )(x)


def _attn_kernel(*refs, tq, tk, past, s_valid, fox, lam_init):
    if fox:
        q_ref, qa_ref, k_ref, ka_ref, v_ref, o_ref, sa_ref, sb_ref, mx_ref, m_ref, l_ref, acc_ref = refs
    else:
        q_ref, k_ref, v_ref, lam_ref, g_ref, o_ref, sa_ref, sb_ref, mx_ref, m_ref, l_ref, acc_ref = refs
    i = pl.program_id(2)
    q = q_ref[...]
    lane = lax.broadcasted_iota(I32, (tq, LANES), 1)
    low = lane < HEAD_DIM
    zero = jnp.zeros_like(q)
    q_part = [jnp.where(low, q, zero), jnp.where(low, zero, q)]
    if fox:
        qa = qa_ref[...]
        for h in range(2):
            lane0 = (2 * pl.program_id(1) + h) * AUX_LANES
            aux = jnp.where((lane >= lane0) & (lane < lane0 + AUX_LANES), qa, zero)
            q_part[h] = jnp.concatenate([q_part[h], aux], axis=1)
    m_ref[...] = jnp.full(m_ref.shape, NEG, F32)
    l_ref[...] = jnp.zeros(l_ref.shape, F32)
    acc_ref[...] = jnp.zeros(acc_ref.shape, F32)
    q0 = past + i * tq
    nb = q0 // tk
    bufs = (sa_ref, sb_ref)

    def scores(j, buf):
        ks = pl.multiple_of(j * tk, tk)
        k = k_ref[:, pl.ds(ks, tk)]
        if fox:
            k = jnp.concatenate([k, ka_ref[:, pl.ds(ks, tk)]], axis=0)
        for h in range(2):
            s = jnp.dot(q_part[h], k, preferred_element_type=F32)
            bufs[buf][h] = s
            mx_ref[buf, h] = jnp.broadcast_to(jnp.max(s, axis=1, keepdims=True), (tq, LANES))

    def consume(j, buf, use_mask):
        ks = pl.multiple_of(j * tk, tk)
        v = jnp.concatenate([v_ref[pl.ds(ks, tk), :], jnp.ones((tk, LANES), BF16)], axis=1)
        pv, alpha = [], []
        for h in range(2):
            t = bufs[buf][h]
            if use_mask:
                qpos = q0 + lax.broadcasted_iota(I32, (tq, tk), 0)
                kpos = j * tk + lax.broadcasted_iota(I32, (tq, tk), 1)
                if fox:
                    seen = kpos <= qpos
                else:
                    seen = kpos < jnp.minimum((qpos // CHUNK + 1) * CHUNK, s_valid)
                t = jnp.where(seen, t, NEG)
                mx = jnp.max(t, axis=1, keepdims=True)
            else:
                mx = mx_ref[buf, h]
            m_prev = m_ref[h]
            m_new = jnp.maximum(m_prev, mx)
            a = jnp.exp2(m_prev - m_new)
            p = jnp.exp2(t - jnp.concatenate([m_new] * (tk // LANES), axis=1))
            m_ref[h] = m_new
            both = jnp.dot(p.astype(BF16), v, preferred_element_type=F32)
            pv.append(both[:, :LANES])
            l_ref[h] = a * l_ref[h] + both[:, LANES:]
            alpha.append(a)
        if fox:
            acc_ref[0] = acc_ref[0] * jnp.where(low, alpha[0], alpha[1]) + jnp.where(low, pv[0], pv[1])
        else:
            for h in range(2):
                acc_ref[h] = acc_ref[h] * alpha[h] + pv[h]

    scores(0, 0)

    def pair(jj, c):
        j = 2 * jj
        scores(j + 1, 1)
        consume(j, 0, False)
        scores(j + 2, 0)
        consume(j + 1, 1, False)
        return c

    lax.fori_loop(0, nb // 2, pair, 0)

    @pl.when(nb % 2 == 1)
    def _():
        scores(nb, 1)
        consume(nb - 1, 0, False)
        consume(nb, 1, True)

    @pl.when(nb % 2 == 0)
    def _():
        consume(nb, 0, True)

    if fox:
        o = acc_ref[0] / jnp.where(low, l_ref[0], l_ref[1])
    else:
        lam_v = lam_ref[...]
        lam = (jnp.exp(jnp.sum(lam_v[0:1] * lam_v[1:2], axis=1, keepdims=True))
               - jnp.exp(jnp.sum(lam_v[2:3] * lam_v[3:4], axis=1, keepdims=True)) + lam_init)
        o = acc_ref[0] / l_ref[0] - lam * (acc_ref[1] / l_ref[1])
        o = _rms(o, g_ref[...]) * (1.0 - lam_init)
    o_ref[...] = o.astype(o_ref.dtype)


def _self_attention(q, k, v, extra, *, b, t, s_pad, s_valid, past, tq, tk, fox, lam_init):
    assert tk % tq == 0 and past % tq == 0 and t % tq == 0 and tk % CHUNK == 0 and s_pad % tk == 0
    assert tq % CHUNK == 0 or (CHUNK % tq == 0 and past % CHUNK == 0)
    n_q = t // tq
    k_spec = pl.BlockSpec((None, LANES, s_pad), lambda bi, p, i: (bi, p, 0))
    v_spec = pl.BlockSpec((None, s_pad, LANES), lambda bi, p, i: (bi, 0, p))
    q_spec = pl.BlockSpec((tq, LANES), lambda bi, p, i: (bi * n_q + i, p))
    const = lambda bi, p, i: (0, 0)
    if fox:
        in_specs = [q_spec, pl.BlockSpec((tq, LANES), lambda bi, p, i: (bi * n_q + i, 0)), k_spec,
                    pl.BlockSpec((None, LANES, s_pad), lambda bi, p, i: (bi, 0, 0)), v_spec]
        args = (q, extra[0], k, extra[1], v)
    else:
        in_specs = [q_spec, k_spec, v_spec, pl.BlockSpec((4, HEAD_DIM), const), pl.BlockSpec((1, LANES), const)]
        args = (q, k, v, extra[0], extra[1].reshape(1, LANES))
    stat = pltpu.VMEM((2, tq, LANES), F32)
    return pl.pallas_call(
        functools.partial(_attn_kernel, tq=tq, tk=tk, past=past, s_valid=s_valid, fox=fox, lam_init=lam_init),
        grid=(b, FOX_W // LANES, n_q),
        in_specs=in_specs,
        out_specs=q_spec,
        out_shape=jax.ShapeDtypeStruct((b * t, FOX_W), BF16),
        scratch_shapes=[pltpu.VMEM((2, tq, tk), F32), pltpu.VMEM((2, tq, tk), F32),
                        pltpu.VMEM((2, 2, tq, LANES), F32), stat, stat, stat],
        compiler_params=_params("arbitrary", "arbitrary", "arbitrary"),
    )(*args)


def _outproj_kernel(x_ref, fo_ref, do_ref, wof_ref, wod_ref, g_ref, wq_ref, h_ref, q_ref, *, q_scale):
    h = (x_ref[...] + jnp.dot(fo_ref[...], wof_ref[...], preferred_element_type=F32)
         + jnp.dot(do_ref[...], wod_ref[...], preferred_element_type=F32))
    h_ref[...] = h
    n = _rms(h, g_ref[...]).astype(BF16)
    q_ref[...] = (jnp.dot(n, wq_ref[...], preferred_element_type=F32) * q_scale).astype(BF16)


def _outproj(x2, fox_o, diff_o, wo_f, wo_d, g_mem_q, w_mq, tm):
    n, d = x2.shape
    row = lambda i: (i, 0)
    const = lambda i: (0, 0)
    return pl.pallas_call(
        functools.partial(_outproj_kernel, q_scale=MEM_HEAD_DIM ** -0.5 * LOG2E),
        grid=(n // tm,),
        in_specs=[pl.BlockSpec((tm, d), row), pl.BlockSpec((tm, FOX_W), row),
                  pl.BlockSpec((tm, DIFF_W), row), pl.BlockSpec((FOX_W, d), const),
                  pl.BlockSpec((DIFF_W, d), const), pl.BlockSpec((1, d), const),
                  pl.BlockSpec((d, d), const)],
        out_specs=[pl.BlockSpec((tm, d), row), pl.BlockSpec((tm, d), row)],
        out_shape=[jax.ShapeDtypeStruct((n, d), F32), jax.ShapeDtypeStruct((n, d), BF16)],
        compiler_params=_params("arbitrary"),
    )(x2, fox_o, diff_o, wo_f, wo_d, g_mem_q.reshape(1, d), w_mq)


def _memkv_kernel(m_ref, g_ref, wk_ref, wv_ref, k_ref, v_ref, kb_ref, vb_ref):
    n = _rms(m_ref[...], g_ref[...]).astype(BF16)
    k = jnp.dot(n, wk_ref[...], preferred_element_type=F32)
    v = jnp.dot(n, wv_ref[...], preferred_element_type=F32)
    k_ref[...] = k
    v_ref[...] = v
    kb_ref[...] = k.astype(BF16)
    vb_ref[...] = v.astype(BF16)


def _memkv(mem2, g, wk, wv, tm):
    n, d = mem2.shape
    row = lambda i: (i, 0)
    const = lambda i: (0, 0)
    blk = pl.BlockSpec((tm, d), row)
    wspec = pl.BlockSpec((d, d), const)
    return pl.pallas_call(
        _memkv_kernel, grid=(n // tm,),
        in_specs=[blk, pl.BlockSpec((1, d), const), wspec, wspec],
        out_specs=[blk, blk, blk, blk],
        out_shape=[jax.ShapeDtypeStruct((n, d), F32)] * 2 + [jax.ShapeDtypeStruct((n, d), BF16)] * 2,
        compiler_params=_params("arbitrary"),
    )(mem2, g.reshape(1, d), wk, wv)


U32 = jnp.uint32
HALF_D = D_MODEL // 2
ROW_TILE = HALF_D // LANES
BF16_BITS = 0xFFFF0000


def _pack_rows(val):
    def rounded(x):
        bits = pltpu.bitcast(x, U32)
        return (bits + U32(0x7FFF) + ((bits >> 16) & U32(1))) & U32(BF16_BITS)
    return rounded(val[:, :HALF_D]) | (rounded(val[:, HALF_D:]) >> 16)


def _unpack_rows(words):
    return jnp.concatenate([pltpu.bitcast(words & U32(BF16_BITS), F32), pltpu.bitcast(words << 16, F32)], axis=1)


def _load_tile_rows(ref):
    rows = ref.shape[0] // ROW_TILE
    return jnp.concatenate([ref[pl.ds(s, rows, stride=ROW_TILE), :] for s in range(ROW_TILE)], axis=1)


def _store_tile_rows(ref, words):
    rows = words.shape[0]
    for s in range(ROW_TILE):
        ref[pl.ds(s, rows, stride=ROW_TILE), :] = words[:, s * LANES:(s + 1) * LANES]


def _memattn_kernel(q_ref, k_ref, v_ref, h_ref, wo_ref, g_ref, wr_ref, br_ref,
                    h2_ref, xn_ref, logit_ref):
    q = q_ref[...]
    outs = []
    for h in range(MEM_HEADS):
        sl = slice(h * MEM_HEAD_DIM, (h + 1) * MEM_HEAD_DIM)
        s = lax.dot_general(q[:, sl], k_ref[:, sl], (((1,), (1,)), ((), ())), preferred_element_type=F32)
        p = jnp.exp2(s - jnp.max(s, axis=1, keepdims=True))
        o = jnp.dot(p.astype(BF16), v_ref[:, sl], preferred_element_type=F32)
        outs.append((o / jnp.sum(p, axis=1, keepdims=True)).astype(BF16))
    h2 = h_ref[...] + jnp.dot(jnp.concatenate(outs, axis=1), wo_ref[...], preferred_element_type=F32)
    h2_ref[...] = h2
    xn_f32 = _rms(h2, g_ref[...])
    _store_tile_rows(xn_ref, _pack_rows(xn_f32))
    logit_ref[...] = jnp.dot(xn_f32.astype(BF16), wr_ref[...], preferred_element_type=F32) + br_ref[...]


def _memattn(qm, mk, mv, h1, w_mo, g_ffn, w_router, b_router, *, b0, b, t, tq):
    d = h1.shape[1]
    n = b * t
    n_q = t // tq
    row = lambda bi, i: (bi * n_q + i, 0)
    src_row = lambda bi, i: ((b0 + bi) * n_q + i, 0)
    const = lambda bi, i: (0, 0)
    mem = pl.BlockSpec((None, MEM_LEN, d), lambda bi, i: (b0 + bi, 0, 0))
    src = pl.BlockSpec((tq, d), src_row)
    blk = pl.BlockSpec((tq, d), row)
    tiled = pl.BlockSpec((tq * ROW_TILE, LANES), row)
    return pl.pallas_call(
        _memattn_kernel,
        grid=(b, n_q),
        in_specs=[src, mem, mem, src, pl.BlockSpec((d, d), const), pl.BlockSpec((1, d), const),
                  pl.BlockSpec((d, LANES), const), pl.BlockSpec((1, LANES), const)],
        out_specs=[blk, tiled, pl.BlockSpec((tq, LANES), row)],
        out_shape=[jax.ShapeDtypeStruct((n, d), F32), jax.ShapeDtypeStruct((n * ROW_TILE, LANES), U32),
                   jax.ShapeDtypeStruct((n, LANES), F32)],
        compiler_params=_params("arbitrary", "arbitrary"),
    )(qm, mk, mv, h1, w_mo, g_ffn.reshape(1, d), w_router, b_router)


ROUTER_ROWS = 48


def _first_slot(hit, slot_f32):
    return jnp.min(jnp.where(hit, slot_f32, float(LANES)), axis=0, keepdims=True).astype(I32)


def _router_kernel(lt_ref, ri_ref, rg_ref, cnt_ref, carry_ref, *, rt):
    @pl.when(pl.program_id(0) == 0)
    def _():
        carry_ref[...] = jnp.zeros(carry_ref.shape, F32)

    lt = lt_ref[0:ROUTER_ROWS, :]
    slot = lax.broadcasted_iota(I32, (ROUTER_ROWS, rt), 0)
    slot_f = slot.astype(F32)
    g_log = jnp.where((slot >= GROUP_LANE0) & (slot < GROUP_LANE0 + N_GROUPS), lt, NEG)
    g_max = jnp.max(g_log, axis=0, keepdims=True)
    g_idx = _first_slot(g_log == g_max, slot_f) - GROUP_LANE0
    g_p = 1.0 / jnp.sum(jnp.exp(g_log - g_max), axis=0, keepdims=True)
    in_group = (slot >= g_idx * EXPERTS_PER_GROUP) & (slot < (g_idx + 1) * EXPERTS_PER_GROUP)
    e_log = jnp.where(in_group, lt, NEG)
    e_exp = jnp.exp(e_log - jnp.max(e_log, axis=0, keepdims=True))
    prob = jnp.where(in_group, e_exp / jnp.sum(e_exp, axis=0, keepdims=True), -1.0)
    p1 = jnp.max(prob, axis=0, keepdims=True)
    i1 = _first_slot(prob == p1, slot_f)
    rest = jnp.where(slot == i1, -1.0, prob)
    p2 = jnp.max(rest, axis=0, keepdims=True)
    i2 = _first_slot(rest == p2, slot_f)
    top_sum = p1 + p2

    hot1, hot2 = slot == i1, slot == i2
    onehot = jnp.where(hot1 | hot2, 1.0, 0.0)
    earlier = lax.broadcasted_iota(I32, (rt, rt), 0) < lax.broadcasted_iota(I32, (rt, rt), 1)
    before = jnp.dot(onehot.astype(BF16), jnp.where(earlier, 1.0, 0.0).astype(BF16),
                     preferred_element_type=F32) + carry_ref[0:ROUTER_ROWS, 0:1]
    r1 = jnp.sum(jnp.where(hot1, before, 0.0), axis=0, keepdims=True)
    r2 = jnp.sum(jnp.where(hot2, before, 0.0), axis=0, keepdims=True)
    carry_ref[0:ROUTER_ROWS, :] = carry_ref[0:ROUTER_ROWS, :] + jnp.sum(onehot, axis=1, keepdims=True)
    cnt_ref[...] = carry_ref[...]

    ri_ref[...] = jnp.concatenate([i1, i2, r1.astype(I32), r2.astype(I32), jnp.zeros((4, rt), I32)], axis=0)
    rg_ref[...] = jnp.concatenate([g_p * p1 / top_sum, g_p * p2 / top_sum, jnp.zeros((6, rt), F32)], axis=0)


def _router(logits_t, rt):
    n = logits_t.shape[1]
    col = lambda i: (0, i)
    return pl.pallas_call(
        functools.partial(_router_kernel, rt=rt),
        grid=(n // rt,),
        in_specs=[pl.BlockSpec((LANES, rt), col)],
        out_specs=[pl.BlockSpec((8, rt), col), pl.BlockSpec((8, rt), col),
                   pl.BlockSpec((LANES, LANES), lambda i: (0, 0))],
        out_shape=[jax.ShapeDtypeStruct((8, n), I32), jax.ShapeDtypeStruct((8, n), F32),
                   jax.ShapeDtypeStruct((LANES, LANES), F32)],
        scratch_shapes=[pltpu.VMEM((LANES, LANES), F32)],
        compiler_params=_params("arbitrary"),
    )(logits_t)


SC_CORES, SC_SUBCORES = 2, 16
SC_WINDOW = 128


def _sc_mesh():
    return plsc.VectorSubcoreMesh(core_axis_name="core", subcore_axis_name="subcore")


def _sc_split(n_windows):
    if n_windows % (SC_CORES * SC_SUBCORES) == 0:
        return ("core", "subcore")
    assert n_windows % SC_SUBCORES == 0
    return "subcore"


def _sc_repeat(idx):
    n_windows = idx.shape[0] // SC_WINDOW
    reps = SC_SUBCORES // math.gcd(n_windows, SC_SUBCORES)
    return jnp.tile(idx, reps)


def _sc_scatter_rows(src, idx, n_out):
    rows = src.shape[0]
    win = SC_WINDOW
    n_src_blocks = rows // win
    idx = _sc_repeat(idx)
    n_idx = idx.shape[0]

    @functools.partial(pl.kernel, out_type=jax.ShapeDtypeStruct((n_out, LANES), src.dtype),
                       mesh=_sc_mesh(), scratch_types=[])
    def scatter(x_hbm, i_hbm, o_hbm):
        def body(x_vmem, i_vmem):
            pltpu.sync_copy(x_vmem, o_hbm.at[i_vmem.at[0]])

        pltpu.emit_pipeline(
            body, grid=(n_idx // win,),
            in_specs=[pl.BlockSpec((win, LANES), lambda i: (i % n_src_blocks, 0)),
                      pl.BlockSpec((1, win), lambda i: (0, i))],
            out_specs=[],
            core_axis_name=_sc_split(n_idx // win), dimension_semantics=(pltpu.PARALLEL,),
        )(x_hbm, i_hbm)

    return scatter(src, idx.reshape(1, n_idx))


def _sc_gather_rows(src, idx):
    n_rows = idx.shape[0]
    idx = _sc_repeat(idx)
    n_out = idx.shape[0]
    win = SC_WINDOW

    @functools.partial(pl.kernel, out_type=jax.ShapeDtypeStruct((n_out, LANES), src.dtype),
                       mesh=_sc_mesh(), scratch_types=[])
    def gather(x_hbm, i_hbm, o_hbm):
        def body(i_vmem, o_vmem):
            pltpu.sync_copy(x_hbm.at[i_vmem.at[0]], o_vmem)

        pltpu.emit_pipeline(
            body, grid=(n_out // win,),
            in_specs=[pl.BlockSpec((1, win), lambda i: (0, i))],
            out_specs=[pl.BlockSpec((win, LANES), lambda i: (i, 0))],
            core_axis_name=_sc_split(n_out // win), dimension_semantics=(pltpu.PARALLEL,),
        )(i_hbm, o_hbm)

    return gather(src, idx.reshape(1, n_out))[:n_rows]


def _expert_kernel(blk_ref, e_ref, lo_ref, hi_ref, x_ref, w1_ref, w3_ref, w2_ref, y_ref):
    w = pl.program_id(0)
    lo, hi = lo_ref[w], hi_ref[w]
    row0 = blk_ref[w] * MOE_ROWS

    def compute():
        x = _unpack_rows(_load_tile_rows(x_ref)).astype(BF16)
        a = jnp.dot(x, w1_ref[...], preferred_element_type=F32)
        g = jnp.dot(x, w3_ref[...], preferred_element_type=F32)
        mid = ((a / (1.0 + jnp.exp(-a))) * g).astype(BF16)
        y = _pack_rows(jnp.dot(mid, w2_ref[...], preferred_element_type=F32))
        rows = row0 + lax.broadcasted_iota(I32, y.shape, 0)
        return y, (rows >= lo) & (rows < hi)

    @pl.when((hi > lo) & (lo == row0))
    def _():
        y, mine = compute()
        _store_tile_rows(y_ref, jnp.where(mine, y, jnp.zeros_like(y)))

    @pl.when((hi > lo) & (lo != row0))
    def _():
        y, mine = compute()
        _store_tile_rows(y_ref, jnp.where(mine, y, _load_tile_rows(y_ref)))


def _experts(items, xs, w1, w3, w2):
    d = D_MODEL
    n_items = items[0].shape[0]
    xmap = lambda w, blk, e, lo, hi: (blk[w], 0)
    wmap = lambda w, blk, e, lo, hi: (e[w], 0, 0)
    return pl.pallas_call(
        _expert_kernel,
        grid_spec=pltpu.PrefetchScalarGridSpec(
            num_scalar_prefetch=4, grid=(n_items,),
            in_specs=[pl.BlockSpec((MOE_ROWS * ROW_TILE, LANES), xmap),
                      pl.BlockSpec((None, d, D_EXPERT), wmap), pl.BlockSpec((None, d, D_EXPERT), wmap),
                      pl.BlockSpec((None, D_EXPERT, d), wmap)],
            out_specs=pl.BlockSpec((MOE_ROWS * ROW_TILE, LANES), xmap)),
        out_shape=jax.ShapeDtypeStruct(xs.shape, U32),
        compiler_params=_params("arbitrary"),
    )(*items, xs, w1, w3, w2)


def _work_items(counts, n_rows):
    n_blocks = n_rows // MOE_ROWS
    n_items = n_blocks + N_EXPERTS - 1
    ends = jnp.cumsum(counts)
    starts = ends - counts
    first_blk = starts // MOE_ROWS
    n_blk = jnp.where(counts > 0, (ends - 1) // MOE_ROWS - first_blk + 1, 0)
    item_end = jnp.cumsum(n_blk)
    total = item_end[-1]
    w = jnp.arange(n_items, dtype=I32)
    wc = jnp.minimum(w, total - 1)
    e = jnp.sum(item_end[None, :] <= wc[:, None], axis=1).astype(I32)
    blk = (first_blk[e] + wc - (item_end[e] - n_blk[e])).astype(I32)
    lo = jnp.maximum(starts[e], blk * MOE_ROWS).astype(I32)
    hi = jnp.minimum(ends[e], (blk + 1) * MOE_ROWS).astype(I32)
    live = w < total
    return blk, e, jnp.where(live, lo, 0), jnp.where(live, hi, 0), starts


def _combine_kernel(h_ref, z0_ref, z1_ref, rg_ref, g_ref, *rest):
    y_ref = rest[-1]
    rg = rg_ref[...]
    moe = (_unpack_rows(_load_tile_rows(z0_ref)) * rg[:, 0:1]
           + _unpack_rows(_load_tile_rows(z1_ref)) * rg[:, 1:2])
    y_ref[...] = _rms(h_ref[...] + moe, g_ref[...])


def _combine(h2, z, rg, g_final, tm, y_prev, row0, n_total):
    n, d = h2.shape
    n_t = n // tm
    row = lambda i: (i, 0)
    in_specs = [pl.BlockSpec((tm, d), row), pl.BlockSpec((tm * ROW_TILE, LANES), row),
                pl.BlockSpec((tm * ROW_TILE, LANES), lambda i: (n_t + i, 0)), pl.BlockSpec((tm, 8), row),
                pl.BlockSpec((1, d), lambda i: (0, 0))]
    args = [h2, z, z, rg, g_final.reshape(1, d)]
    aliases = {}
    if y_prev is not None:
        in_specs.append(pl.BlockSpec(memory_space=pl.ANY))
        args.append(y_prev)
        aliases = {len(args) - 1: 0}
    return pl.pallas_call(
        _combine_kernel, grid=(n_t,),
        in_specs=in_specs,
        out_specs=pl.BlockSpec((tm, d), lambda i: (row0 // tm + i, 0)),
        out_shape=jax.ShapeDtypeStruct((n_total, d), F32),
        input_output_aliases=aliases,
        compiler_params=_params("arbitrary"),
    )(*args)


def _moe_and_final(h2, xn, ri, rg, counts, w1, w3, w2, g_final, *, tm, y_prev, row0, n_total):
    n = h2.shape[0]
    blk, e, lo, hi, starts = _work_items(counts, 2 * n)
    expert = jnp.arange(N_EXPERTS, dtype=I32)[:, None, None]
    first = jnp.sum(jnp.where(ri[None, 0:2] == expert, starts.astype(I32)[:, None, None], 0), axis=0)
    pos = (first + ri[2:4]).astype(I32)
    sub = (pos[:, :, None] * ROW_TILE + jnp.arange(ROW_TILE, dtype=I32)).reshape(2 * n * ROW_TILE)
    xs = _sc_scatter_rows(xn, sub, 2 * n * ROW_TILE)
    ys = _experts((blk, e, lo, hi), xs, w1, w3, w2)
    z = _sc_gather_rows(ys, sub)
    return _combine(h2, z, rg.T, g_final, tm, y_prev, row0, n_total)


def _pad_time(a, s_pad):
    return jnp.pad(a, ((0, 0), (0, s_pad - a.shape[1])) + ((0, 0),) * (a.ndim - 2))


def _layer(x, past, mem_k, mem_v, wts, lam_init, g_final, *, tm, tq, tk, moe_parts):
    b, t, d = x.shape
    n = b * t
    p_len = 0 if past is None else past[0].shape[1]
    s_valid = p_len + t
    s_pad = -(-s_valid // tk) * tk
    pos = p_len + jnp.arange(t)

    fk, fv, logf, dk, dv, qf, kf, vf, qd, kd, vd = _inproj(
        x, wts['g_mix'], wts['w_main'], wts['w_ff'], wts['b_forget'], pos, tm)
    time_major = tm <= t

    def leaf(a, dims):
        if time_major:
            return jnp.moveaxis(a.reshape((b,) + dims + (t,)), -1, 1)
        return a.reshape((b, t) + dims)

    new = (leaf(fk, (FOX_HEADS, HEAD_DIM)), leaf(fv, (FOX_HEADS, HEAD_DIM)), leaf(logf, (FOX_HEADS,)),
           leaf(dk, (DIFF_HEADS, 2, HEAD_DIM)), dv.reshape(b, t, DIFF_HEADS, 2 * HEAD_DIM))

    def keys_t(cur, old_rows):
        if not time_major:
            cur = jnp.swapaxes(cur.reshape(b, t, FOX_W), 1, 2)
        if old_rows is not None:
            prev = jnp.swapaxes(old_rows.reshape(b, p_len, FOX_W).astype(BF16), 1, 2)
            cur = jnp.concatenate([prev, cur], axis=2)
        return jnp.pad(cur, ((0, 0), (0, 0), (0, s_pad - s_valid)))

    def with_past(new_b16, old_rows):
        cur = new_b16.reshape(b, t, FOX_W)
        if old_rows is not None:
            cur = jnp.concatenate([old_rows.reshape(b, p_len, FOX_W).astype(BF16), cur], axis=1)
        return _pad_time(cur, s_pad)

    logf_t = logf if time_major else jnp.swapaxes(logf.reshape(b, t, FOX_HEADS), 1, 2)
    if past is not None:
        logf_t = jnp.concatenate([jnp.swapaxes(past[2].astype(F32), 1, 2), logf_t], axis=2)
    k_bias, q_bias_t = _fox_bias(jnp.pad(logf_t, ((0, 0), (0, 0), (0, s_pad - s_valid))))
    q_bias = jnp.swapaxes(q_bias_t[:, :, p_len:p_len + t], 1, 2).reshape(n, LANES)

    old = (None,) * 5 if past is None else past
    geom = dict(b=b, t=t, s_pad=s_pad, s_valid=s_valid, past=p_len, tq=tq, tk=tk, lam_init=lam_init)
    fox_o = _self_attention(qf, keys_t(kf, old[0]), with_past(vf, old[1]), (q_bias, k_bias),
                            fox=True, **geom)
    diff_o = _self_attention(qd, keys_t(kd, old[3]), with_past(vd, old[4]),
                             (wts['lam_vecs'], wts['g_diff']), fox=False, **geom)
    h1, qm = _outproj(x.reshape(n, d), fox_o, diff_o, wts['wo_f'], wts['wo_d'], wts['g_mem_q'],
                      wts['w_mq'], tm)
    assert b % moe_parts == 0
    bp = b // moe_parts
    y = None
    for part in range(moe_parts):
        h2, xn, logits = _memattn(qm, mem_k, mem_v, h1, wts['w_mo'], wts['g_ffn'], wts['w_router'],
                                  wts['b_router'], b0=part * bp, b=bp, t=t, tq=tq)
        ri, rg, cnt = _router(logits.T, min(tm, bp * t))
        counts = cnt[:N_EXPERTS, 0].astype(I32)
        y = _moe_and_final(h2, xn, ri, rg, counts, wts['w1'], wts['w3'], wts['w2'], g_final, tm=tm,
                           y_prev=y, row0=part * bp * t, n_total=n)
    return y.reshape(b, t, d), new


def kernel(x_prompt, x_sample, cache_fox_k, cache_fox_v, cache_fox_logf, cache_diff_k, cache_diff_v, cache_mem_k, cache_mem_v, mem_prompt, g_mix, w_in, b_forget, lam_q1, lam_k1, lam_q2, lam_k2, g_diff, w_o, g_mem_q, g_mem_kv, w_mq, w_mk, w_mv, w_mo, g_ffn, w_group, b_group, w_erouter, b_erouter, w1, w3, w2, g_final):
    depth = w_in.shape[0]
    assert depth == 1
    l = 0
    lam_init = 0.8 - 0.6 * math.exp(-0.3 * l)
    d = D_MODEL
    w = w_in[l]
    ff0 = 3 * FOX_W
    w_router = jnp.zeros((d, LANES), F32).at[:, :N_EXPERTS].set(w_erouter[l])
    w_router = w_router.at[:, GROUP_LANE0:GROUP_LANE0 + N_GROUPS].set(w_group[l])
    b_router = jnp.zeros((1, LANES), F32).at[0, :N_EXPERTS].set(b_erouter[l])
    b_router = b_router.at[0, GROUP_LANE0:GROUP_LANE0 + N_GROUPS].set(b_group[l])
    wts = {
        'g_mix': g_mix[l],
        'w_main': jnp.concatenate([w[:, :ff0], w[:, ff0 + FOX_HEADS:]], axis=1).astype(BF16),
        'w_ff': jnp.pad(w[:, ff0:ff0 + FOX_HEADS], ((0, 0), (0, LANES - FOX_HEADS))).astype(BF16),
        'b_forget': b_forget[l],
        'lam_vecs': jnp.stack([lam_q1[l], lam_k1[l], lam_q2[l], lam_k2[l]]),
        'g_diff': g_diff[l],
        'wo_f': w_o[l][:FOX_W].astype(BF16), 'wo_d': w_o[l][FOX_W:].astype(BF16),
        'g_mem_q': g_mem_q[l], 'w_mq': w_mq[l].astype(BF16), 'w_mo': w_mo[l].astype(BF16),
        'g_ffn': g_ffn[l], 'w_router': w_router.astype(BF16), 'b_router': b_router,
        'w1': w1[l].astype(BF16), 'w3': w3[l].astype(BF16), 'w2': w2[l].astype(BF16),
    }

    bp, tp, _ = x_prompt.shape
    mk, mv, mk_b, mv_b = _memkv(mem_prompt.reshape(bp * MEM_LEN, d), g_mem_kv[l],
                                w_mk[l].astype(BF16), w_mv[l].astype(BF16), 512)
    yp, new_p = _layer(x_prompt, None, mk_b.reshape(bp, MEM_LEN, d), mv_b.reshape(bp, MEM_LEN, d),
                       wts, lam_init, g_final, tm=512, tq=512, tk=512, moe_parts=2)

    bs, ts, _ = x_sample.shape
    past = (cache_fox_k[l], cache_fox_v[l], cache_fox_logf[l], cache_diff_k[l], cache_diff_v[l])
    ys, new_s = _layer(x_sample, past, cache_mem_k[l].reshape(bs, MEM_LEN, d).astype(BF16),
                       cache_mem_v[l].reshape(bs, MEM_LEN, d).astype(BF16),
                       wts, lam_init, g_final, tm=bs * ts, tq=ts, tk=512, moe_parts=1)

    mem_shape = (1, bp, MEM_LEN, MEM_HEADS, MEM_HEAD_DIM)
    return (yp, ys) + tuple(a[None] for a in new_p) + (mk.reshape(mem_shape), mv.reshape(mem_shape)) \
        + tuple(a[None] for a in new_s)
```

```python
import functools
import math

import jax
import jax.numpy as jnp
from jax import lax
from jax.experimental import pallas as pl
from jax.experimental.pallas import tpu as pltpu
from jax.experimental.pallas import tpu_sc as plsc

F32 = jnp.float32
BF16 = jnp.bfloat16
I32 = jnp.int32

D_MODEL = 1024
HEAD_DIM = 64
FOX_HEADS = 8
DIFF_HEADS = 4
FOX_W = FOX_HEADS * HEAD_DIM
DIFF_W = DIFF_HEADS * 2 * HEAD_DIM
ROT_DIM = HEAD_DIM // 4
ROPE_THETA = 500000.0
CHUNK = 64
MEM_LEN = 256
MEM_HEADS = 4
MEM_HEAD_DIM = D_MODEL // MEM_HEADS
N_GROUPS = 4
EXPERTS_PER_GROUP = 8
N_EXPERTS = N_GROUPS * EXPERTS_PER_GROUP
D_EXPERT = 512
EPS = 1e-6

LANES = 128
LOG2E = 1.4426950408889634
NEG = -1e30
VMEM_LIMIT_BYTES = 56 * 1024 * 1024
MOE_ROWS = 256
GROUP_LANE0 = N_EXPERTS


def _params(*sem):
    return pltpu.CompilerParams(dimension_semantics=sem, vmem_limit_bytes=VMEM_LIMIT_BYTES)


def _rms(x, g):
    return (x * lax.rsqrt(jnp.mean(x * x, axis=-1, keepdims=True) + EPS)) * g


def _rope(x, cos, sin_lo, sin_hi):
    outs = []
    for c in range(x.shape[1] // LANES):
        xc = x[:, c * LANES:(c + 1) * LANES]
        up = pltpu.roll(xc, LANES - ROT_DIM // 2, axis=1)
        dn = pltpu.roll(xc, ROT_DIM // 2, axis=1)
        outs.append(xc * cos + up * sin_lo + dn * sin_hi)
    return jnp.concatenate(outs, axis=1)


def _inproj_kernel(x_ref, g_ref, w_ref, wf_ref, bf_ref, cos_ref, slo_ref, shi_ref,
                   fk_ref, fv_ref, logf_ref, dk_ref, dv_ref,
                   qf_ref, kf_ref, vf_ref, qd_ref, kd_ref, vd_ref, *, q_scale, time_major):
    n = _rms(x_ref[...], g_ref[...]).astype(BF16)
    turn = (lambda a: a.T) if time_major else (lambda a: a)

    def proj(c):
        return jnp.dot(n, w_ref[:, c * FOX_W:(c + 1) * FOX_W], preferred_element_type=F32)

    cos, slo, shi = cos_ref[...], slo_ref[...], shi_ref[...]
    qf_ref[...] = (proj(0) * q_scale).astype(BF16)
    fk = turn(proj(1))
    fk_ref[...] = fk
    kf_ref[...] = fk.astype(BF16)
    fv = proj(2)
    fv_ref[...] = turn(fv)
    vf_ref[...] = fv.astype(BF16)
    qd_ref[...] = (_rope(proj(3), cos, slo, shi) * q_scale).astype(BF16)
    dk = turn(_rope(proj(4), cos, slo, shi))
    dk_ref[...] = dk
    kd_ref[...] = dk.astype(BF16)
    dv = proj(5)
    for h in range(DIFF_HEADS):
        dv_ref[pl.ds(h, dv.shape[0], stride=DIFF_HEADS), :] = dv[:, h * LANES:(h + 1) * LANES]
    vd_ref[...] = dv.astype(BF16)
    z = jnp.dot(n, wf_ref[...], preferred_element_type=F32) + bf_ref[...]
    logf = jnp.minimum(z, 0.0) - jnp.log1p(jnp.exp(-jnp.abs(z)))
    logf_ref[...] = logf.T[:FOX_HEADS] if time_major else logf[:, :FOX_HEADS]


def _rope_tables(pos):
    half = ROT_DIM // 2
    inv = jnp.power(ROPE_THETA, -jnp.arange(half, dtype=F32) / half)
    ang = pos.astype(F32)[:, None] * inv[None, :]
    cos, sin = jnp.cos(ang), jnp.sin(ang)
    t = pos.shape[0]
    pad = jnp.zeros((t, HEAD_DIM - ROT_DIM), F32)
    zero = jnp.zeros((t, half), F32)
    cos64 = jnp.concatenate([cos, cos, pad + 1.0], axis=1)
    slo64 = jnp.concatenate([-sin, zero, pad], axis=1)
    shi64 = jnp.concatenate([zero, sin, pad], axis=1)
    return tuple(jnp.tile(a, (1, LANES // HEAD_DIM)) for a in (cos64, slo64, shi64))


def _inproj(x, g, w_main, w_ff, b_ff, pos, tm):
    b, t, d = x.shape
    n = b * t
    tables = _rope_tables(pos)
    time_major = tm <= t
    if time_major:
        assert t % tm == 0
        per = t // tm
        tab_map = lambda i: (i % per, 0)
        turned = pl.BlockSpec((None, FOX_W, tm), lambda i: (i // per, 0, i % per))
        gate = pl.BlockSpec((None, FOX_HEADS, tm), lambda i: (i // per, 0, i % per))
        turned_shape, gate_shape = (b, FOX_W, t), (b, FOX_HEADS, t)
    else:
        assert tm % t == 0
        tables = tuple(jnp.tile(a, (tm // t, 1)) for a in tables)
        tab_map = lambda i: (0, 0)
    assert n % tm == 0
    row = lambda i: (i, 0)
    const = lambda i: (0, 0)
    wide = pl.BlockSpec((tm, FOX_W), row)
    if not time_major:
        turned, gate = wide, pl.BlockSpec((tm, FOX_HEADS), row)
        turned_shape, gate_shape = (n, FOX_W), (n, FOX_HEADS)
    tab = pl.BlockSpec((tm, LANES), tab_map)
    sds = jax.ShapeDtypeStruct
    return pl.pallas_call(
        functools.partial(_inproj_kernel, q_scale=HEAD_DIM ** -0.5 * LOG2E, time_major=time_major),
        grid=(n // tm,),
        in_specs=[pl.BlockSpec((tm, d), row), pl.BlockSpec((1, d), const),
                  pl.BlockSpec(w_main.shape, const), pl.BlockSpec(w_ff.shape, const),
                  pl.BlockSpec((1, LANES), const), tab, tab, tab],
        out_specs=[turned, turned, gate, turned, pl.BlockSpec((tm * DIFF_HEADS, LANES), row),
                   wide, turned, wide, wide, turned, wide],
        out_shape=[sds(turned_shape, F32), sds(turned_shape, F32), sds(gate_shape, F32),
                   sds(turned_shape, F32), sds((n * DIFF_HEADS, LANES), F32),
                   sds((n, FOX_W), BF16), sds(turned_shape, BF16), sds((n, FOX_W), BF16),
                   sds((n, FOX_W), BF16), sds(turned_shape, BF16), sds((n, FOX_W), BF16)],
        compiler_params=_params("arbitrary"),
    )(x.reshape(n, d), g.reshape(1, d), w_main, w_ff,
      jnp.pad(b_ff.reshape(1, FOX_HEADS), ((0, 0), (0, LANES - FOX_HEADS))), *tables)


AUX_LANES = 8


def _bf16_part(x):
    return pltpu.bitcast(pltpu.bitcast(x, jnp.uint32) & jnp.uint32(0xFFFF0000), F32)


def _cumsum_kernel(x_ref, kb_ref, qb_ref):
    x = x_ref[...]
    s_len = x.shape[1]
    lane = lax.broadcasted_iota(I32, x.shape, 1)
    shift = 1
    while shift < s_len:
        x = x + jnp.where(lane >= shift, pltpu.roll(x, shift, axis=1), 0.0)
        shift *= 2
    c = x * LOG2E
    hi = _bf16_part(c)
    mid = _bf16_part(c - hi)
    lo = _bf16_part(c - hi - mid)
    slot = lax.broadcasted_iota(I32, (LANES, x.shape[0]), 0)
    head = lax.broadcasted_iota(I32, (LANES, x.shape[0]), 1)

    def place(first_slot):
        out = jnp.zeros((LANES, s_len), F32)
        for i, part in enumerate((hi, mid, lo)):
            sel = jnp.where(slot == head * AUX_LANES + first_slot + i, 1.0, 0.0).astype(BF16)
            out = out + jnp.dot(sel, part.astype(BF16), preferred_element_type=F32)
        return out

    slot_col = lax.broadcasted_iota(I32, (LANES, s_len), 0)
    used = slot_col < FOX_HEADS * AUX_LANES
    within = slot_col % AUX_LANES
    kb_ref[...] = (jnp.where(used & (within < 3), 1.0, 0.0) - place(3)).astype(BF16)
    qb_ref[...] = (jnp.where(used & (within >= 3) & (within < 6), 1.0, 0.0) + place(0)).astype(BF16)


def _fox_bias(logf_t):
    b, h, s = logf_t.shape
    x = jnp.pad(logf_t, ((0, 0), (0, 16 - h), (0, 0)))
    out = pl.BlockSpec((None, LANES, s), lambda i: (i, 0, 0))
    return pl.pallas_call(
        _cumsum_kernel, grid=(b,), in_specs=[pl.BlockSpec((None, 16, s), lambda i: (i, 0, 0))],
        out_specs=[out, out],
        out_shape=[jax.ShapeDtypeStruct((b, LANES, s), BF16)] * 2,
        compiler_params=_params("arbitrary"),
    )(x)


def _attn_kernel(*refs, tq, tk, past, s_valid, fox, lam_init):
    if fox:
        q_ref, qa_ref, k_ref, ka_ref, v_ref, o_ref, sa_ref, sb_ref, mx_ref, m_ref, l_ref, acc_ref = refs
    else:
        q_ref, k_ref, v_ref, lam_ref, g_ref, o_ref, sa_ref, sb_ref, mx_ref, m_ref, l_ref, acc_ref = refs
    i = pl.program_id(2)
    q = q_ref[...]
    lane = lax.broadcasted_iota(I32, (tq, LANES), 1)
    low = lane < HEAD_DIM
    zero = jnp.zeros_like(q)
    q_part = [jnp.where(low, q, zero), jnp.where(low, zero, q)]
    if fox:
        qa = qa_ref[...]
        for h in range(2):
            lane0 = (2 * pl.program_id(1) + h) * AUX_LANES
            aux = jnp.where((lane >= lane0) & (lane < lane0 + AUX_LANES), qa, zero)
            q_part[h] = jnp.concatenate([q_part[h], aux], axis=1)
    m_ref[...] = jnp.full(m_ref.shape, NEG, F32)
    l_ref[...] = jnp.zeros(l_ref.shape, F32)
    acc_ref[...] = jnp.zeros(acc_ref.shape, F32)
    q0 = past + i * tq
    nb = q0 // tk
    bufs = (sa_ref, sb_ref)

    def scores(j, buf):
        ks = pl.multiple_of(j * tk, tk)
        k = k_ref[:, pl.ds(ks, tk)]
        if fox:
            k = jnp.concatenate([k, ka_ref[:, pl.ds(ks, tk)]], axis=0)
        for h in range(2):
            s = jnp.dot(q_part[h], k, preferred_element_type=F32)
            bufs[buf][h] = s
            mx_ref[buf, h] = jnp.broadcast_to(jnp.max(s, axis=1, keepdims=True), (tq, LANES))

    def consume(j, buf, use_mask):
        ks = pl.multiple_of(j * tk, tk)
        v = jnp.concatenate([v_ref[pl.ds(ks, tk), :], jnp.ones((tk, LANES), BF16)], axis=1)
        pv, alpha = [], []
        for h in range(2):
            t = bufs[buf][h]
            if use_mask:
                qpos = q0 + lax.broadcasted_iota(I32, (tq, tk), 0)
                kpos = j * tk + lax.broadcasted_iota(I32, (tq, tk), 1)
                if fox:
                    seen = kpos <= qpos
                else:
                    seen = kpos < jnp.minimum((qpos // CHUNK + 1) * CHUNK, s_valid)
                t = jnp.where(seen, t, NEG)
                mx = jnp.max(t, axis=1, keepdims=True)
            else:
                mx = mx_ref[buf, h]
            m_prev = m_ref[h]
            m_new = jnp.maximum(m_prev, mx)
            a = jnp.exp2(m_prev - m_new)
            p = jnp.exp2(t - jnp.concatenate([m_new] * (tk // LANES), axis=1))
            m_ref[h] = m_new
            both = jnp.dot(p.astype(BF16), v, preferred_element_type=F32)
            pv.append(both[:, :LANES])
            l_ref[h] = a * l_ref[h] + both[:, LANES:]
            alpha.append(a)
        if fox:
            acc_ref[0] = acc_ref[0] * jnp.where(low, alpha[0], alpha[1]) + jnp.where(low, pv[0], pv[1])
        else:
            for h in range(2):
                acc_ref[h] = acc_ref[h] * alpha[h] + pv[h]

    scores(0, 0)

    def pair(jj, c):
        j = 2 * jj
        scores(j + 1, 1)
        consume(j, 0, False)
        scores(j + 2, 0)
        consume(j + 1, 1, False)
        return c

    lax.fori_loop(0, nb // 2, pair, 0)

    @pl.when(nb % 2 == 1)
    def _():
        scores(nb, 1)
        consume(nb - 1, 0, False)
        consume(nb, 1, True)

    @pl.when(nb % 2 == 0)
    def _():
        consume(nb, 0, True)

    if fox:
        o = acc_ref[0] / jnp.where(low, l_ref[0], l_ref[1])
    else:
        lam_v = lam_ref[...]
        lam = (jnp.exp(jnp.sum(lam_v[0:1] * lam_v[1:2], axis=1, keepdims=True))
               - jnp.exp(jnp.sum(lam_v[2:3] * lam_v[3:4], axis=1, keepdims=True)) + lam_init)
        o = acc_ref[0] / l_ref[0] - lam * (acc_ref[1] / l_ref[1])
        o = _rms(o, g_ref[...]) * (1.0 - lam_init)
    o_ref[...] = o.astype(o_ref.dtype)


def _self_attention(q, k, v, extra, *, b, t, s_pad, s_valid, past, tq, tk, fox, lam_init):
    assert tk % tq == 0 and past % tq == 0 and t % tq == 0 and tk % CHUNK == 0 and s_pad % tk == 0
    assert tq % CHUNK == 0 or (CHUNK % tq == 0 and past % CHUNK == 0)
    n_q = t // tq
    k_spec = pl.BlockSpec((None, LANES, s_pad), lambda bi, p, i: (bi, p, 0))
    v_spec = pl.BlockSpec((None, s_pad, LANES), lambda bi, p, i: (bi, 0, p))
    q_spec = pl.BlockSpec((tq, LANES), lambda bi, p, i: (bi * n_q + i, p))
    const = lambda bi, p, i: (0, 0)
    if fox:
        in_specs = [q_spec, pl.BlockSpec((tq, LANES), lambda bi, p, i: (bi * n_q + i, 0)), k_spec,
                    pl.BlockSpec((None, LANES, s_pad), lambda bi, p, i: (bi, 0, 0)), v_spec]
        args = (q, extra[0], k, extra[1], v)
    else:
        in_specs = [q_spec, k_spec, v_spec, pl.BlockSpec((4, HEAD_DIM), const), pl.BlockSpec((1, LANES), const)]
        args = (q, k, v, extra[0], extra[1].reshape(1, LANES))
    stat = pltpu.VMEM((2, tq, LANES), F32)
    return pl.pallas_call(
        functools.partial(_attn_kernel, tq=tq, tk=tk, past=past, s_valid=s_valid, fox=fox, lam_init=lam_init),
        grid=(b, FOX_W // LANES, n_q),
        in_specs=in_specs,
        out_specs=q_spec,
        out_shape=jax.ShapeDtypeStruct((b * t, FOX_W), BF16),
        scratch_shapes=[pltpu.VMEM((2, tq, tk), F32), pltpu.VMEM((2, tq, tk), F32),
                        pltpu.VMEM((2, 2, tq, LANES), F32), stat, stat, stat],
        compiler_params=_params("arbitrary", "arbitrary", "arbitrary"),
    )(*args)


def _outproj_kernel(x_ref, fo_ref, do_ref, wof_ref, wod_ref, g_ref, wq_ref, h_ref, q_ref, *, q_scale):
    h = (x_ref[...] + jnp.dot(fo_ref[...], wof_ref[...], preferred_element_type=F32)
         + jnp.dot(do_ref[...], wod_ref[...], preferred_element_type=F32))
    h_ref[...] = h
    n = _rms(h, g_ref[...]).astype(BF16)
    q_ref[...] = (jnp.dot(n, wq_ref[...], preferred_element_type=F32) * q_scale).astype(BF16)


def _outproj(x2, fox_o, diff_o, wo_f, wo_d, g_mem_q, w_mq, tm):
    n, d = x2.shape
    row = lambda i: (i, 0)
    const = lambda i: (0, 0)
    return pl.pallas_call(
        functools.partial(_outproj_kernel, q_scale=MEM_HEAD_DIM ** -0.5 * LOG2E),
        grid=(n // tm,),
        in_specs=[pl.BlockSpec((tm, d), row), pl.BlockSpec((tm, FOX_W), row),
                  pl.BlockSpec((tm, DIFF_W), row), pl.BlockSpec((FOX_W, d), const),
                  pl.BlockSpec((DIFF_W, d), const), pl.BlockSpec((1, d), const),
                  pl.BlockSpec((d, d), const)],
        out_specs=[pl.BlockSpec((tm, d), row), pl.BlockSpec((tm, d), row)],
        out_shape=[jax.ShapeDtypeStruct((n, d), F32), jax.ShapeDtypeStruct((n, d), BF16)],
        compiler_params=_params("arbitrary"),
    )(x2, fox_o, diff_o, wo_f, wo_d, g_mem_q.reshape(1, d), w_mq)


def _memkv_kernel(m_ref, g_ref, wk_ref, wv_ref, k_ref, v_ref, kb_ref, vb_ref):
    n = _rms(m_ref[...], g_ref[...]).astype(BF16)
    k = jnp.dot(n, wk_ref[...], preferred_element_type=F32)
    v = jnp.dot(n, wv_ref[...], preferred_element_type=F32)
    k_ref[...] = k
    v_ref[...] = v
    kb_ref[...] = k.astype(BF16)
    vb_ref[...] = v.astype(BF16)


def _memkv(mem2, g, wk, wv, tm):
    n, d = mem2.shape
    row = lambda i: (i, 0)
    const = lambda i: (0, 0)
    blk = pl.BlockSpec((tm, d), row)
    wspec = pl.BlockSpec((d, d), const)
    return pl.pallas_call(
        _memkv_kernel, grid=(n // tm,),
        in_specs=[blk, pl.BlockSpec((1, d), const), wspec, wspec],
        out_specs=[blk, blk, blk, blk],
        out_shape=[jax.ShapeDtypeStruct((n, d), F32)] * 2 + [jax.ShapeDtypeStruct((n, d), BF16)] * 2,
        compiler_params=_params("arbitrary"),
    )(mem2, g.reshape(1, d), wk, wv)


U32 = jnp.uint32
HALF_D = D_MODEL // 2
ROW_TILE = HALF_D // LANES
BF16_BITS = 0xFFFF0000


def _pack_rows(val):
    def rounded(x):
        bits = pltpu.bitcast(x, U32)
        return (bits + U32(0x7FFF) + ((bits >> 16) & U32(1))) & U32(BF16_BITS)
    return rounded(val[:, :HALF_D]) | (rounded(val[:, HALF_D:]) >> 16)


def _unpack_rows(words):
    return jnp.concatenate([pltpu.bitcast(words & U32(BF16_BITS), F32), pltpu.bitcast(words << 16, F32)], axis=1)


def _load_tile_rows(ref):
    rows = ref.shape[0] // ROW_TILE
    return jnp.concatenate([ref[pl.ds(s, rows, stride=ROW_TILE), :] for s in range(ROW_TILE)], axis=1)


def _store_tile_rows(ref, words):
    rows = words.shape[0]
    for s in range(ROW_TILE):
        ref[pl.ds(s, rows, stride=ROW_TILE), :] = words[:, s * LANES:(s + 1) * LANES]


def _memattn_kernel(q_ref, k_ref, v_ref, h_ref, wo_ref, g_ref, wr_ref, br_ref,
                    h2_ref, xn_ref, logit_ref):
    q = q_ref[...]
    outs = []
    for h in range(MEM_HEADS):
        sl = slice(h * MEM_HEAD_DIM, (h + 1) * MEM_HEAD_DIM)
        s = lax.dot_general(q[:, sl], k_ref[:, sl], (((1,), (1,)), ((), ())), preferred_element_type=F32)
        p = jnp.exp2(s - jnp.max(s, axis=1, keepdims=True))
        o = jnp.dot(p.astype(BF16), v_ref[:, sl], preferred_element_type=F32)
        outs.append((o / jnp.sum(p, axis=1, keepdims=True)).astype(BF16))
    h2 = h_ref[...] + jnp.dot(jnp.concatenate(outs, axis=1), wo_ref[...], preferred_element_type=F32)
    h2_ref[...] = h2
    xn_f32 = _rms(h2, g_ref[...])
    words = _pack_rows(xn_f32)
    for s in range(ROW_TILE):
        xn_ref[s] = words[:, s * LANES:(s + 1) * LANES]
    logit_ref[...] = jnp.dot(xn_f32.astype(BF16), wr_ref[...], preferred_element_type=F32) + br_ref[...]


def _memattn(qm, mk, mv, h1, w_mo, g_ffn, w_router, b_router, *, b0, b, t, tq):
    d = h1.shape[1]
    n = b * t
    n_q = t // tq
    row = lambda bi, i: (bi * n_q + i, 0)
    src_row = lambda bi, i: ((b0 + bi) * n_q + i, 0)
    const = lambda bi, i: (0, 0)
    mem = pl.BlockSpec((None, MEM_LEN, d), lambda bi, i: (b0 + bi, 0, 0))
    src = pl.BlockSpec((tq, d), src_row)
    blk = pl.BlockSpec((tq, d), row)
    planes = pl.BlockSpec((ROW_TILE, tq, LANES), lambda bi, i: (0, bi * n_q + i, 0))
    return pl.pallas_call(
        _memattn_kernel,
        grid=(b, n_q),
        in_specs=[src, mem, mem, src, pl.BlockSpec((d, d), const), pl.BlockSpec((1, d), const),
                  pl.BlockSpec((d, LANES), const), pl.BlockSpec((1, LANES), const)],
        out_specs=[blk, planes, pl.BlockSpec((tq, LANES), row)],
        out_shape=[jax.ShapeDtypeStruct((n, d), F32), jax.ShapeDtypeStruct((ROW_TILE, n, LANES), U32),
                   jax.ShapeDtypeStruct((n, LANES), F32)],
        compiler_params=_params("arbitrary", "arbitrary"),
    )(qm, mk, mv, h1, w_mo, g_ffn.reshape(1, d), w_router, b_router)


ROUTER_ROWS = 48


def _first_slot(hit, slot_f32):
    return jnp.min(jnp.where(hit, slot_f32, float(LANES)), axis=0, keepdims=True).astype(I32)


def _router_kernel(lt_ref, ri_ref, rg_ref, cnt_ref, carry_ref, *, rt):
    @pl.when(pl.program_id(0) == 0)
    def _():
        carry_ref[...] = jnp.zeros(carry_ref.shape, F32)

    lt = lt_ref[0:ROUTER_ROWS, :]
    slot = lax.broadcasted_iota(I32, (ROUTER_ROWS, rt), 0)
    slot_f = slot.astype(F32)
    g_log = jnp.where((slot >= GROUP_LANE0) & (slot < GROUP_LANE0 + N_GROUPS), lt, NEG)
    g_max = jnp.max(g_log, axis=0, keepdims=True)
    g_idx = _first_slot(g_log == g_max, slot_f) - GROUP_LANE0
    g_p = 1.0 / jnp.sum(jnp.exp(g_log - g_max), axis=0, keepdims=True)
    in_group = (slot >= g_idx * EXPERTS_PER_GROUP) & (slot < (g_idx + 1) * EXPERTS_PER_GROUP)
    e_log = jnp.where(in_group, lt, NEG)
    e_exp = jnp.exp(e_log - jnp.max(e_log, axis=0, keepdims=True))
    prob = jnp.where(in_group, e_exp / jnp.sum(e_exp, axis=0, keepdims=True), -1.0)
    p1 = jnp.max(prob, axis=0, keepdims=True)
    i1 = _first_slot(prob == p1, slot_f)
    rest = jnp.where(slot == i1, -1.0, prob)
    p2 = jnp.max(rest, axis=0, keepdims=True)
    i2 = _first_slot(rest == p2, slot_f)
    top_sum = p1 + p2

    hot1, hot2 = slot == i1, slot == i2
    onehot = jnp.where(hot1 | hot2, 1.0, 0.0)
    earlier = lax.broadcasted_iota(I32, (rt, rt), 0) < lax.broadcasted_iota(I32, (rt, rt), 1)
    before = jnp.dot(onehot.astype(BF16), jnp.where(earlier, 1.0, 0.0).astype(BF16),
                     preferred_element_type=F32) + carry_ref[0:ROUTER_ROWS, 0:1]
    r1 = jnp.sum(jnp.where(hot1, before, 0.0), axis=0, keepdims=True)
    r2 = jnp.sum(jnp.where(hot2, before, 0.0), axis=0, keepdims=True)
    carry_ref[0:ROUTER_ROWS, :] = carry_ref[0:ROUTER_ROWS, :] + jnp.sum(onehot, axis=1, keepdims=True)
    cnt_ref[...] = carry_ref[...]

    ri_ref[...] = jnp.concatenate([i1, i2, r1.astype(I32), r2.astype(I32), jnp.zeros((4, rt), I32)], axis=0)
    rg_ref[...] = jnp.concatenate([g_p * p1 / top_sum, g_p * p2 / top_sum, jnp.zeros((6, rt), F32)], axis=0)


def _router(logits_t, rt):
    n = logits_t.shape[1]
    col = lambda i: (0, i)
    return pl.pallas_call(
        functools.partial(_router_kernel, rt=rt),
        grid=(n // rt,),
        in_specs=[pl.BlockSpec((LANES, rt), col)],
        out_specs=[pl.BlockSpec((8, rt), col), pl.BlockSpec((8, rt), col),
                   pl.BlockSpec((LANES, LANES), lambda i: (0, 0))],
        out_shape=[jax.ShapeDtypeStruct((8, n), I32), jax.ShapeDtypeStruct((8, n), F32),
                   jax.ShapeDtypeStruct((LANES, LANES), F32)],
        scratch_shapes=[pltpu.VMEM((LANES, LANES), F32)],
        compiler_params=_params("arbitrary"),
    )(logits_t)


SC_CORES, SC_SUBCORES = 2, 16
SC_WINDOW = 128


def _sc_mesh():
    return plsc.VectorSubcoreMesh(core_axis_name="core", subcore_axis_name="subcore")


def _sc_split(n_windows):
    if n_windows % (SC_CORES * SC_SUBCORES) == 0:
        return ("core", "subcore")
    assert n_windows % SC_SUBCORES == 0
    return "subcore"


def _sc_repeat(idx):
    n_windows = idx.shape[0] // SC_WINDOW
    reps = SC_SUBCORES // math.gcd(n_windows, SC_SUBCORES)
    return jnp.tile(idx, reps)


def _sc_scatter_rows(src, idx, n_out):
    rows = src.shape[0]
    win = SC_WINDOW
    n_src_blocks = rows // win
    idx = _sc_repeat(idx)
    n_idx = idx.shape[0]

    @functools.partial(pl.kernel, out_type=jax.ShapeDtypeStruct((n_out, LANES), src.dtype),
                       mesh=_sc_mesh(), scratch_types=[])
    def scatter(x_hbm, i_hbm, o_hbm):
        def body(x_vmem, i_vmem):
            pltpu.sync_copy(x_vmem, o_hbm.at[i_vmem.at[0]])

        pltpu.emit_pipeline(
            body, grid=(n_idx // win,),
            in_specs=[pl.BlockSpec((win, LANES), lambda i: (i % n_src_blocks, 0)),
                      pl.BlockSpec((1, win), lambda i: (0, i))],
            out_specs=[],
            core_axis_name=_sc_split(n_idx // win), dimension_semantics=(pltpu.PARALLEL,),
        )(x_hbm, i_hbm)

    return scatter(src, idx.reshape(1, n_idx))


def _sc_gather_rows(src, idx):
    n_rows = idx.shape[0]
    idx = _sc_repeat(idx)
    n_out = idx.shape[0]
    win = SC_WINDOW

    @functools.partial(pl.kernel, out_type=jax.ShapeDtypeStruct((n_out, LANES), src.dtype),
                       mesh=_sc_mesh(), scratch_types=[])
    def gather(x_hbm, i_hbm, o_hbm):
        def body(i_vmem, o_vmem):
            pltpu.sync_copy(x_hbm.at[i_vmem.at[0]], o_vmem)

        pltpu.emit_pipeline(
            body, grid=(n_out // win,),
            in_specs=[pl.BlockSpec((1, win), lambda i: (0, i))],
            out_specs=[pl.BlockSpec((win, LANES), lambda i: (i, 0))],
            core_axis_name=_sc_split(n_out // win), dimension_semantics=(pltpu.PARALLEL,),
        )(i_hbm, o_hbm)

    return gather(src, idx.reshape(1, n_out))[:n_rows]


def _expert_kernel(blk_ref, e_ref, lo_ref, hi_ref, x_ref, w1_ref, w3_ref, w2_ref, y_ref):
    w = pl.program_id(0)
    lo, hi = lo_ref[w], hi_ref[w]
    row0 = blk_ref[w] * MOE_ROWS

    def compute():
        x = _unpack_rows(_load_tile_rows(x_ref)).astype(BF16)
        a = jnp.dot(x, w1_ref[...], preferred_element_type=F32)
        g = jnp.dot(x, w3_ref[...], preferred_element_type=F32)
        mid = ((a / (1.0 + jnp.exp(-a))) * g).astype(BF16)
        y = _pack_rows(jnp.dot(mid, w2_ref[...], preferred_element_type=F32))
        rows = row0 + lax.broadcasted_iota(I32, y.shape, 0)
        return y, (rows >= lo) & (rows < hi)

    @pl.when((hi > lo) & (lo == row0))
    def _():
        y, mine = compute()
        _store_tile_rows(y_ref, jnp.where(mine, y, jnp.zeros_like(y)))

    @pl.when((hi > lo) & (lo != row0))
    def _():
        y, mine = compute()
        _store_tile_rows(y_ref, jnp.where(mine, y, _load_tile_rows(y_ref)))


def _experts(items, xs, w1, w3, w2):
    d = D_MODEL
    n_items = items[0].shape[0]
    xmap = lambda w, blk, e, lo, hi: (blk[w], 0)
    wmap = lambda w, blk, e, lo, hi: (e[w], 0, 0)
    return pl.pallas_call(
        _expert_kernel,
        grid_spec=pltpu.PrefetchScalarGridSpec(
            num_scalar_prefetch=4, grid=(n_items,),
            in_specs=[pl.BlockSpec((MOE_ROWS * ROW_TILE, LANES), xmap),
                      pl.BlockSpec((None, d, D_EXPERT), wmap), pl.BlockSpec((None, d, D_EXPERT), wmap),
                      pl.BlockSpec((None, D_EXPERT, d), wmap)],
            out_specs=pl.BlockSpec((MOE_ROWS * ROW_TILE, LANES), xmap)),
        out_shape=jax.ShapeDtypeStruct(xs.shape, U32),
        compiler_params=_params("arbitrary"),
    )(*items, xs, w1, w3, w2)


def _work_items(counts, n_rows):
    n_blocks = n_rows // MOE_ROWS
    n_items = n_blocks + N_EXPERTS - 1
    ends = jnp.cumsum(counts)
    starts = ends - counts
    first_blk = starts // MOE_ROWS
    n_blk = jnp.where(counts > 0, (ends - 1) // MOE_ROWS - first_blk + 1, 0)
    item_end = jnp.cumsum(n_blk)
    total = item_end[-1]
    w = jnp.arange(n_items, dtype=I32)
    wc = jnp.minimum(w, total - 1)
    e = jnp.sum(item_end[None, :] <= wc[:, None], axis=1).astype(I32)
    mine = e[:, None] == jnp.arange(N_EXPERTS, dtype=I32)[None, :]

    def of_expert(per_expert):
        return jnp.sum(jnp.where(mine, per_expert[None, :], 0), axis=1)

    blk = (of_expert(first_blk) + wc - (of_expert(item_end) - of_expert(n_blk))).astype(I32)
    lo = jnp.maximum(of_expert(starts), blk * MOE_ROWS).astype(I32)
    hi = jnp.minimum(of_expert(ends), (blk + 1) * MOE_ROWS).astype(I32)
    live = w < total
    return blk, e, jnp.where(live, lo, 0), jnp.where(live, hi, 0), starts


def _combine_kernel(h_ref, z0_ref, z1_ref, rg_ref, g_ref, *rest):
    y_ref = rest[-1]
    rg = rg_ref[...]
    def rows_of(z_ref):
        return _unpack_rows(jnp.concatenate([z_ref[s] for s in range(ROW_TILE)], axis=1))

    moe = rows_of(z0_ref) * rg[:, 0:1] + rows_of(z1_ref) * rg[:, 1:2]
    y_ref[...] = _rms(h_ref[...] + moe, g_ref[...])


def _combine(h2, z, rg, g_final, tm, y_prev, row0, n_total):
    n, d = h2.shape
    n_t = n // tm
    row = lambda i: (i, 0)
    in_specs = [pl.BlockSpec((tm, d), row), pl.BlockSpec((ROW_TILE, tm, LANES), lambda i: (0, i, 0)),
                pl.BlockSpec((ROW_TILE, tm, LANES), lambda i: (1, i, 0)), pl.BlockSpec((tm, 8), row),
                pl.BlockSpec((1, d), lambda i: (0, 0))]
    args = [h2, z, z, rg, g_final.reshape(1, d)]
    aliases = {}
    if y_prev is not None:
        in_specs.append(pl.BlockSpec(memory_space=pl.ANY))
        args.append(y_prev)
        aliases = {len(args) - 1: 0}
    return pl.pallas_call(
        _combine_kernel, grid=(n_t,),
        in_specs=in_specs,
        out_specs=pl.BlockSpec((tm, d), lambda i: (row0 // tm + i, 0)),
        out_shape=jax.ShapeDtypeStruct((n_total, d), F32),
        input_output_aliases=aliases,
        compiler_params=_params("arbitrary"),
    )(*args)


def _moe_and_final(h2, xn, ri, rg, counts, w1, w3, w2, g_final, *, tm, y_prev, row0, n_total):
    n = h2.shape[0]
    blk, e, lo, hi, starts = _work_items(counts, 2 * n)
    expert = jnp.arange(N_EXPERTS, dtype=I32)[:, None, None]
    first = jnp.sum(jnp.where(ri[None, 0:2] == expert, starts.astype(I32)[:, None, None], 0), axis=0)
    pos = (first + ri[2:4]).astype(I32)
    sub = (pos[:, None, :] * ROW_TILE + jnp.arange(ROW_TILE, dtype=I32)[None, :, None]).reshape(2 * n * ROW_TILE)
    xs = _sc_scatter_rows(xn.reshape(ROW_TILE * n, LANES), sub, 2 * n * ROW_TILE)
    ys = _experts((blk, e, lo, hi), xs, w1, w3, w2)
    z = _sc_gather_rows(ys, sub).reshape(2 * ROW_TILE, n, LANES)
    return _combine(h2, z, rg.T, g_final, tm, y_prev, row0, n_total)


def _pad_time(a, s_pad):
    return jnp.pad(a, ((0, 0), (0, s_pad - a.shape[1])) + ((0, 0),) * (a.ndim - 2))


def _layer(x, past, mem_k, mem_v, wts, lam_init, g_final, *, tm, tq, tk, moe_parts):
    b, t, d = x.shape
    n = b * t
    p_len = 0 if past is None else past[0].shape[1]
    s_valid = p_len + t
    s_pad = -(-s_valid // tk) * tk
    pos = p_len + jnp.arange(t)

    fk, fv, logf, dk, dv, qf, kf, vf, qd, kd, vd = _inproj(
        x, wts['g_mix'], wts['w_main'], wts['w_ff'], wts['b_forget'], pos, tm)
    time_major = tm <= t

    def leaf(a, dims):
        if time_major:
            return jnp.moveaxis(a.reshape((b,) + dims + (t,)), -1, 1)
        return a.reshape((b, t) + dims)

    new = (leaf(fk, (FOX_HEADS, HEAD_DIM)), leaf(fv, (FOX_HEADS, HEAD_DIM)), leaf(logf, (FOX_HEADS,)),
           leaf(dk, (DIFF_HEADS, 2, HEAD_DIM)), dv.reshape(b, t, DIFF_HEADS, 2 * HEAD_DIM))

    def keys_t(cur, old_rows):
        if not time_major:
            cur = jnp.swapaxes(cur.reshape(b, t, FOX_W), 1, 2)
        if old_rows is not None:
            prev = jnp.swapaxes(old_rows.reshape(b, p_len, FOX_W).astype(BF16), 1, 2)
            cur = jnp.concatenate([prev, cur], axis=2)
        return jnp.pad(cur, ((0, 0), (0, 0), (0, s_pad - s_valid)))

    def with_past(new_b16, old_rows):
        cur = new_b16.reshape(b, t, FOX_W)
        if old_rows is not None:
            cur = jnp.concatenate([old_rows.reshape(b, p_len, FOX_W).astype(BF16), cur], axis=1)
        return _pad_time(cur, s_pad)

    logf_t = logf if time_major else jnp.swapaxes(logf.reshape(b, t, FOX_HEADS), 1, 2)
    if past is not None:
        logf_t = jnp.concatenate([jnp.swapaxes(past[2].astype(F32), 1, 2), logf_t], axis=2)
    k_bias, q_bias_t = _fox_bias(jnp.pad(logf_t, ((0, 0), (0, 0), (0, s_pad - s_valid))))
    q_bias = jnp.swapaxes(q_bias_t[:, :, p_len:p_len + t], 1, 2).reshape(n, LANES)

    old = (None,) * 5 if past is None else past
    geom = dict(b=b, t=t, s_pad=s_pad, s_valid=s_valid, past=p_len, tq=tq, tk=tk, lam_init=lam_init)
    fox_o = _self_attention(qf, keys_t(kf, old[0]), with_past(vf, old[1]), (q_bias, k_bias),
                            fox=True, **geom)
    diff_o = _self_attention(qd, keys_t(kd, old[3]), with_past(vd, old[4]),
                             (wts['lam_vecs'], wts['g_diff']), fox=False, **geom)
    h1, qm = _outproj(x.reshape(n, d), fox_o, diff_o, wts['wo_f'], wts['wo_d'], wts['g_mem_q'],
                      wts['w_mq'], tm)
    assert b % moe_parts == 0
    bp = b // moe_parts
    y = None
    for part in range(moe_parts):
        h2, xn, logits = _memattn(qm, mem_k, mem_v, h1, wts['w_mo'], wts['g_ffn'], wts['w_router'],
                                  wts['b_router'], b0=part * bp, b=bp, t=t, tq=tq)
        ri, rg, cnt = _router(logits.T, min(tm, bp * t))
        counts = cnt[:N_EXPERTS, 0].astype(I32)
        y = _moe_and_final(h2, xn, ri, rg, counts, wts['w1'], wts['w3'], wts['w2'], g_final, tm=tm,
                           y_prev=y, row0=part * bp * t, n_total=n)
    return y.reshape(b, t, d), new


def kernel(x_prompt, x_sample, cache_fox_k, cache_fox_v, cache_fox_logf, cache_diff_k, cache_diff_v, cache_mem_k, cache_mem_v, mem_prompt, g_mix, w_in, b_forget, lam_q1, lam_k1, lam_q2, lam_k2, g_diff, w_o, g_mem_q, g_mem_kv, w_mq, w_mk, w_mv, w_mo, g_ffn, w_group, b_group, w_erouter, b_erouter, w1, w3, w2, g_final):
    depth = w_in.shape[0]
    assert depth == 1
    l = 0
    lam_init = 0.8 - 0.6 * math.exp(-0.3 * l)
    d = D_MODEL
    w = w_in[l]
    ff0 = 3 * FOX_W
    w_router = jnp.zeros((d, LANES), F32).at[:, :N_EXPERTS].set(w_erouter[l])
    w_router = w_router.at[:, GROUP_LANE0:GROUP_LANE0 + N_GROUPS].set(w_group[l])
    b_router = jnp.zeros((1, LANES), F32).at[0, :N_EXPERTS].set(b_erouter[l])
    b_router = b_router.at[0, GROUP_LANE0:GROUP_LANE0 + N_GROUPS].set(b_group[l])
    wts = {
        'g_mix': g_mix[l],
        'w_main': jnp.concatenate([w[:, :ff0], w[:, ff0 + FOX_HEADS:]], axis=1).astype(BF16),
        'w_ff': jnp.pad(w[:, ff0:ff0 + FOX_HEADS], ((0, 0), (0, LANES - FOX_HEADS))).astype(BF16),
        'b_forget': b_forget[l],
        'lam_vecs': jnp.stack([lam_q1[l], lam_k1[l], lam_q2[l], lam_k2[l]]),
        'g_diff': g_diff[l],
        'wo_f': w_o[l][:FOX_W].astype(BF16), 'wo_d': w_o[l][FOX_W:].astype(BF16),
        'g_mem_q': g_mem_q[l], 'w_mq': w_mq[l].astype(BF16), 'w_mo': w_mo[l].astype(BF16),
        'g_ffn': g_ffn[l], 'w_router': w_router.astype(BF16), 'b_router': b_router,
        'w1': w1[l].astype(BF16), 'w3': w3[l].astype(BF16), 'w2': w2[l].astype(BF16),
    }

    bp, tp, _ = x_prompt.shape
    mk, mv, mk_b, mv_b = _memkv(mem_prompt.reshape(bp * MEM_LEN, d), g_mem_kv[l],
                                w_mk[l].astype(BF16), w_mv[l].astype(BF16), 512)
    yp, new_p = _layer(x_prompt, None, mk_b.reshape(bp, MEM_LEN, d), mv_b.reshape(bp, MEM_LEN, d),
                       wts, lam_init, g_final, tm=512, tq=512, tk=512, moe_parts=2)

    bs, ts, _ = x_sample.shape
    past = (cache_fox_k[l], cache_fox_v[l], cache_fox_logf[l], cache_diff_k[l], cache_diff_v[l])
    ys, new_s = _layer(x_sample, past, cache_mem_k[l].reshape(bs, MEM_LEN, d).astype(BF16),
                       cache_mem_v[l].reshape(bs, MEM_LEN, d).astype(BF16),
                       wts, lam_init, g_final, tm=bs * ts, tq=ts, tk=512, moe_parts=1)

    mem_shape = (1, bp, MEM_LEN, MEM_HEADS, MEM_HEAD_DIM)
    return (yp, ys) + tuple(a[None] for a in new_p) + (mk.reshape(mem_shape), mv.reshape(mem_shape)) \
        + tuple(a[None] for a in new_s)
```

```python
import functools
import math

import jax
import jax.numpy as jnp
from jax import lax
from jax.experimental import pallas as pl
from jax.experimental.pallas import tpu as pltpu
from jax.experimental.pallas import tpu_sc as plsc

F32 = jnp.float32
BF16 = jnp.bfloat16
I32 = jnp.int32

D_MODEL = 1024
HEAD_DIM = 64
FOX_HEADS = 8
DIFF_HEADS = 4
FOX_W = FOX_HEADS * HEAD_DIM
DIFF_W = DIFF_HEADS * 2 * HEAD_DIM
ROT_DIM = HEAD_DIM // 4
ROPE_THETA = 500000.0
CHUNK = 64
MEM_LEN = 256
MEM_HEADS = 4
MEM_HEAD_DIM = D_MODEL // MEM_HEADS
N_GROUPS = 4
EXPERTS_PER_GROUP = 8
N_EXPERTS = N_GROUPS * EXPERTS_PER_GROUP
D_EXPERT = 512
EPS = 1e-6

LANES = 128
LOG2E = 1.4426950408889634
NEG = -1e30
VMEM_LIMIT_BYTES = 56 * 1024 * 1024
MOE_ROWS = 256
TOKEN_TILE = 512
KEY_TILE = 512
PROMPT_MOE_PARTS = 2
GROUP_LANE0 = N_EXPERTS


def _params(*sem):
    return pltpu.CompilerParams(dimension_semantics=sem, vmem_limit_bytes=VMEM_LIMIT_BYTES)


def _rms(x, g):
    return (x * lax.rsqrt(jnp.mean(x * x, axis=-1, keepdims=True) + EPS)) * g


def _rope(x, cos, sin_lo, sin_hi):
    outs = []
    for c in range(x.shape[1] // LANES):
        xc = x[:, c * LANES:(c + 1) * LANES]
        up = pltpu.roll(xc, LANES - ROT_DIM // 2, axis=1)
        dn = pltpu.roll(xc, ROT_DIM // 2, axis=1)
        outs.append(xc * cos + up * sin_lo + dn * sin_hi)
    return jnp.concatenate(outs, axis=1)


def _inproj_kernel(x_ref, g_ref, w_ref, wf_ref, bf_ref, cos_ref, slo_ref, shi_ref,
                   fk_ref, fv_ref, logf_ref, dk_ref, dv_ref,
                   qf_ref, kf_ref, vf_ref, qd_ref, kd_ref, vd_ref, *, q_scale, time_major):
    n = _rms(x_ref[...], g_ref[...]).astype(BF16)
    turn = (lambda a: a.T) if time_major else (lambda a: a)

    def proj(c):
        return jnp.dot(n, w_ref[:, c * FOX_W:(c + 1) * FOX_W], preferred_element_type=F32)

    cos, slo, shi = cos_ref[...], slo_ref[...], shi_ref[...]
    qf_ref[...] = (proj(0) * q_scale).astype(BF16)
    fk = turn(proj(1))
    fk_ref[...] = fk
    kf_ref[...] = fk.astype(BF16)
    fv = proj(2)
    fv_ref[...] = turn(fv)
    vf_ref[...] = fv.astype(BF16)
    qd_ref[...] = (_rope(proj(3), cos, slo, shi) * q_scale).astype(BF16)
    dk = turn(_rope(proj(4), cos, slo, shi))
    dk_ref[...] = dk
    kd_ref[...] = dk.astype(BF16)
    dv = proj(5)
    for h in range(DIFF_HEADS):
        dv_ref[pl.ds(h, dv.shape[0], stride=DIFF_HEADS), :] = dv[:, h * LANES:(h + 1) * LANES]
    vd_ref[...] = dv.astype(BF16)
    z = jnp.dot(n, wf_ref[...], preferred_element_type=F32) + bf_ref[...]
    logf = jnp.minimum(z, 0.0) - jnp.log1p(jnp.exp(-jnp.abs(z)))
    logf_ref[...] = logf.T[:FOX_HEADS] if time_major else logf[:, :FOX_HEADS]


def _rope_tables(pos):
    half = ROT_DIM // 2
    inv = jnp.power(ROPE_THETA, -jnp.arange(half, dtype=F32) / half)
    ang = pos.astype(F32)[:, None] * inv[None, :]
    cos, sin = jnp.cos(ang), jnp.sin(ang)
    t = pos.shape[0]
    pad = jnp.zeros((t, HEAD_DIM - ROT_DIM), F32)
    zero = jnp.zeros((t, half), F32)
    cos64 = jnp.concatenate([cos, cos, pad + 1.0], axis=1)
    slo64 = jnp.concatenate([-sin, zero, pad], axis=1)
    shi64 = jnp.concatenate([zero, sin, pad], axis=1)
    return tuple(jnp.tile(a, (1, LANES // HEAD_DIM)) for a in (cos64, slo64, shi64))


def _inproj(x, g, w_main, w_ff, b_ff, pos, tm):
    b, t, d = x.shape
    n = b * t
    tables = _rope_tables(pos)
    time_major = tm <= t
    if time_major:
        assert t % tm == 0
        per = t // tm
        tab_map = lambda i: (i % per, 0)
        turned = pl.BlockSpec((None, FOX_W, tm), lambda i: (i // per, 0, i % per))
        gate = pl.BlockSpec((None, FOX_HEADS, tm), lambda i: (i // per, 0, i % per))
        turned_shape, gate_shape = (b, FOX_W, t), (b, FOX_HEADS, t)
    else:
        assert tm % t == 0
        tables = tuple(jnp.tile(a, (tm // t, 1)) for a in tables)
        tab_map = lambda i: (0, 0)
    assert n % tm == 0
    row = lambda i: (i, 0)
    const = lambda i: (0, 0)
    wide = pl.BlockSpec((tm, FOX_W), row)
    if not time_major:
        turned, gate = wide, pl.BlockSpec((tm, FOX_HEADS), row)
        turned_shape, gate_shape = (n, FOX_W), (n, FOX_HEADS)
    tab = pl.BlockSpec((tm, LANES), tab_map)
    sds = jax.ShapeDtypeStruct
    return pl.pallas_call(
        functools.partial(_inproj_kernel, q_scale=HEAD_DIM ** -0.5 * LOG2E, time_major=time_major),
        grid=(n // tm,),
        in_specs=[pl.BlockSpec((tm, d), row), pl.BlockSpec((1, d), const),
                  pl.BlockSpec(w_main.shape, const), pl.BlockSpec(w_ff.shape, const),
                  pl.BlockSpec((1, LANES), const), tab, tab, tab],
        out_specs=[turned, turned, gate, turned, pl.BlockSpec((tm * DIFF_HEADS, LANES), row),
                   wide, turned, wide, wide, turned, wide],
        out_shape=[sds(turned_shape, F32), sds(turned_shape, F32), sds(gate_shape, F32),
                   sds(turned_shape, F32), sds((n * DIFF_HEADS, LANES), F32),
                   sds((n, FOX_W), BF16), sds(turned_shape, BF16), sds((n, FOX_W), BF16),
                   sds((n, FOX_W), BF16), sds(turned_shape, BF16), sds((n, FOX_W), BF16)],
        compiler_params=_params("arbitrary"),
    )(x.reshape(n, d), g.reshape(1, d), w_main, w_ff,
      jnp.pad(b_ff.reshape(1, FOX_HEADS), ((0, 0), (0, LANES - FOX_HEADS))), *tables)


AUX_LANES = 8


def _bf16_part(x):
    return pltpu.bitcast(pltpu.bitcast(x, jnp.uint32) & jnp.uint32(0xFFFF0000), F32)


def _cumsum_kernel(x_ref, kb_ref, qb_ref):
    x = x_ref[...]
    s_len = x.shape[1]
    lane = lax.broadcasted_iota(I32, x.shape, 1)
    shift = 1
    while shift < s_len:
        x = x + jnp.where(lane >= shift, pltpu.roll(x, shift, axis=1), 0.0)
        shift *= 2
    c = x * LOG2E
    hi = _bf16_part(c)
    mid = _bf16_part(c - hi)
    lo = _bf16_part(c - hi - mid)
    slot = lax.broadcasted_iota(I32, (LANES, x.shape[0]), 0)
    head = lax.broadcasted_iota(I32, (LANES, x.shape[0]), 1)

    def place(first_slot):
        out = jnp.zeros((LANES, s_len), F32)
        for i, part in enumerate((hi, mid, lo)):
            sel = jnp.where(slot == head * AUX_LANES + first_slot + i, 1.0, 0.0).astype(BF16)
            out = out + jnp.dot(sel, part.astype(BF16), preferred_element_type=F32)
        return out

    slot_col = lax.broadcasted_iota(I32, (LANES, s_len), 0)
    used = slot_col < FOX_HEADS * AUX_LANES
    within = slot_col % AUX_LANES
    kb_ref[...] = (jnp.where(used & (within < 3), 1.0, 0.0) - place(3)).astype(BF16)
    qb_ref[...] = (jnp.where(used & (within >= 3) & (within < 6), 1.0, 0.0) + place(0)).astype(BF16)


def _fox_bias(logf_t):
    b, h, s = logf_t.shape
    x = jnp.pad(logf_t, ((0, 0), (0, 16 - h), (0, 0)))
    out = pl.BlockSpec((None, LANES, s), lambda i: (i, 0, 0))
    return pl.pallas_call(
        _cumsum_kernel, grid=(b,), in_specs=[pl.BlockSpec((None, 16, s), lambda i: (i, 0, 0))],
        out_specs=[out, out],
        out_shape=[jax.ShapeDtypeStruct((b, LANES, s), BF16)] * 2,
        compiler_params=_params("arbitrary"),
    )(x)


def _attn_kernel(*refs, tq, tk, past, s_valid, fox, lam_init):
    if fox:
        q_ref, qa_ref, k_ref, ka_ref, v_ref, o_ref, sa_ref, sb_ref, mx_ref, m_ref, l_ref, acc_ref = refs
    else:
        q_ref, k_ref, v_ref, lam_ref, g_ref, o_ref, sa_ref, sb_ref, mx_ref, m_ref, l_ref, acc_ref = refs
    i = pl.program_id(2)
    q = q_ref[...]
    lane = lax.broadcasted_iota(I32, (tq, LANES), 1)
    low = lane < HEAD_DIM
    zero = jnp.zeros_like(q)
    q_part = [jnp.where(low, q, zero), jnp.where(low, zero, q)]
    if fox:
        qa = qa_ref[...]
        for h in range(2):
            lane0 = (2 * pl.program_id(1) + h) * AUX_LANES
            aux = jnp.where((lane >= lane0) & (lane < lane0 + AUX_LANES), qa, zero)
            q_part[h] = jnp.concatenate([q_part[h], aux], axis=1)
    m_ref[...] = jnp.full(m_ref.shape, NEG, F32)
    l_ref[...] = jnp.zeros(l_ref.shape, F32)
    acc_ref[...] = jnp.zeros(acc_ref.shape, F32)
    q0 = past + i * tq
    nb = q0 // tk
    bufs = (sa_ref, sb_ref)

    def scores(j, buf):
        ks = pl.multiple_of(j * tk, tk)
        k = k_ref[:, pl.ds(ks, tk)]
        if fox:
            k = jnp.concatenate([k, ka_ref[:, pl.ds(ks, tk)]], axis=0)
        for h in range(2):
            s = jnp.dot(q_part[h], k, preferred_element_type=F32)
            bufs[buf][h] = s
            mx_ref[buf, h] = jnp.broadcast_to(jnp.max(s, axis=1, keepdims=True), (tq, LANES))

    def consume(j, buf, use_mask):
        ks = pl.multiple_of(j * tk, tk)
        v = jnp.concatenate([v_ref[pl.ds(ks, tk), :], jnp.ones((tk, LANES), BF16)], axis=1)
        pv, alpha = [], []
        for h in range(2):
            t = bufs[buf][h]
            if use_mask:
                qpos = q0 + lax.broadcasted_iota(I32, (tq, tk), 0)
                kpos = j * tk + lax.broadcasted_iota(I32, (tq, tk), 1)
                if fox:
                    seen = kpos <= qpos
                else:
                    seen = kpos < jnp.minimum((qpos // CHUNK + 1) * CHUNK, s_valid)
                t = jnp.where(seen, t, NEG)
                mx = jnp.max(t, axis=1, keepdims=True)
            else:
                mx = mx_ref[buf, h]
            m_prev = m_ref[h]
            m_new = jnp.maximum(m_prev, mx)
            a = jnp.exp2(m_prev - m_new)
            p = jnp.exp2(t - jnp.concatenate([m_new] * (tk // LANES), axis=1))
            m_ref[h] = m_new
            both = jnp.dot(p.astype(BF16), v, preferred_element_type=F32)
            pv.append(both[:, :LANES])
            l_ref[h] = a * l_ref[h] + both[:, LANES:]
            alpha.append(a)
        if fox:
            acc_ref[0] = acc_ref[0] * jnp.where(low, alpha[0], alpha[1]) + jnp.where(low, pv[0], pv[1])
        else:
            for h in range(2):
                acc_ref[h] = acc_ref[h] * alpha[h] + pv[h]

    scores(0, 0)

    def pair(jj, c):
        j = 2 * jj
        scores(j + 1, 1)
        consume(j, 0, False)
        scores(j + 2, 0)
        consume(j + 1, 1, False)
        return c

    lax.fori_loop(0, nb // 2, pair, 0)

    @pl.when(nb % 2 == 1)
    def _():
        scores(nb, 1)
        consume(nb - 1, 0, False)
        consume(nb, 1, True)

    @pl.when(nb % 2 == 0)
    def _():
        consume(nb, 0, True)

    if fox:
        o = acc_ref[0] / jnp.where(low, l_ref[0], l_ref[1])
    else:
        lam_v = lam_ref[...]
        lam = (jnp.exp(jnp.sum(lam_v[0:1] * lam_v[1:2], axis=1, keepdims=True))
               - jnp.exp(jnp.sum(lam_v[2:3] * lam_v[3:4], axis=1, keepdims=True)) + lam_init)
        o = acc_ref[0] / l_ref[0] - lam * (acc_ref[1] / l_ref[1])
        o = _rms(o, g_ref[...]) * (1.0 - lam_init)
    o_ref[...] = o.astype(o_ref.dtype)


def _self_attention(q, k, v, extra, *, b, t, s_pad, s_valid, past, tq, tk, fox, lam_init):
    assert tk % tq == 0 and past % tq == 0 and t % tq == 0 and tk % CHUNK == 0 and s_pad % tk == 0
    assert tq % CHUNK == 0 or (CHUNK % tq == 0 and past % CHUNK == 0)
    n_q = t // tq
    k_spec = pl.BlockSpec((None, LANES, s_pad), lambda bi, p, i: (bi, p, 0))
    v_spec = pl.BlockSpec((None, s_pad, LANES), lambda bi, p, i: (bi, 0, p))
    q_spec = pl.BlockSpec((tq, LANES), lambda bi, p, i: (bi * n_q + i, p))
    const = lambda bi, p, i: (0, 0)
    if fox:
        in_specs = [q_spec, pl.BlockSpec((tq, LANES), lambda bi, p, i: (bi * n_q + i, 0)), k_spec,
                    pl.BlockSpec((None, LANES, s_pad), lambda bi, p, i: (bi, 0, 0)), v_spec]
        args = (q, extra[0], k, extra[1], v)
    else:
        in_specs = [q_spec, k_spec, v_spec, pl.BlockSpec((4, HEAD_DIM), const), pl.BlockSpec((1, LANES), const)]
        args = (q, k, v, extra[0], extra[1].reshape(1, LANES))
    stat = pltpu.VMEM((2, tq, LANES), F32)
    return pl.pallas_call(
        functools.partial(_attn_kernel, tq=tq, tk=tk, past=past, s_valid=s_valid, fox=fox, lam_init=lam_init),
        grid=(b, FOX_W // LANES, n_q),
        in_specs=in_specs,
        out_specs=q_spec,
        out_shape=jax.ShapeDtypeStruct((b * t, FOX_W), BF16),
        scratch_shapes=[pltpu.VMEM((2, tq, tk), F32), pltpu.VMEM((2, tq, tk), F32),
                        pltpu.VMEM((2, 2, tq, LANES), F32), stat, stat, stat],
        compiler_params=_params("arbitrary", "arbitrary", "arbitrary"),
    )(*args)


def _outproj_kernel(x_ref, fo_ref, do_ref, wof_ref, wod_ref, g_ref, wq_ref, h_ref, q_ref, *, q_scale):
    h = (x_ref[...] + jnp.dot(fo_ref[...], wof_ref[...], preferred_element_type=F32)
         + jnp.dot(do_ref[...], wod_ref[...], preferred_element_type=F32))
    h_ref[...] = h
    n = _rms(h, g_ref[...]).astype(BF16)
    q_ref[...] = (jnp.dot(n, wq_ref[...], preferred_element_type=F32) * q_scale).astype(BF16)


def _outproj(x2, fox_o, diff_o, wo_f, wo_d, g_mem_q, w_mq, tm):
    n, d = x2.shape
    row = lambda i: (i, 0)
    const = lambda i: (0, 0)
    return pl.pallas_call(
        functools.partial(_outproj_kernel, q_scale=MEM_HEAD_DIM ** -0.5 * LOG2E),
        grid=(n // tm,),
        in_specs=[pl.BlockSpec((tm, d), row), pl.BlockSpec((tm, FOX_W), row),
                  pl.BlockSpec((tm, DIFF_W), row), pl.BlockSpec((FOX_W, d), const),
                  pl.BlockSpec((DIFF_W, d), const), pl.BlockSpec((1, d), const),
                  pl.BlockSpec((d, d), const)],
        out_specs=[pl.BlockSpec((tm, d), row), pl.BlockSpec((tm, d), row)],
        out_shape=[jax.ShapeDtypeStruct((n, d), F32), jax.ShapeDtypeStruct((n, d), BF16)],
        compiler_params=_params("arbitrary"),
    )(x2, fox_o, diff_o, wo_f, wo_d, g_mem_q.reshape(1, d), w_mq)


def _memkv_kernel(m_ref, g_ref, wk_ref, wv_ref, k_ref, v_ref, kb_ref, vb_ref):
    n = _rms(m_ref[...], g_ref[...]).astype(BF16)
    k = jnp.dot(n, wk_ref[...], preferred_element_type=F32)
    v = jnp.dot(n, wv_ref[...], preferred_element_type=F32)
    k_ref[...] = k
    v_ref[...] = v
    kb_ref[...] = k.astype(BF16)
    vb_ref[...] = v.astype(BF16)


def _memkv(mem2, g, wk, wv, tm):
    n, d = mem2.shape
    row = lambda i: (i, 0)
    const = lambda i: (0, 0)
    blk = pl.BlockSpec((tm, d), row)
    wspec = pl.BlockSpec((d, d), const)
    return pl.pallas_call(
        _memkv_kernel, grid=(n // tm,),
        in_specs=[blk, pl.BlockSpec((1, d), const), wspec, wspec],
        out_specs=[blk, blk, blk, blk],
        out_shape=[jax.ShapeDtypeStruct((n, d), F32)] * 2 + [jax.ShapeDtypeStruct((n, d), BF16)] * 2,
        compiler_params=_params("arbitrary"),
    )(mem2, g.reshape(1, d), wk, wv)


U32 = jnp.uint32
HALF_D = D_MODEL // 2
ROW_TILE = HALF_D // LANES
BF16_BITS = 0xFFFF0000


def _pack_rows(val):
    def rounded(x):
        bits = pltpu.bitcast(x, U32)
        return (bits + U32(0x7FFF) + ((bits >> 16) & U32(1))) & U32(BF16_BITS)
    return rounded(val[:, :HALF_D]) | (rounded(val[:, HALF_D:]) >> 16)


def _unpack_rows(words):
    return jnp.concatenate([pltpu.bitcast(words & U32(BF16_BITS), F32), pltpu.bitcast(words << 16, F32)], axis=1)


def _load_tile_rows(ref):
    rows = ref.shape[0] // ROW_TILE
    return jnp.concatenate([ref[pl.ds(s, rows, stride=ROW_TILE), :] for s in range(ROW_TILE)], axis=1)


def _store_tile_rows(ref, words):
    rows = words.shape[0]
    for s in range(ROW_TILE):
        ref[pl.ds(s, rows, stride=ROW_TILE), :] = words[:, s * LANES:(s + 1) * LANES]


def _memattn_kernel(q_ref, k_ref, v_ref, h_ref, wo_ref, g_ref, wr_ref, br_ref,
                    h2_ref, xn_ref, logit_ref):
    q = q_ref[...]
    outs = []
    for h in range(MEM_HEADS):
        sl = slice(h * MEM_HEAD_DIM, (h + 1) * MEM_HEAD_DIM)
        s = lax.dot_general(q[:, sl], k_ref[:, sl], (((1,), (1,)), ((), ())), preferred_element_type=F32)
        p = jnp.exp2(s - jnp.max(s, axis=1, keepdims=True))
        o = jnp.dot(p.astype(BF16), v_ref[:, sl], preferred_element_type=F32)
        outs.append((o / jnp.sum(p, axis=1, keepdims=True)).astype(BF16))
    h2 = h_ref[...] + jnp.dot(jnp.concatenate(outs, axis=1), wo_ref[...], preferred_element_type=F32)
    h2_ref[...] = h2
    xn_f32 = _rms(h2, g_ref[...])
    words = _pack_rows(xn_f32)
    for s in range(ROW_TILE):
        xn_ref[s] = words[:, s * LANES:(s + 1) * LANES]
    logit_ref[...] = jnp.dot(xn_f32.astype(BF16), wr_ref[...], preferred_element_type=F32) + br_ref[...]


def _memattn(qm, mk, mv, h1, w_mo, g_ffn, w_router, b_router, *, b0, b, t, tq):
    d = h1.shape[1]
    n = b * t
    n_q = t // tq
    row = lambda bi, i: (bi * n_q + i, 0)
    src_row = lambda bi, i: ((b0 + bi) * n_q + i, 0)
    const = lambda bi, i: (0, 0)
    mem = pl.BlockSpec((None, MEM_LEN, d), lambda bi, i: (b0 + bi, 0, 0))
    src = pl.BlockSpec((tq, d), src_row)
    blk = pl.BlockSpec((tq, d), row)
    planes = pl.BlockSpec((ROW_TILE, tq, LANES), lambda bi, i: (0, bi * n_q + i, 0))
    return pl.pallas_call(
        _memattn_kernel,
        grid=(b, n_q),
        in_specs=[src, mem, mem, src, pl.BlockSpec((d, d), const), pl.BlockSpec((1, d), const),
                  pl.BlockSpec((d, LANES), const), pl.BlockSpec((1, LANES), const)],
        out_specs=[blk, planes, pl.BlockSpec((tq, LANES), row)],
        out_shape=[jax.ShapeDtypeStruct((n, d), F32), jax.ShapeDtypeStruct((ROW_TILE, n, LANES), U32),
                   jax.ShapeDtypeStruct((n, LANES), F32)],
        compiler_params=_params("arbitrary", "arbitrary"),
    )(qm, mk, mv, h1, w_mo, g_ffn.reshape(1, d), w_router, b_router)


ROUTER_ROWS = 48


def _first_slot(hit, slot_f32):
    return jnp.min(jnp.where(hit, slot_f32, float(LANES)), axis=0, keepdims=True).astype(I32)


def _router_kernel(lt_ref, ri_ref, rg_ref, cnt_ref, carry_ref, *, rt):
    @pl.when(pl.program_id(0) == 0)
    def _():
        carry_ref[...] = jnp.zeros(carry_ref.shape, F32)

    lt = lt_ref[0:ROUTER_ROWS, :]
    slot = lax.broadcasted_iota(I32, (ROUTER_ROWS, rt), 0)
    slot_f = slot.astype(F32)
    g_log = jnp.where((slot >= GROUP_LANE0) & (slot < GROUP_LANE0 + N_GROUPS), lt, NEG)
    g_max = jnp.max(g_log, axis=0, keepdims=True)
    g_idx = _first_slot(g_log == g_max, slot_f) - GROUP_LANE0
    g_p = 1.0 / jnp.sum(jnp.exp(g_log - g_max), axis=0, keepdims=True)
    in_group = (slot >= g_idx * EXPERTS_PER_GROUP) & (slot < (g_idx + 1) * EXPERTS_PER_GROUP)
    e_log = jnp.where(in_group, lt, NEG)
    e_exp = jnp.exp(e_log - jnp.max(e_log, axis=0, keepdims=True))
    prob = jnp.where(in_group, e_exp / jnp.sum(e_exp, axis=0, keepdims=True), -1.0)
    p1 = jnp.max(prob, axis=0, keepdims=True)
    i1 = _first_slot(prob == p1, slot_f)
    rest = jnp.where(slot == i1, -1.0, prob)
    p2 = jnp.max(rest, axis=0, keepdims=True)
    i2 = _first_slot(rest == p2, slot_f)
    top_sum = p1 + p2

    hot1, hot2 = slot == i1, slot == i2
    onehot = jnp.where(hot1 | hot2, 1.0, 0.0)
    earlier = lax.broadcasted_iota(I32, (rt, rt), 0) < lax.broadcasted_iota(I32, (rt, rt), 1)
    before = jnp.dot(onehot.astype(BF16), jnp.where(earlier, 1.0, 0.0).astype(BF16),
                     preferred_element_type=F32) + carry_ref[0:ROUTER_ROWS, 0:1]
    r1 = jnp.sum(jnp.where(hot1, before, 0.0), axis=0, keepdims=True)
    r2 = jnp.sum(jnp.where(hot2, before, 0.0), axis=0, keepdims=True)
    carry_ref[0:ROUTER_ROWS, :] = carry_ref[0:ROUTER_ROWS, :] + jnp.sum(onehot, axis=1, keepdims=True)
    cnt_ref[...] = carry_ref[...]

    ri_ref[...] = jnp.concatenate([i1, i2, r1.astype(I32), r2.astype(I32), jnp.zeros((4, rt), I32)], axis=0)
    rg_ref[...] = jnp.concatenate([g_p * p1 / top_sum, g_p * p2 / top_sum, jnp.zeros((6, rt), F32)], axis=0)


def _router(logits_t, rt):
    n = logits_t.shape[1]
    col = lambda i: (0, i)
    return pl.pallas_call(
        functools.partial(_router_kernel, rt=rt),
        grid=(n // rt,),
        in_specs=[pl.BlockSpec((LANES, rt), col)],
        out_specs=[pl.BlockSpec((8, rt), col), pl.BlockSpec((8, rt), col),
                   pl.BlockSpec((LANES, LANES), lambda i: (0, 0))],
        out_shape=[jax.ShapeDtypeStruct((8, n), I32), jax.ShapeDtypeStruct((8, n), F32),
                   jax.ShapeDtypeStruct((LANES, LANES), F32)],
        scratch_shapes=[pltpu.VMEM((LANES, LANES), F32)],
        compiler_params=_params("arbitrary"),
    )(logits_t)


SC_CORES, SC_SUBCORES = 2, 16
SC_WINDOW = 128


def _sc_mesh():
    return plsc.VectorSubcoreMesh(core_axis_name="core", subcore_axis_name="subcore")


def _sc_split(n_windows):
    if n_windows % (SC_CORES * SC_SUBCORES) == 0:
        return ("core", "subcore")
    assert n_windows % SC_SUBCORES == 0
    return "subcore"


def _sc_repeat(idx):
    n_windows = idx.shape[0] // SC_WINDOW
    reps = SC_SUBCORES // math.gcd(n_windows, SC_SUBCORES)
    return jnp.tile(idx, reps)


def _sc_scatter_rows(src, idx, n_out):
    rows = src.shape[0]
    win = SC_WINDOW
    n_src_blocks = rows // win
    idx = _sc_repeat(idx)
    n_idx = idx.shape[0]

    @functools.partial(pl.kernel, out_type=jax.ShapeDtypeStruct((n_out, LANES), src.dtype),
                       mesh=_sc_mesh(), scratch_types=[])
    def scatter(x_hbm, i_hbm, o_hbm):
        def body(x_vmem, i_vmem):
            pltpu.sync_copy(x_vmem, o_hbm.at[i_vmem.at[0]])

        pltpu.emit_pipeline(
            body, grid=(n_idx // win,),
            in_specs=[pl.BlockSpec((win, LANES), lambda i: (i % n_src_blocks, 0)),
                      pl.BlockSpec((1, win), lambda i: (0, i))],
            out_specs=[],
            core_axis_name=_sc_split(n_idx // win), dimension_semantics=(pltpu.PARALLEL,),
        )(x_hbm, i_hbm)

    return scatter(src, idx.reshape(1, n_idx))


def _sc_gather_rows(src, idx):
    n_rows = idx.shape[0]
    idx = _sc_repeat(idx)
    n_out = idx.shape[0]
    win = SC_WINDOW

    @functools.partial(pl.kernel, out_type=jax.ShapeDtypeStruct((n_out, LANES), src.dtype),
                       mesh=_sc_mesh(), scratch_types=[])
    def gather(x_hbm, i_hbm, o_hbm):
        def body(i_vmem, o_vmem):
            pltpu.sync_copy(x_hbm.at[i_vmem.at[0]], o_vmem)

        pltpu.emit_pipeline(
            body, grid=(n_out // win,),
            in_specs=[pl.BlockSpec((1, win), lambda i: (0, i))],
            out_specs=[pl.BlockSpec((win, LANES), lambda i: (i, 0))],
            core_axis_name=_sc_split(n_out // win), dimension_semantics=(pltpu.PARALLEL,),
        )(i_hbm, o_hbm)

    return gather(src, idx.reshape(1, n_out))[:n_rows]


def _expert_kernel(blk_ref, e_ref, lo_ref, hi_ref, x_ref, w1_ref, w3_ref, w2_ref, y_ref):
    w = pl.program_id(0)
    lo, hi = lo_ref[w], hi_ref[w]
    row0 = blk_ref[w] * MOE_ROWS

    def compute():
        x = _unpack_rows(_load_tile_rows(x_ref)).astype(BF16)
        a = jnp.dot(x, w1_ref[...], preferred_element_type=F32)
        g = jnp.dot(x, w3_ref[...], preferred_element_type=F32)
        mid = ((a / (1.0 + jnp.exp(-a))) * g).astype(BF16)
        y = _pack_rows(jnp.dot(mid, w2_ref[...], preferred_element_type=F32))
        rows = row0 + lax.broadcasted_iota(I32, y.shape, 0)
        return y, (rows >= lo) & (rows < hi)

    @pl.when((hi > lo) & (lo == row0))
    def _():
        y, mine = compute()
        _store_tile_rows(y_ref, jnp.where(mine, y, jnp.zeros_like(y)))

    @pl.when((hi > lo) & (lo != row0))
    def _():
        y, mine = compute()
        _store_tile_rows(y_ref, jnp.where(mine, y, _load_tile_rows(y_ref)))


def _experts(items, xs, w1, w3, w2):
    d = D_MODEL
    n_items = items[0].shape[0]
    xmap = lambda w, blk, e, lo, hi: (blk[w], 0)
    wmap = lambda w, blk, e, lo, hi: (e[w], 0, 0)
    return pl.pallas_call(
        _expert_kernel,
        grid_spec=pltpu.PrefetchScalarGridSpec(
            num_scalar_prefetch=4, grid=(n_items,),
            in_specs=[pl.BlockSpec((MOE_ROWS * ROW_TILE, LANES), xmap),
                      pl.BlockSpec((None, d, D_EXPERT), wmap), pl.BlockSpec((None, d, D_EXPERT), wmap),
                      pl.BlockSpec((None, D_EXPERT, d), wmap)],
            out_specs=pl.BlockSpec((MOE_ROWS * ROW_TILE, LANES), xmap)),
        out_shape=jax.ShapeDtypeStruct(xs.shape, U32),
        compiler_params=_params("arbitrary"),
    )(*items, xs, w1, w3, w2)


def _work_items(counts, n_rows):
    n_blocks = n_rows // MOE_ROWS
    n_items = n_blocks + N_EXPERTS - 1
    ends = jnp.cumsum(counts)
    starts = ends - counts
    first_blk = starts // MOE_ROWS
    n_blk = jnp.where(counts > 0, (ends - 1) // MOE_ROWS - first_blk + 1, 0)
    item_end = jnp.cumsum(n_blk)
    total = item_end[-1]
    w = jnp.arange(n_items, dtype=I32)
    wc = jnp.minimum(w, total - 1)
    e = jnp.sum(item_end[None, :] <= wc[:, None], axis=1).astype(I32)
    mine = e[:, None] == jnp.arange(N_EXPERTS, dtype=I32)[None, :]

    def of_expert(per_expert):
        return jnp.sum(jnp.where(mine, per_expert[None, :], 0), axis=1)

    blk = (of_expert(first_blk) + wc - (of_expert(item_end) - of_expert(n_blk))).astype(I32)
    lo = jnp.maximum(of_expert(starts), blk * MOE_ROWS).astype(I32)
    hi = jnp.minimum(of_expert(ends), (blk + 1) * MOE_ROWS).astype(I32)
    live = w < total
    return blk, e, jnp.where(live, lo, 0), jnp.where(live, hi, 0), starts


def _combine_kernel(h_ref, z0_ref, z1_ref, rg_ref, g_ref, *rest):
    y_ref = rest[-1]
    rg = rg_ref[...]
    def rows_of(z_ref):
        return _unpack_rows(jnp.concatenate([z_ref[s] for s in range(ROW_TILE)], axis=1))

    moe = rows_of(z0_ref) * rg[:, 0:1] + rows_of(z1_ref) * rg[:, 1:2]
    y_ref[...] = _rms(h_ref[...] + moe, g_ref[...])


def _combine(h2, z, rg, g_final, tm, y_prev, row0, n_total):
    n, d = h2.shape
    n_t = n // tm
    row = lambda i: (i, 0)
    in_specs = [pl.BlockSpec((tm, d), row), pl.BlockSpec((ROW_TILE, tm, LANES), lambda i: (0, i, 0)),
                pl.BlockSpec((ROW_TILE, tm, LANES), lambda i: (1, i, 0)), pl.BlockSpec((tm, 8), row),
                pl.BlockSpec((1, d), lambda i: (0, 0))]
    args = [h2, z, z, rg, g_final.reshape(1, d)]
    aliases = {}
    if y_prev is not None:
        in_specs.append(pl.BlockSpec(memory_space=pl.ANY))
        args.append(y_prev)
        aliases = {len(args) - 1: 0}
    return pl.pallas_call(
        _combine_kernel, grid=(n_t,),
        in_specs=in_specs,
        out_specs=pl.BlockSpec((tm, d), lambda i: (row0 // tm + i, 0)),
        out_shape=jax.ShapeDtypeStruct((n_total, d), F32),
        input_output_aliases=aliases,
        compiler_params=_params("arbitrary"),
    )(*args)


def _moe_and_final(h2, xn, ri, rg, counts, w1, w3, w2, g_final, *, tm, y_prev, row0, n_total):
    n = h2.shape[0]
    blk, e, lo, hi, starts = _work_items(counts, 2 * n)
    expert = jnp.arange(N_EXPERTS, dtype=I32)[:, None, None]
    first = jnp.sum(jnp.where(ri[None, 0:2] == expert, starts.astype(I32)[:, None, None], 0), axis=0)
    pos = (first + ri[2:4]).astype(I32)
    sub = (pos[:, None, :] * ROW_TILE + jnp.arange(ROW_TILE, dtype=I32)[None, :, None]).reshape(2 * n * ROW_TILE)
    xs = _sc_scatter_rows(xn.reshape(ROW_TILE * n, LANES), sub, 2 * n * ROW_TILE)
    ys = _experts((blk, e, lo, hi), xs, w1, w3, w2)
    z = _sc_gather_rows(ys, sub).reshape(2 * ROW_TILE, n, LANES)
    return _combine(h2, z, rg.T, g_final, tm, y_prev, row0, n_total)


def _pad_time(a, s_pad):
    return jnp.pad(a, ((0, 0), (0, s_pad - a.shape[1])) + ((0, 0),) * (a.ndim - 2))


def _layer(x, past, mem_k, mem_v, wts, lam_init, g_final, *, tm, tq, tk, moe_parts):
    b, t, d = x.shape
    n = b * t
    p_len = 0 if past is None else past[0].shape[1]
    s_valid = p_len + t
    s_pad = -(-s_valid // tk) * tk
    pos = p_len + jnp.arange(t)

    fk, fv, logf, dk, dv, qf, kf, vf, qd, kd, vd = _inproj(
        x, wts['g_mix'], wts['w_main'], wts['w_ff'], wts['b_forget'], pos, tm)
    time_major = tm <= t

    def leaf(a, dims):
        if time_major:
            return jnp.moveaxis(a.reshape((b,) + dims + (t,)), -1, 1)
        return a.reshape((b, t) + dims)

    new = (leaf(fk, (FOX_HEADS, HEAD_DIM)), leaf(fv, (FOX_HEADS, HEAD_DIM)), leaf(logf, (FOX_HEADS,)),
           leaf(dk, (DIFF_HEADS, 2, HEAD_DIM)), dv.reshape(b, t, DIFF_HEADS, 2 * HEAD_DIM))

    def keys_t(cur, old_rows):
        if not time_major:
            cur = jnp.swapaxes(cur.reshape(b, t, FOX_W), 1, 2)
        if old_rows is not None:
            prev = jnp.swapaxes(old_rows.reshape(b, p_len, FOX_W).astype(BF16), 1, 2)
            cur = jnp.concatenate([prev, cur], axis=2)
        return jnp.pad(cur, ((0, 0), (0, 0), (0, s_pad - s_valid)))

    def with_past(new_b16, old_rows):
        cur = new_b16.reshape(b, t, FOX_W)
        if old_rows is not None:
            cur = jnp.concatenate([old_rows.reshape(b, p_len, FOX_W).astype(BF16), cur], axis=1)
        return _pad_time(cur, s_pad)

    logf_t = logf if time_major else jnp.swapaxes(logf.reshape(b, t, FOX_HEADS), 1, 2)
    if past is not None:
        logf_t = jnp.concatenate([jnp.swapaxes(past[2].astype(F32), 1, 2), logf_t], axis=2)
    k_bias, q_bias_t = _fox_bias(jnp.pad(logf_t, ((0, 0), (0, 0), (0, s_pad - s_valid))))
    q_bias = jnp.swapaxes(q_bias_t[:, :, p_len:p_len + t], 1, 2).reshape(n, LANES)

    old = (None,) * 5 if past is None else past
    geom = dict(b=b, t=t, s_pad=s_pad, s_valid=s_valid, past=p_len, tq=tq, tk=tk, lam_init=lam_init)
    fox_o = _self_attention(qf, keys_t(kf, old[0]), with_past(vf, old[1]), (q_bias, k_bias),
                            fox=True, **geom)
    diff_o = _self_attention(qd, keys_t(kd, old[3]), with_past(vd, old[4]),
                             (wts['lam_vecs'], wts['g_diff']), fox=False, **geom)
    h1, qm = _outproj(x.reshape(n, d), fox_o, diff_o, wts['wo_f'], wts['wo_d'], wts['g_mem_q'],
                      wts['w_mq'], tm)
    assert b % moe_parts == 0
    bp = b // moe_parts
    y = None
    for part in range(moe_parts):
        h2, xn, logits = _memattn(qm, mem_k, mem_v, h1, wts['w_mo'], wts['g_ffn'], wts['w_router'],
                                  wts['b_router'], b0=part * bp, b=bp, t=t, tq=tq)
        ri, rg, cnt = _router(logits.T, min(tm, bp * t))
        counts = cnt[:N_EXPERTS, 0].astype(I32)
        y = _moe_and_final(h2, xn, ri, rg, counts, wts['w1'], wts['w3'], wts['w2'], g_final, tm=tm,
                           y_prev=y, row0=part * bp * t, n_total=n)
    return y.reshape(b, t, d), new


def kernel(x_prompt, x_sample, cache_fox_k, cache_fox_v, cache_fox_logf, cache_diff_k, cache_diff_v, cache_mem_k, cache_mem_v, mem_prompt, g_mix, w_in, b_forget, lam_q1, lam_k1, lam_q2, lam_k2, g_diff, w_o, g_mem_q, g_mem_kv, w_mq, w_mk, w_mv, w_mo, g_ffn, w_group, b_group, w_erouter, b_erouter, w1, w3, w2, g_final):
    depth = w_in.shape[0]
    assert depth == 1
    l = 0
    lam_init = 0.8 - 0.6 * math.exp(-0.3 * l)
    d = D_MODEL
    w = w_in[l]
    ff0 = 3 * FOX_W
    w_router = jnp.zeros((d, LANES), F32).at[:, :N_EXPERTS].set(w_erouter[l])
    w_router = w_router.at[:, GROUP_LANE0:GROUP_LANE0 + N_GROUPS].set(w_group[l])
    b_router = jnp.zeros((1, LANES), F32).at[0, :N_EXPERTS].set(b_erouter[l])
    b_router = b_router.at[0, GROUP_LANE0:GROUP_LANE0 + N_GROUPS].set(b_group[l])
    wts = {
        'g_mix': g_mix[l],
        'w_main': jnp.concatenate([w[:, :ff0], w[:, ff0 + FOX_HEADS:]], axis=1).astype(BF16),
        'w_ff': jnp.pad(w[:, ff0:ff0 + FOX_HEADS], ((0, 0), (0, LANES - FOX_HEADS))).astype(BF16),
        'b_forget': b_forget[l],
        'lam_vecs': jnp.stack([lam_q1[l], lam_k1[l], lam_q2[l], lam_k2[l]]),
        'g_diff': g_diff[l],
        'wo_f': w_o[l][:FOX_W].astype(BF16), 'wo_d': w_o[l][FOX_W:].astype(BF16),
        'g_mem_q': g_mem_q[l], 'w_mq': w_mq[l].astype(BF16), 'w_mo': w_mo[l].astype(BF16),
        'g_ffn': g_ffn[l], 'w_router': w_router.astype(BF16), 'b_router': b_router,
        'w1': w1[l].astype(BF16), 'w3': w3[l].astype(BF16), 'w2': w2[l].astype(BF16),
    }

    bp, tp, _ = x_prompt.shape
    mk, mv, mk_b, mv_b = _memkv(mem_prompt.reshape(bp * MEM_LEN, d), g_mem_kv[l],
                                w_mk[l].astype(BF16), w_mv[l].astype(BF16), TOKEN_TILE)
    yp, new_p = _layer(x_prompt, None, mk_b.reshape(bp, MEM_LEN, d), mv_b.reshape(bp, MEM_LEN, d),
                       wts, lam_init, g_final, tm=TOKEN_TILE, tq=TOKEN_TILE, tk=KEY_TILE,
                       moe_parts=PROMPT_MOE_PARTS)

    bs, ts, _ = x_sample.shape
    past = (cache_fox_k[l], cache_fox_v[l], cache_fox_logf[l], cache_diff_k[l], cache_diff_v[l])
    ys, new_s = _layer(x_sample, past, cache_mem_k[l].reshape(bs, MEM_LEN, d).astype(BF16),
                       cache_mem_v[l].reshape(bs, MEM_LEN, d).astype(BF16),
                       wts, lam_init, g_final, tm=bs * ts, tq=ts, tk=KEY_TILE, moe_parts=1)

    mem_shape = (1, bp, MEM_LEN, MEM_HEADS, MEM_HEAD_DIM)
    return (yp, ys) + tuple(a[None] for a in new_p) + (mk.reshape(mem_shape), mv.reshape(mem_shape)) \
        + tuple(a[None] for a in new_s)
```

```python
import functools
import math

import jax
import jax.numpy as jnp
from jax import lax
from jax.experimental import pallas as pl
from jax.experimental.pallas import tpu as pltpu
from jax.experimental.pallas import tpu_sc as plsc

F32 = jnp.float32
BF16 = jnp.bfloat16
I32 = jnp.int32

D_MODEL = 1024
HEAD_DIM = 64
FOX_HEADS = 8
DIFF_HEADS = 4
FOX_W = FOX_HEADS * HEAD_DIM
DIFF_W = DIFF_HEADS * 2 * HEAD_DIM
ROT_DIM = HEAD_DIM // 4
ROPE_THETA = 500000.0
CHUNK = 64
MEM_LEN = 256
MEM_HEADS = 4
MEM_HEAD_DIM = D_MODEL // MEM_HEADS
N_GROUPS = 4
EXPERTS_PER_GROUP = 8
N_EXPERTS = N_GROUPS * EXPERTS_PER_GROUP
D_EXPERT = 512
EPS = 1e-6

LANES = 128
LOG2E = 1.4426950408889634
NEG = -1e30
VMEM_LIMIT_BYTES = 56 * 1024 * 1024
MOE_ROWS = 256
TOKEN_TILE = 512
KEY_TILE = 512
PROMPT_MOE_PARTS = 2
GROUP_LANE0 = N_EXPERTS


def _params(*sem):
    return pltpu.CompilerParams(dimension_semantics=sem, vmem_limit_bytes=VMEM_LIMIT_BYTES)


def _rms(x, g):
    return (x * lax.rsqrt(jnp.mean(x * x, axis=-1, keepdims=True) + EPS)) * g


def _rope(x, cos, sin_lo, sin_hi):
    outs = []
    for c in range(x.shape[1] // LANES):
        xc = x[:, c * LANES:(c + 1) * LANES]
        up = pltpu.roll(xc, LANES - ROT_DIM // 2, axis=1)
        dn = pltpu.roll(xc, ROT_DIM // 2, axis=1)
        outs.append(xc * cos + up * sin_lo + dn * sin_hi)
    return jnp.concatenate(outs, axis=1)


def _inproj_kernel(x_ref, g_ref, w_ref, wf_ref, bf_ref, cos_ref, slo_ref, shi_ref,
                   fk_ref, fv_ref, logf_ref, dk_ref, dv_ref,
                   qf_ref, kf_ref, vf_ref, qd_ref, kd_ref, vd_ref, *, q_scale, time_major):
    n = _rms(x_ref[...], g_ref[...]).astype(BF16)
    turn = (lambda a: a.T) if time_major else (lambda a: a)

    def proj(c):
        return jnp.dot(n, w_ref[:, c * FOX_W:(c + 1) * FOX_W], preferred_element_type=F32)

    cos, slo, shi = cos_ref[...], slo_ref[...], shi_ref[...]
    qf_ref[...] = (proj(0) * q_scale).astype(BF16)
    fk = turn(proj(1))
    fk_ref[...] = fk
    kf_ref[...] = fk.astype(BF16)
    fv = proj(2)
    fv_ref[...] = turn(fv)
    vf_ref[...] = fv.astype(BF16)
    qd_ref[...] = (_rope(proj(3), cos, slo, shi) * q_scale).astype(BF16)
    dk = turn(_rope(proj(4), cos, slo, shi))
    dk_ref[...] = dk
    kd_ref[...] = dk.astype(BF16)
    dv = proj(5)
    for h in range(DIFF_HEADS):
        dv_ref[pl.ds(h, dv.shape[0], stride=DIFF_HEADS), :] = dv[:, h * LANES:(h + 1) * LANES]
    vd_ref[...] = dv.astype(BF16)
    z = jnp.dot(n, wf_ref[...], preferred_element_type=F32) + bf_ref[...]
    logf = jnp.minimum(z, 0.0) - jnp.log1p(jnp.exp(-jnp.abs(z)))
    logf_ref[...] = logf.T[:FOX_HEADS] if time_major else logf[:, :FOX_HEADS]


def _rope_tables(pos):
    half = ROT_DIM // 2
    inv = jnp.power(ROPE_THETA, -jnp.arange(half, dtype=F32) / half)
    ang = pos.astype(F32)[:, None] * inv[None, :]
    cos, sin = jnp.cos(ang), jnp.sin(ang)
    t = pos.shape[0]
    pad = jnp.zeros((t, HEAD_DIM - ROT_DIM), F32)
    zero = jnp.zeros((t, half), F32)
    cos64 = jnp.concatenate([cos, cos, pad + 1.0], axis=1)
    slo64 = jnp.concatenate([-sin, zero, pad], axis=1)
    shi64 = jnp.concatenate([zero, sin, pad], axis=1)
    return tuple(jnp.tile(a, (1, LANES // HEAD_DIM)) for a in (cos64, slo64, shi64))


def _inproj(x, g, w_main, w_ff, b_ff, pos, tm):
    b, t, d = x.shape
    n = b * t
    tables = _rope_tables(pos)
    time_major = tm <= t
    if time_major:
        assert t % tm == 0
        per = t // tm
        tab_map = lambda i: (i % per, 0)
        turned = pl.BlockSpec((None, FOX_W, tm), lambda i: (i // per, 0, i % per))
        gate = pl.BlockSpec((None, FOX_HEADS, tm), lambda i: (i // per, 0, i % per))
        turned_shape, gate_shape = (b, FOX_W, t), (b, FOX_HEADS, t)
    else:
        assert tm % t == 0
        tables = tuple(jnp.tile(a, (tm // t, 1)) for a in tables)
        tab_map = lambda i: (0, 0)
    assert n % tm == 0
    row = lambda i: (i, 0)
    const = lambda i: (0, 0)
    wide = pl.BlockSpec((tm, FOX_W), row)
    if not time_major:
        turned, gate = wide, pl.BlockSpec((tm, FOX_HEADS), row)
        turned_shape, gate_shape = (n, FOX_W), (n, FOX_HEADS)
    tab = pl.BlockSpec((tm, LANES), tab_map)
    sds = jax.ShapeDtypeStruct
    return pl.pallas_call(
        functools.partial(_inproj_kernel, q_scale=HEAD_DIM ** -0.5 * LOG2E, time_major=time_major),
        grid=(n // tm,),
        in_specs=[pl.BlockSpec((tm, d), row), pl.BlockSpec((1, d), const),
                  pl.BlockSpec(w_main.shape, const), pl.BlockSpec(w_ff.shape, const),
                  pl.BlockSpec((1, LANES), const), tab, tab, tab],
        out_specs=[turned, turned, gate, turned, pl.BlockSpec((tm * DIFF_HEADS, LANES), row),
                   wide, turned, wide, wide, turned, wide],
        out_shape=[sds(turned_shape, F32), sds(turned_shape, F32), sds(gate_shape, F32),
                   sds(turned_shape, F32), sds((n * DIFF_HEADS, LANES), F32),
                   sds((n, FOX_W), BF16), sds(turned_shape, BF16), sds((n, FOX_W), BF16),
                   sds((n, FOX_W), BF16), sds(turned_shape, BF16), sds((n, FOX_W), BF16)],
        compiler_params=_params("arbitrary"),
    )(x.reshape(n, d), g.reshape(1, d), w_main, w_ff,
      jnp.pad(b_ff.reshape(1, FOX_HEADS), ((0, 0), (0, LANES - FOX_HEADS))), *tables)


AUX_LANES = 8


def _bf16_part(x):
    return pltpu.bitcast(pltpu.bitcast(x, jnp.uint32) & jnp.uint32(0xFFFF0000), F32)


def _cumsum_kernel(x_ref, kb_ref, qb_ref):
    x = x_ref[...]
    s_len = x.shape[1]
    lane = lax.broadcasted_iota(I32, x.shape, 1)
    shift = 1
    while shift < s_len:
        x = x + jnp.where(lane >= shift, pltpu.roll(x, shift, axis=1), 0.0)
        shift *= 2
    c = x * LOG2E
    hi = _bf16_part(c)
    mid = _bf16_part(c - hi)
    lo = _bf16_part(c - hi - mid)
    slot = lax.broadcasted_iota(I32, (LANES, x.shape[0]), 0)
    head = lax.broadcasted_iota(I32, (LANES, x.shape[0]), 1)

    def place(first_slot):
        out = jnp.zeros((LANES, s_len), F32)
        for i, part in enumerate((hi, mid, lo)):
            sel = jnp.where(slot == head * AUX_LANES + first_slot + i, 1.0, 0.0).astype(BF16)
            out = out + jnp.dot(sel, part.astype(BF16), preferred_element_type=F32)
        return out

    slot_col = lax.broadcasted_iota(I32, (LANES, s_len), 0)
    used = slot_col < FOX_HEADS * AUX_LANES
    within = slot_col % AUX_LANES
    kb_ref[...] = (jnp.where(used & (within < 3), 1.0, 0.0) - place(3)).astype(BF16)
    qb_ref[...] = (jnp.where(used & (within >= 3) & (within < 6), 1.0, 0.0) + place(0)).astype(BF16)


def _fox_bias(logf_t):
    b, h, s = logf_t.shape
    x = jnp.pad(logf_t, ((0, 0), (0, 16 - h), (0, 0)))
    out = pl.BlockSpec((None, LANES, s), lambda i: (i, 0, 0))
    return pl.pallas_call(
        _cumsum_kernel, grid=(b,), in_specs=[pl.BlockSpec((None, 16, s), lambda i: (i, 0, 0))],
        out_specs=[out, out],
        out_shape=[jax.ShapeDtypeStruct((b, LANES, s), BF16)] * 2,
        compiler_params=_params("arbitrary"),
    )(x)


def _attn_kernel(*refs, tq, tk, past, s_valid, fox, lam_init):
    if fox:
        q_ref, qa_ref, k_ref, ka_ref, v_ref, o_ref, sa_ref, sb_ref, mx_ref, m_ref, l_ref, acc_ref = refs
    else:
        q_ref, k_ref, v_ref, lam_ref, g_ref, o_ref, sa_ref, sb_ref, mx_ref, m_ref, l_ref, acc_ref = refs
    i = pl.program_id(2)
    q = q_ref[...]
    lane = lax.broadcasted_iota(I32, (tq, LANES), 1)
    low = lane < HEAD_DIM
    zero = jnp.zeros_like(q)
    q_part = [jnp.where(low, q, zero), jnp.where(low, zero, q)]
    if fox:
        qa = qa_ref[...]
        for h in range(2):
            lane0 = (2 * pl.program_id(1) + h) * AUX_LANES
            aux = jnp.where((lane >= lane0) & (lane < lane0 + AUX_LANES), qa, zero)
            q_part[h] = jnp.concatenate([q_part[h], aux], axis=1)
    m_ref[...] = jnp.full(m_ref.shape, NEG, F32)
    l_ref[...] = jnp.zeros(l_ref.shape, F32)
    acc_ref[...] = jnp.zeros(acc_ref.shape, F32)
    q0 = past + i * tq
    nb = q0 // tk
    bufs = (sa_ref, sb_ref)

    def scores(j, buf):
        ks = pl.multiple_of(j * tk, tk)
        k = k_ref[:, pl.ds(ks, tk)]
        if fox:
            k = jnp.concatenate([k, ka_ref[:, pl.ds(ks, tk)]], axis=0)
        for h in range(2):
            s = jnp.dot(q_part[h], k, preferred_element_type=F32)
            bufs[buf][h] = s
            mx_ref[buf, h] = jnp.broadcast_to(jnp.max(s, axis=1, keepdims=True), (tq, LANES))

    def consume(j, buf, use_mask):
        ks = pl.multiple_of(j * tk, tk)
        v = jnp.concatenate([v_ref[pl.ds(ks, tk), :], jnp.ones((tk, LANES), BF16)], axis=1)
        pv, alpha = [], []
        for h in range(2):
            t = bufs[buf][h]
            if use_mask:
                qpos = q0 + lax.broadcasted_iota(I32, (tq, tk), 0)
                kpos = j * tk + lax.broadcasted_iota(I32, (tq, tk), 1)
                if fox:
                    seen = kpos <= qpos
                else:
                    seen = kpos < jnp.minimum((qpos // CHUNK + 1) * CHUNK, s_valid)
                t = jnp.where(seen, t, NEG)
                mx = jnp.max(t, axis=1, keepdims=True)
            else:
                mx = mx_ref[buf, h]
            m_prev = m_ref[h]
            m_new = jnp.maximum(m_prev, mx)
            a = jnp.exp2(m_prev - m_new)
            p = jnp.exp2(t - jnp.concatenate([m_new] * (tk // LANES), axis=1))
            m_ref[h] = m_new
            both = jnp.dot(p.astype(BF16), v, preferred_element_type=F32)
            pv.append(both[:, :LANES])
            l_ref[h] = a * l_ref[h] + both[:, LANES:]
            alpha.append(a)
        if fox:
            acc_ref[0] = acc_ref[0] * jnp.where(low, alpha[0], alpha[1]) + jnp.where(low, pv[0], pv[1])
        else:
            for h in range(2):
                acc_ref[h] = acc_ref[h] * alpha[h] + pv[h]

    scores(0, 0)

    def pair(jj, c):
        j = 2 * jj
        scores(j + 1, 1)
        consume(j, 0, False)
        scores(j + 2, 0)
        consume(j + 1, 1, False)
        return c

    lax.fori_loop(0, nb // 2, pair, 0)

    @pl.when(nb % 2 == 1)
    def _():
        scores(nb, 1)
        consume(nb - 1, 0, False)
        consume(nb, 1, True)

    @pl.when(nb % 2 == 0)
    def _():
        consume(nb, 0, True)

    if fox:
        o = acc_ref[0] / jnp.where(low, l_ref[0], l_ref[1])
    else:
        lam_v = lam_ref[...]
        lam = (jnp.exp(jnp.sum(lam_v[0:1] * lam_v[1:2], axis=1, keepdims=True))
               - jnp.exp(jnp.sum(lam_v[2:3] * lam_v[3:4], axis=1, keepdims=True)) + lam_init)
        o = acc_ref[0] / l_ref[0] - lam * (acc_ref[1] / l_ref[1])
        o = _rms(o, g_ref[...]) * (1.0 - lam_init)
    o_ref[...] = o.astype(o_ref.dtype)


def _self_attention(q, k, v, extra, *, b, t, s_pad, s_valid, past, tq, tk, fox, lam_init):
    assert tk % tq == 0 and past % tq == 0 and t % tq == 0 and tk % CHUNK == 0 and s_pad % tk == 0
    assert tq % CHUNK == 0 or (CHUNK % tq == 0 and past % CHUNK == 0)
    n_q = t // tq
    k_spec = pl.BlockSpec((None, LANES, s_pad), lambda bi, p, i: (bi, p, 0))
    v_spec = pl.BlockSpec((None, s_pad, LANES), lambda bi, p, i: (bi, 0, p))
    q_spec = pl.BlockSpec((tq, LANES), lambda bi, p, i: (bi * n_q + i, p))
    const = lambda bi, p, i: (0, 0)
    if fox:
        in_specs = [q_spec, pl.BlockSpec((tq, LANES), lambda bi, p, i: (bi * n_q + i, 0)), k_spec,
                    pl.BlockSpec((None, LANES, s_pad), lambda bi, p, i: (bi, 0, 0)), v_spec]
        args = (q, extra[0], k, extra[1], v)
    else:
        in_specs = [q_spec, k_spec, v_spec, pl.BlockSpec((4, HEAD_DIM), const), pl.BlockSpec((1, LANES), const)]
        args = (q, k, v, extra[0], extra[1].reshape(1, LANES))
    stat = pltpu.VMEM((2, tq, LANES), F32)
    return pl.pallas_call(
        functools.partial(_attn_kernel, tq=tq, tk=tk, past=past, s_valid=s_valid, fox=fox, lam_init=lam_init),
        grid=(b, FOX_W // LANES, n_q),
        in_specs=in_specs,
        out_specs=q_spec,
        out_shape=jax.ShapeDtypeStruct((b * t, FOX_W), BF16),
        scratch_shapes=[pltpu.VMEM((2, tq, tk), F32), pltpu.VMEM((2, tq, tk), F32),
                        pltpu.VMEM((2, 2, tq, LANES), F32), stat, stat, stat],
        compiler_params=_params("arbitrary", "arbitrary", "arbitrary"),
    )(*args)


def _outproj_kernel(x_ref, fo_ref, do_ref, wof_ref, wod_ref, g_ref, wq_ref, h_ref, q_ref, *, q_scale):
    h = (x_ref[...] + jnp.dot(fo_ref[...], wof_ref[...], preferred_element_type=F32)
         + jnp.dot(do_ref[...], wod_ref[...], preferred_element_type=F32))
    h_ref[...] = h
    n = _rms(h, g_ref[...]).astype(BF16)
    q_ref[...] = (jnp.dot(n, wq_ref[...], preferred_element_type=F32) * q_scale).astype(BF16)


def _outproj(x2, fox_o, diff_o, wo_f, wo_d, g_mem_q, w_mq, tm):
    n, d = x2.shape
    row = lambda i: (i, 0)
    const = lambda i: (0, 0)
    return pl.pallas_call(
        functools.partial(_outproj_kernel, q_scale=MEM_HEAD_DIM ** -0.5 * LOG2E),
        grid=(n // tm,),
        in_specs=[pl.BlockSpec((tm, d), row), pl.BlockSpec((tm, FOX_W), row),
                  pl.BlockSpec((tm, DIFF_W), row), pl.BlockSpec((FOX_W, d), const),
                  pl.BlockSpec((DIFF_W, d), const), pl.BlockSpec((1, d), const),
                  pl.BlockSpec((d, d), const)],
        out_specs=[pl.BlockSpec((tm, d), row), pl.BlockSpec((tm, d), row)],
        out_shape=[jax.ShapeDtypeStruct((n, d), F32), jax.ShapeDtypeStruct((n, d), BF16)],
        compiler_params=_params("arbitrary"),
    )(x2, fox_o, diff_o, wo_f, wo_d, g_mem_q.reshape(1, d), w_mq)


MEM_HALVES = MEM_HEAD_DIM // LANES


def _store_head_rows(ref, val):
    rows = val.shape[0]
    per_token = MEM_HALVES * MEM_HEADS
    for c in range(MEM_HALVES):
        for h in range(MEM_HEADS):
            lane0 = h * MEM_HEAD_DIM + c * LANES
            ref[pl.ds(c * MEM_HEADS + h, rows, stride=per_token), :] = val[:, lane0:lane0 + LANES]


def _memkv_kernel(m_ref, g_ref, wk_ref, wv_ref, k_ref, v_ref, kb_ref, vb_ref):
    n = _rms(m_ref[...], g_ref[...]).astype(BF16)
    k = jnp.dot(n, wk_ref[...], preferred_element_type=F32)
    v = jnp.dot(n, wv_ref[...], preferred_element_type=F32)
    _store_head_rows(k_ref, k)
    _store_head_rows(v_ref, v)
    kb_ref[...] = k.astype(BF16)
    vb_ref[...] = v.astype(BF16)


def _memkv(mem2, g, wk, wv, tm):
    n, d = mem2.shape
    row = lambda i: (i, 0)
    const = lambda i: (0, 0)
    blk = pl.BlockSpec((tm, d), row)
    wspec = pl.BlockSpec((d, d), const)
    return pl.pallas_call(
        _memkv_kernel, grid=(n // tm,),
        in_specs=[blk, pl.BlockSpec((1, d), const), wspec, wspec],
        out_specs=[pl.BlockSpec((tm * MEM_HALVES * MEM_HEADS, LANES), row)] * 2 + [blk, blk],
        out_shape=[jax.ShapeDtypeStruct((n * MEM_HALVES * MEM_HEADS, LANES), F32)] * 2
        + [jax.ShapeDtypeStruct((n, d), BF16)] * 2,
        compiler_params=_params("arbitrary"),
    )(mem2, g.reshape(1, d), wk, wv)


U32 = jnp.uint32
HALF_D = D_MODEL // 2
ROW_TILE = HALF_D // LANES
BF16_BITS = 0xFFFF0000


def _pack_rows(val):
    def rounded(x):
        bits = pltpu.bitcast(x, U32)
        return (bits + U32(0x7FFF) + ((bits >> 16) & U32(1))) & U32(BF16_BITS)
    return rounded(val[:, :HALF_D]) | (rounded(val[:, HALF_D:]) >> 16)


def _unpack_rows(words):
    return jnp.concatenate([pltpu.bitcast(words & U32(BF16_BITS), F32), pltpu.bitcast(words << 16, F32)], axis=1)


def _load_tile_rows(ref):
    rows = ref.shape[0] // ROW_TILE
    return jnp.concatenate([ref[pl.ds(s, rows, stride=ROW_TILE), :] for s in range(ROW_TILE)], axis=1)


def _store_tile_rows(ref, words):
    rows = words.shape[0]
    for s in range(ROW_TILE):
        ref[pl.ds(s, rows, stride=ROW_TILE), :] = words[:, s * LANES:(s + 1) * LANES]


def _memattn_kernel(q_ref, k_ref, v_ref, h_ref, wo_ref, g_ref, wr_ref, br_ref,
                    h2_ref, xn_ref, logit_ref):
    q = q_ref[...]
    outs = []
    for h in range(MEM_HEADS):
        sl = slice(h * MEM_HEAD_DIM, (h + 1) * MEM_HEAD_DIM)
        s = lax.dot_general(q[:, sl], k_ref[:, sl], (((1,), (1,)), ((), ())), preferred_element_type=F32)
        p = jnp.exp2(s - jnp.max(s, axis=1, keepdims=True))
        o = jnp.dot(p.astype(BF16), v_ref[:, sl], preferred_element_type=F32)
        outs.append((o / jnp.sum(p, axis=1, keepdims=True)).astype(BF16))
    h2 = h_ref[...] + jnp.dot(jnp.concatenate(outs, axis=1), wo_ref[...], preferred_element_type=F32)
    h2_ref[...] = h2
    xn_f32 = _rms(h2, g_ref[...])
    words = _pack_rows(xn_f32)
    for s in range(ROW_TILE):
        xn_ref[s] = words[:, s * LANES:(s + 1) * LANES]
    logit_ref[...] = jnp.dot(xn_f32.astype(BF16), wr_ref[...], preferred_element_type=F32) + br_ref[...]


def _memattn(qm, mk, mv, h1, w_mo, g_ffn, w_router, b_router, *, b0, b, t, tq):
    d = h1.shape[1]
    n = b * t
    n_q = t // tq
    row = lambda bi, i: (bi * n_q + i, 0)
    src_row = lambda bi, i: ((b0 + bi) * n_q + i, 0)
    const = lambda bi, i: (0, 0)
    mem = pl.BlockSpec((None, MEM_LEN, d), lambda bi, i: (b0 + bi, 0, 0))
    src = pl.BlockSpec((tq, d), src_row)
    blk = pl.BlockSpec((tq, d), row)
    planes = pl.BlockSpec((ROW_TILE, tq, LANES), lambda bi, i: (0, bi * n_q + i, 0))
    return pl.pallas_call(
        _memattn_kernel,
        grid=(b, n_q),
        in_specs=[src, mem, mem, src, pl.BlockSpec((d, d), const), pl.BlockSpec((1, d), const),
                  pl.BlockSpec((d, LANES), const), pl.BlockSpec((1, LANES), const)],
        out_specs=[blk, planes, pl.BlockSpec((tq, LANES), row)],
        out_shape=[jax.ShapeDtypeStruct((n, d), F32), jax.ShapeDtypeStruct((ROW_TILE, n, LANES), U32),
                   jax.ShapeDtypeStruct((n, LANES), F32)],
        compiler_params=_params("arbitrary", "arbitrary"),
    )(qm, mk, mv, h1, w_mo, g_ffn.reshape(1, d), w_router, b_router)


ROUTER_ROWS = 48


def _first_slot(hit, slot_f32):
    return jnp.min(jnp.where(hit, slot_f32, float(LANES)), axis=0, keepdims=True).astype(I32)


def _router_kernel(lt_ref, ri_ref, rg_ref, cnt_ref, carry_ref, *, rt):
    @pl.when(pl.program_id(0) == 0)
    def _():
        carry_ref[...] = jnp.zeros(carry_ref.shape, F32)

    lt = lt_ref[0:ROUTER_ROWS, :]
    slot = lax.broadcasted_iota(I32, (ROUTER_ROWS, rt), 0)
    slot_f = slot.astype(F32)
    g_log = jnp.where((slot >= GROUP_LANE0) & (slot < GROUP_LANE0 + N_GROUPS), lt, NEG)
    g_max = jnp.max(g_log, axis=0, keepdims=True)
    g_idx = _first_slot(g_log == g_max, slot_f) - GROUP_LANE0
    g_p = 1.0 / jnp.sum(jnp.exp(g_log - g_max), axis=0, keepdims=True)
    in_group = (slot >= g_idx * EXPERTS_PER_GROUP) & (slot < (g_idx + 1) * EXPERTS_PER_GROUP)
    e_log = jnp.where(in_group, lt, NEG)
    e_exp = jnp.exp(e_log - jnp.max(e_log, axis=0, keepdims=True))
    prob = jnp.where(in_group, e_exp / jnp.sum(e_exp, axis=0, keepdims=True), -1.0)
    p1 = jnp.max(prob, axis=0, keepdims=True)
    i1 = _first_slot(prob == p1, slot_f)
    rest = jnp.where(slot == i1, -1.0, prob)
    p2 = jnp.max(rest, axis=0, keepdims=True)
    i2 = _first_slot(rest == p2, slot_f)
    top_sum = p1 + p2

    hot1, hot2 = slot == i1, slot == i2
    onehot = jnp.where(hot1 | hot2, 1.0, 0.0)
    earlier = lax.broadcasted_iota(I32, (rt, rt), 0) < lax.broadcasted_iota(I32, (rt, rt), 1)
    before = jnp.dot(onehot.astype(BF16), jnp.where(earlier, 1.0, 0.0).astype(BF16),
                     preferred_element_type=F32) + carry_ref[0:ROUTER_ROWS, 0:1]
    r1 = jnp.sum(jnp.where(hot1, before, 0.0), axis=0, keepdims=True)
    r2 = jnp.sum(jnp.where(hot2, before, 0.0), axis=0, keepdims=True)
    carry_ref[0:ROUTER_ROWS, :] = carry_ref[0:ROUTER_ROWS, :] + jnp.sum(onehot, axis=1, keepdims=True)
    cnt_ref[...] = carry_ref[...]

    ri_ref[...] = jnp.concatenate([i1, i2, r1.astype(I32), r2.astype(I32), jnp.zeros((4, rt), I32)], axis=0)
    rg_ref[...] = jnp.concatenate([g_p * p1 / top_sum, g_p * p2 / top_sum, jnp.zeros((6, rt), F32)], axis=0)


def _router(logits_t, rt):
    n = logits_t.shape[1]
    col = lambda i: (0, i)
    return pl.pallas_call(
        functools.partial(_router_kernel, rt=rt),
        grid=(n // rt,),
        in_specs=[pl.BlockSpec((LANES, rt), col)],
        out_specs=[pl.BlockSpec((8, rt), col), pl.BlockSpec((8, rt), col),
                   pl.BlockSpec((LANES, LANES), lambda i: (0, 0))],
        out_shape=[jax.ShapeDtypeStruct((8, n), I32), jax.ShapeDtypeStruct((8, n), F32),
                   jax.ShapeDtypeStruct((LANES, LANES), F32)],
        scratch_shapes=[pltpu.VMEM((LANES, LANES), F32)],
        compiler_params=_params("arbitrary"),
    )(logits_t)


SC_CORES, SC_SUBCORES = 2, 16
SC_WINDOW = 128


def _sc_mesh():
    return plsc.VectorSubcoreMesh(core_axis_name="core", subcore_axis_name="subcore")


def _sc_split(n_windows):
    if n_windows % (SC_CORES * SC_SUBCORES) == 0:
        return ("core", "subcore")
    assert n_windows % SC_SUBCORES == 0
    return "subcore"


def _sc_repeat(idx):
    n_windows = idx.shape[0] // SC_WINDOW
    reps = SC_SUBCORES // math.gcd(n_windows, SC_SUBCORES)
    return jnp.tile(idx, reps)


def _sc_scatter_rows(src, idx, n_out):
    rows = src.shape[0]
    win = SC_WINDOW
    n_src_blocks = rows // win
    idx = _sc_repeat(idx)
    n_idx = idx.shape[0]

    @functools.partial(pl.kernel, out_type=jax.ShapeDtypeStruct((n_out, LANES), src.dtype),
                       mesh=_sc_mesh(), scratch_types=[])
    def scatter(x_hbm, i_hbm, o_hbm):
        def body(x_vmem, i_vmem):
            pltpu.sync_copy(x_vmem, o_hbm.at[i_vmem.at[0]])

        pltpu.emit_pipeline(
            body, grid=(n_idx // win,),
            in_specs=[pl.BlockSpec((win, LANES), lambda i: (i % n_src_blocks, 0)),
                      pl.BlockSpec((1, win), lambda i: (0, i))],
            out_specs=[],
            core_axis_name=_sc_split(n_idx // win), dimension_semantics=(pltpu.PARALLEL,),
        )(x_hbm, i_hbm)

    return scatter(src, idx.reshape(1, n_idx))


def _sc_gather_rows(src, idx):
    n_rows = idx.shape[0]
    idx = _sc_repeat(idx)
    n_out = idx.shape[0]
    win = SC_WINDOW

    @functools.partial(pl.kernel, out_type=jax.ShapeDtypeStruct((n_out, LANES), src.dtype),
                       mesh=_sc_mesh(), scratch_types=[])
    def gather(x_hbm, i_hbm, o_hbm):
        def body(i_vmem, o_vmem):
            pltpu.sync_copy(x_hbm.at[i_vmem.at[0]], o_vmem)

        pltpu.emit_pipeline(
            body, grid=(n_out // win,),
            in_specs=[pl.BlockSpec((1, win), lambda i: (0, i))],
            out_specs=[pl.BlockSpec((win, LANES), lambda i: (i, 0))],
            core_axis_name=_sc_split(n_out // win), dimension_semantics=(pltpu.PARALLEL,),
        )(i_hbm, o_hbm)

    return gather(src, idx.reshape(1, n_out))[:n_rows]


def _expert_kernel(blk_ref, e_ref, lo_ref, hi_ref, x_ref, w1_ref, w3_ref, w2_ref, y_ref):
    w = pl.program_id(0)
    lo, hi = lo_ref[w], hi_ref[w]
    row0 = blk_ref[w] * MOE_ROWS

    def compute():
        x = _unpack_rows(_load_tile_rows(x_ref)).astype(BF16)
        a = jnp.dot(x, w1_ref[...], preferred_element_type=F32)
        g = jnp.dot(x, w3_ref[...], preferred_element_type=F32)
        mid = ((a / (1.0 + jnp.exp(-a))) * g).astype(BF16)
        y = _pack_rows(jnp.dot(mid, w2_ref[...], preferred_element_type=F32))
        rows = row0 + lax.broadcasted_iota(I32, y.shape, 0)
        return y, (rows >= lo) & (rows < hi)

    @pl.when((hi > lo) & (lo == row0))
    def _():
        y, mine = compute()
        _store_tile_rows(y_ref, jnp.where(mine, y, jnp.zeros_like(y)))

    @pl.when((hi > lo) & (lo != row0))
    def _():
        y, mine = compute()
        _store_tile_rows(y_ref, jnp.where(mine, y, _load_tile_rows(y_ref)))


def _experts(items, xs, w1, w3, w2):
    d = D_MODEL
    n_items = items[0].shape[0]
    xmap = lambda w, blk, e, lo, hi: (blk[w], 0)
    wmap = lambda w, blk, e, lo, hi: (e[w], 0, 0)
    return pl.pallas_call(
        _expert_kernel,
        grid_spec=pltpu.PrefetchScalarGridSpec(
            num_scalar_prefetch=4, grid=(n_items,),
            in_specs=[pl.BlockSpec((MOE_ROWS * ROW_TILE, LANES), xmap),
                      pl.BlockSpec((None, d, D_EXPERT), wmap), pl.BlockSpec((None, d, D_EXPERT), wmap),
                      pl.BlockSpec((None, D_EXPERT, d), wmap)],
            out_specs=pl.BlockSpec((MOE_ROWS * ROW_TILE, LANES), xmap)),
        out_shape=jax.ShapeDtypeStruct(xs.shape, U32),
        compiler_params=_params("arbitrary"),
    )(*items, xs, w1, w3, w2)


def _work_items(counts, n_rows):
    n_blocks = n_rows // MOE_ROWS
    n_items = n_blocks + N_EXPERTS - 1
    ends = jnp.cumsum(counts)
    starts = ends - counts
    first_blk = starts // MOE_ROWS
    n_blk = jnp.where(counts > 0, (ends - 1) // MOE_ROWS - first_blk + 1, 0)
    item_end = jnp.cumsum(n_blk)
    total = item_end[-1]
    w = jnp.arange(n_items, dtype=I32)
    wc = jnp.minimum(w, total - 1)
    e = jnp.sum(item_end[None, :] <= wc[:, None], axis=1).astype(I32)
    mine = e[:, None] == jnp.arange(N_EXPERTS, dtype=I32)[None, :]

    def of_expert(per_expert):
        return jnp.sum(jnp.where(mine, per_expert[None, :], 0), axis=1)

    blk = (of_expert(first_blk) + wc - (of_expert(item_end) - of_expert(n_blk))).astype(I32)
    lo = jnp.maximum(of_expert(starts), blk * MOE_ROWS).astype(I32)
    hi = jnp.minimum(of_expert(ends), (blk + 1) * MOE_ROWS).astype(I32)
    live = w < total
    return blk, e, jnp.where(live, lo, 0), jnp.where(live, hi, 0), starts


def _combine_kernel(h_ref, z0_ref, z1_ref, rg_ref, g_ref, *rest):
    y_ref = rest[-1]
    rg = rg_ref[...]
    def rows_of(z_ref):
        return _unpack_rows(jnp.concatenate([z_ref[s] for s in range(ROW_TILE)], axis=1))

    moe = rows_of(z0_ref) * rg[:, 0:1] + rows_of(z1_ref) * rg[:, 1:2]
    y_ref[...] = _rms(h_ref[...] + moe, g_ref[...])


def _combine(h2, z, rg, g_final, tm, y_prev, row0, n_total):
    n, d = h2.shape
    n_t = n // tm
    row = lambda i: (i, 0)
    in_specs = [pl.BlockSpec((tm, d), row), pl.BlockSpec((ROW_TILE, tm, LANES), lambda i: (0, i, 0)),
                pl.BlockSpec((ROW_TILE, tm, LANES), lambda i: (1, i, 0)), pl.BlockSpec((tm, 8), row),
                pl.BlockSpec((1, d), lambda i: (0, 0))]
    args = [h2, z, z, rg, g_final.reshape(1, d)]
    aliases = {}
    if y_prev is not None:
        in_specs.append(pl.BlockSpec(memory_space=pl.ANY))
        args.append(y_prev)
        aliases = {len(args) - 1: 0}
    return pl.pallas_call(
        _combine_kernel, grid=(n_t,),
        in_specs=in_specs,
        out_specs=pl.BlockSpec((tm, d), lambda i: (row0 // tm + i, 0)),
        out_shape=jax.ShapeDtypeStruct((n_total, d), F32),
        input_output_aliases=aliases,
        compiler_params=_params("arbitrary"),
    )(*args)


def _moe_and_final(h2, xn, ri, rg, counts, w1, w3, w2, g_final, *, tm, y_prev, row0, n_total):
    n = h2.shape[0]
    blk, e, lo, hi, starts = _work_items(counts, 2 * n)
    expert = jnp.arange(N_EXPERTS, dtype=I32)[:, None, None]
    first = jnp.sum(jnp.where(ri[None, 0:2] == expert, starts.astype(I32)[:, None, None], 0), axis=0)
    pos = (first + ri[2:4]).astype(I32)
    sub = (pos[:, None, :] * ROW_TILE + jnp.arange(ROW_TILE, dtype=I32)[None, :, None]).reshape(2 * n * ROW_TILE)
    xs = _sc_scatter_rows(xn.reshape(ROW_TILE * n, LANES), sub, 2 * n * ROW_TILE)
    ys = _experts((blk, e, lo, hi), xs, w1, w3, w2)
    z = _sc_gather_rows(ys, sub).reshape(2 * ROW_TILE, n, LANES)
    return _combine(h2, z, rg.T, g_final, tm, y_prev, row0, n_total)


def _pad_time(a, s_pad):
    return jnp.pad(a, ((0, 0), (0, s_pad - a.shape[1])) + ((0, 0),) * (a.ndim - 2))


def _layer(x, past, mem_k, mem_v, wts, lam_init, g_final, *, tm, tq, tk, moe_parts):
    b, t, d = x.shape
    n = b * t
    p_len = 0 if past is None else past[0].shape[1]
    s_valid = p_len + t
    s_pad = -(-s_valid // tk) * tk
    pos = p_len + jnp.arange(t)

    fk, fv, logf, dk, dv, qf, kf, vf, qd, kd, vd = _inproj(
        x, wts['g_mix'], wts['w_main'], wts['w_ff'], wts['b_forget'], pos, tm)
    time_major = tm <= t

    def leaf(a, dims):
        if time_major:
            return jnp.moveaxis(a.reshape((b,) + dims + (t,)), -1, 1)
        return a.reshape((b, t) + dims)

    new = (leaf(fk, (FOX_HEADS, HEAD_DIM)), leaf(fv, (FOX_HEADS, HEAD_DIM)), leaf(logf, (FOX_HEADS,)),
           leaf(dk, (DIFF_HEADS, 2, HEAD_DIM)), dv.reshape(b, t, DIFF_HEADS, 2 * HEAD_DIM))

    def keys_t(cur, old_rows):
        if not time_major:
            cur = jnp.swapaxes(cur.reshape(b, t, FOX_W), 1, 2)
        if old_rows is not None:
            prev = jnp.swapaxes(old_rows.reshape(b, p_len, FOX_W).astype(BF16), 1, 2)
            cur = jnp.concatenate([prev, cur], axis=2)
        return jnp.pad(cur, ((0, 0), (0, 0), (0, s_pad - s_valid)))

    def with_past(new_b16, old_rows):
        cur = new_b16.reshape(b, t, FOX_W)
        if old_rows is not None:
            cur = jnp.concatenate([old_rows.reshape(b, p_len, FOX_W).astype(BF16), cur], axis=1)
        return _pad_time(cur, s_pad)

    logf_t = logf if time_major else jnp.swapaxes(logf.reshape(b, t, FOX_HEADS), 1, 2)
    if past is not None:
        logf_t = jnp.concatenate([jnp.swapaxes(past[2].astype(F32), 1, 2), logf_t], axis=2)
    k_bias, q_bias_t = _fox_bias(jnp.pad(logf_t, ((0, 0), (0, 0), (0, s_pad - s_valid))))
    q_bias = jnp.swapaxes(q_bias_t[:, :, p_len:p_len + t], 1, 2).reshape(n, LANES)

    old = (None,) * 5 if past is None else past
    geom = dict(b=b, t=t, s_pad=s_pad, s_valid=s_valid, past=p_len, tq=tq, tk=tk, lam_init=lam_init)
    fox_o = _self_attention(qf, keys_t(kf, old[0]), with_past(vf, old[1]), (q_bias, k_bias),
                            fox=True, **geom)
    diff_o = _self_attention(qd, keys_t(kd, old[3]), with_past(vd, old[4]),
                             (wts['lam_vecs'], wts['g_diff']), fox=False, **geom)
    h1, qm = _outproj(x.reshape(n, d), fox_o, diff_o, wts['wo_f'], wts['wo_d'], wts['g_mem_q'],
                      wts['w_mq'], tm)
    assert b % moe_parts == 0
    bp = b // moe_parts
    y = None
    for part in range(moe_parts):
        h2, xn, logits = _memattn(qm, mem_k, mem_v, h1, wts['w_mo'], wts['g_ffn'], wts['w_router'],
                                  wts['b_router'], b0=part * bp, b=bp, t=t, tq=tq)
        ri, rg, cnt = _router(logits.T, min(tm, bp * t))
        counts = cnt[:N_EXPERTS, 0].astype(I32)
        y = _moe_and_final(h2, xn, ri, rg, counts, wts['w1'], wts['w3'], wts['w2'], g_final, tm=tm,
                           y_prev=y, row0=part * bp * t, n_total=n)
    return y.reshape(b, t, d), new


def kernel(x_prompt, x_sample, cache_fox_k, cache_fox_v, cache_fox_logf, cache_diff_k, cache_diff_v, cache_mem_k, cache_mem_v, mem_prompt, g_mix, w_in, b_forget, lam_q1, lam_k1, lam_q2, lam_k2, g_diff, w_o, g_mem_q, g_mem_kv, w_mq, w_mk, w_mv, w_mo, g_ffn, w_group, b_group, w_erouter, b_erouter, w1, w3, w2, g_final):
    depth = w_in.shape[0]
    assert depth == 1
    l = 0
    lam_init = 0.8 - 0.6 * math.exp(-0.3 * l)
    d = D_MODEL
    w = w_in[l]
    ff0 = 3 * FOX_W
    w_router = jnp.zeros((d, LANES), F32).at[:, :N_EXPERTS].set(w_erouter[l])
    w_router = w_router.at[:, GROUP_LANE0:GROUP_LANE0 + N_GROUPS].set(w_group[l])
    b_router = jnp.zeros((1, LANES), F32).at[0, :N_EXPERTS].set(b_erouter[l])
    b_router = b_router.at[0, GROUP_LANE0:GROUP_LANE0 + N_GROUPS].set(b_group[l])
    wts = {
        'g_mix': g_mix[l],
        'w_main': jnp.concatenate([w[:, :ff0], w[:, ff0 + FOX_HEADS:]], axis=1).astype(BF16),
        'w_ff': jnp.pad(w[:, ff0:ff0 + FOX_HEADS], ((0, 0), (0, LANES - FOX_HEADS))).astype(BF16),
        'b_forget': b_forget[l],
        'lam_vecs': jnp.stack([lam_q1[l], lam_k1[l], lam_q2[l], lam_k2[l]]),
        'g_diff': g_diff[l],
        'wo_f': w_o[l][:FOX_W].astype(BF16), 'wo_d': w_o[l][FOX_W:].astype(BF16),
        'g_mem_q': g_mem_q[l], 'w_mq': w_mq[l].astype(BF16), 'w_mo': w_mo[l].astype(BF16),
        'g_ffn': g_ffn[l], 'w_router': w_router.astype(BF16), 'b_router': b_router,
        'w1': w1[l].astype(BF16), 'w3': w3[l].astype(BF16), 'w2': w2[l].astype(BF16),
    }

    bp, tp, _ = x_prompt.shape
    mk, mv, mk_b, mv_b = _memkv(mem_prompt.reshape(bp * MEM_LEN, d), g_mem_kv[l],
                                w_mk[l].astype(BF16), w_mv[l].astype(BF16), TOKEN_TILE)
    yp, new_p = _layer(x_prompt, None, mk_b.reshape(bp, MEM_LEN, d), mv_b.reshape(bp, MEM_LEN, d),
                       wts, lam_init, g_final, tm=TOKEN_TILE, tq=TOKEN_TILE, tk=KEY_TILE,
                       moe_parts=PROMPT_MOE_PARTS)

    bs, ts, _ = x_sample.shape
    past = (cache_fox_k[l], cache_fox_v[l], cache_fox_logf[l], cache_diff_k[l], cache_diff_v[l])
    ys, new_s = _layer(x_sample, past, cache_mem_k[l].reshape(bs, MEM_LEN, d).astype(BF16),
                       cache_mem_v[l].reshape(bs, MEM_LEN, d).astype(BF16),
                       wts, lam_init, g_final, tm=bs * ts, tq=ts, tk=KEY_TILE, moe_parts=1)

    def mem_leaf(a):
        a = a.reshape(bp, MEM_LEN, MEM_HALVES, MEM_HEADS, LANES)
        return jnp.swapaxes(a, 2, 3).reshape(1, bp, MEM_LEN, MEM_HEADS, MEM_HEAD_DIM)

    return (yp, ys) + tuple(a[None] for a in new_p) + (mem_leaf(mk), mem_leaf(mv)) \
        + tuple(a[None] for a in new_s)
```

```python
import functools
import math

import jax
import jax.numpy as jnp
from jax import lax
from jax.experimental import pallas as pl
from jax.experimental.pallas import tpu as pltpu
from jax.experimental.pallas import tpu_sc as plsc

F32 = jnp.float32
BF16 = jnp.bfloat16
I32 = jnp.int32

D_MODEL = 1024
HEAD_DIM = 64
FOX_HEADS = 8
DIFF_HEADS = 4
FOX_W = FOX_HEADS * HEAD_DIM
DIFF_W = DIFF_HEADS * 2 * HEAD_DIM
ROT_DIM = HEAD_DIM // 4
ROPE_THETA = 500000.0
CHUNK = 64
MEM_LEN = 256
MEM_HEADS = 4
MEM_HEAD_DIM = D_MODEL // MEM_HEADS
N_GROUPS = 4
EXPERTS_PER_GROUP = 8
N_EXPERTS = N_GROUPS * EXPERTS_PER_GROUP
D_EXPERT = 512
EPS = 1e-6

LANES = 128
LOG2E = 1.4426950408889634
NEG = -1e30
VMEM_LIMIT_BYTES = 56 * 1024 * 1024
MOE_ROWS = 512
TOKEN_TILE = 512
KEY_TILE = 512
PROMPT_MOE_PARTS = 2
GROUP_LANE0 = N_EXPERTS


def _params(*sem):
    return pltpu.CompilerParams(dimension_semantics=sem, vmem_limit_bytes=VMEM_LIMIT_BYTES)


def _rms(x, g):
    return (x * lax.rsqrt(jnp.mean(x * x, axis=-1, keepdims=True) + EPS)) * g


def _rope(x, cos, sin_lo, sin_hi):
    outs = []
    for c in range(x.shape[1] // LANES):
        xc = x[:, c * LANES:(c + 1) * LANES]
        up = pltpu.roll(xc, LANES - ROT_DIM // 2, axis=1)
        dn = pltpu.roll(xc, ROT_DIM // 2, axis=1)
        outs.append(xc * cos + up * sin_lo + dn * sin_hi)
    return jnp.concatenate(outs, axis=1)


def _inproj_kernel(x_ref, g_ref, w_ref, wf_ref, bf_ref, cos_ref, slo_ref, shi_ref,
                   fk_ref, fv_ref, logf_ref, dk_ref, dv_ref,
                   qf_ref, kf_ref, vf_ref, qd_ref, kd_ref, vd_ref, *, q_scale, time_major):
    n = _rms(x_ref[...], g_ref[...]).astype(BF16)
    turn = (lambda a: a.T) if time_major else (lambda a: a)

    def proj(c):
        return jnp.dot(n, w_ref[:, c * FOX_W:(c + 1) * FOX_W], preferred_element_type=F32)

    cos, slo, shi = cos_ref[...], slo_ref[...], shi_ref[...]
    qf_ref[...] = (proj(0) * q_scale).astype(BF16)
    fk = turn(proj(1))
    fk_ref[...] = fk
    kf_ref[...] = fk.astype(BF16)
    fv = proj(2)
    fv_ref[...] = turn(fv)
    vf_ref[...] = fv.astype(BF16)
    qd_ref[...] = (_rope(proj(3), cos, slo, shi) * q_scale).astype(BF16)
    dk = turn(_rope(proj(4), cos, slo, shi))
    dk_ref[...] = dk
    kd_ref[...] = dk.astype(BF16)
    dv = proj(5)
    for h in range(DIFF_HEADS):
        dv_ref[pl.ds(h, dv.shape[0], stride=DIFF_HEADS), :] = dv[:, h * LANES:(h + 1) * LANES]
    vd_ref[...] = dv.astype(BF16)
    z = jnp.dot(n, wf_ref[...], preferred_element_type=F32) + bf_ref[...]
    logf = jnp.minimum(z, 0.0) - jnp.log1p(jnp.exp(-jnp.abs(z)))
    logf_ref[...] = logf.T[:FOX_HEADS] if time_major else logf[:, :FOX_HEADS]


def _rope_tables(pos):
    half = ROT_DIM // 2
    inv = jnp.power(ROPE_THETA, -jnp.arange(half, dtype=F32) / half)
    ang = pos.astype(F32)[:, None] * inv[None, :]
    cos, sin = jnp.cos(ang), jnp.sin(ang)
    t = pos.shape[0]
    pad = jnp.zeros((t, HEAD_DIM - ROT_DIM), F32)
    zero = jnp.zeros((t, half), F32)
    cos64 = jnp.concatenate([cos, cos, pad + 1.0], axis=1)
    slo64 = jnp.concatenate([-sin, zero, pad], axis=1)
    shi64 = jnp.concatenate([zero, sin, pad], axis=1)
    return tuple(jnp.tile(a, (1, LANES // HEAD_DIM)) for a in (cos64, slo64, shi64))


def _inproj(x, g, w_main, w_ff, b_ff, pos, tm):
    b, t, d = x.shape
    n = b * t
    tables = _rope_tables(pos)
    time_major = tm <= t
    if time_major:
        assert t % tm == 0
        per = t // tm
        tab_map = lambda i: (i % per, 0)
        turned = pl.BlockSpec((None, FOX_W, tm), lambda i: (i // per, 0, i % per))
        gate = pl.BlockSpec((None, FOX_HEADS, tm), lambda i: (i // per, 0, i % per))
        turned_shape, gate_shape = (b, FOX_W, t), (b, FOX_HEADS, t)
    else:
        assert tm % t == 0
        tables = tuple(jnp.tile(a, (tm // t, 1)) for a in tables)
        tab_map = lambda i: (0, 0)
    assert n % tm == 0
    row = lambda i: (i, 0)
    const = lambda i: (0, 0)
    wide = pl.BlockSpec((tm, FOX_W), row)
    if not time_major:
        turned, gate = wide, pl.BlockSpec((tm, FOX_HEADS), row)
        turned_shape, gate_shape = (n, FOX_W), (n, FOX_HEADS)
    tab = pl.BlockSpec((tm, LANES), tab_map)
    sds = jax.ShapeDtypeStruct
    return pl.pallas_call(
        functools.partial(_inproj_kernel, q_scale=HEAD_DIM ** -0.5 * LOG2E, time_major=time_major),
        grid=(n // tm,),
        in_specs=[pl.BlockSpec((tm, d), row), pl.BlockSpec((1, d), const),
                  pl.BlockSpec(w_main.shape, const), pl.BlockSpec(w_ff.shape, const),
                  pl.BlockSpec((1, LANES), const), tab, tab, tab],
        out_specs=[turned, turned, gate, turned, pl.BlockSpec((tm * DIFF_HEADS, LANES), row),
                   wide, turned, wide, wide, turned, wide],
        out_shape=[sds(turned_shape, F32), sds(turned_shape, F32), sds(gate_shape, F32),
                   sds(turned_shape, F32), sds((n * DIFF_HEADS, LANES), F32),
                   sds((n, FOX_W), BF16), sds(turned_shape, BF16), sds((n, FOX_W), BF16),
                   sds((n, FOX_W), BF16), sds(turned_shape, BF16), sds((n, FOX_W), BF16)],
        compiler_params=_params("arbitrary"),
    )(x.reshape(n, d), g.reshape(1, d), w_main, w_ff,
      jnp.pad(b_ff.reshape(1, FOX_HEADS), ((0, 0), (0, LANES - FOX_HEADS))), *tables)


AUX_LANES = 8


def _bf16_part(x):
    return pltpu.bitcast(pltpu.bitcast(x, jnp.uint32) & jnp.uint32(0xFFFF0000), F32)


def _cumsum_kernel(x_ref, kb_ref, qb_ref):
    x = x_ref[...]
    s_len = x.shape[1]
    lane = lax.broadcasted_iota(I32, x.shape, 1)
    shift = 1
    while shift < s_len:
        x = x + jnp.where(lane >= shift, pltpu.roll(x, shift, axis=1), 0.0)
        shift *= 2
    c = x * LOG2E
    hi = _bf16_part(c)
    mid = _bf16_part(c - hi)
    lo = _bf16_part(c - hi - mid)
    slot = lax.broadcasted_iota(I32, (LANES, x.shape[0]), 0)
    head = lax.broadcasted_iota(I32, (LANES, x.shape[0]), 1)

    def place(first_slot):
        out = jnp.zeros((LANES, s_len), F32)
        for i, part in enumerate((hi, mid, lo)):
            sel = jnp.where(slot == head * AUX_LANES + first_slot + i, 1.0, 0.0).astype(BF16)
            out = out + jnp.dot(sel, part.astype(BF16), preferred_element_type=F32)
        return out

    slot_col = lax.broadcasted_iota(I32, (LANES, s_len), 0)
    used = slot_col < FOX_HEADS * AUX_LANES
    within = slot_col % AUX_LANES
    kb_ref[...] = (jnp.where(used & (within < 3), 1.0, 0.0) - place(3)).astype(BF16)
    qb_ref[...] = (jnp.where(used & (within >= 3) & (within < 6), 1.0, 0.0) + place(0)).astype(BF16)


def _fox_bias(logf_t):
    b, h, s = logf_t.shape
    x = jnp.pad(logf_t, ((0, 0), (0, 16 - h), (0, 0)))
    out = pl.BlockSpec((None, LANES, s), lambda i: (i, 0, 0))
    return pl.pallas_call(
        _cumsum_kernel, grid=(b,), in_specs=[pl.BlockSpec((None, 16, s), lambda i: (i, 0, 0))],
        out_specs=[out, out],
        out_shape=[jax.ShapeDtypeStruct((b, LANES, s), BF16)] * 2,
        compiler_params=_params("arbitrary"),
    )(x)


def _attn_kernel(*refs, tq, tk, past, s_valid, fox, lam_init):
    if fox:
        q_ref, qa_ref, k_ref, ka_ref, v_ref, o_ref, sa_ref, sb_ref, mx_ref, m_ref, l_ref, acc_ref = refs
    else:
        q_ref, k_ref, v_ref, lam_ref, g_ref, o_ref, sa_ref, sb_ref, mx_ref, m_ref, l_ref, acc_ref = refs
    i = pl.program_id(2)
    q = q_ref[...]
    lane = lax.broadcasted_iota(I32, (tq, LANES), 1)
    low = lane < HEAD_DIM
    zero = jnp.zeros_like(q)
    q_part = [jnp.where(low, q, zero), jnp.where(low, zero, q)]
    if fox:
        qa = qa_ref[...]
        for h in range(2):
            lane0 = (2 * pl.program_id(1) + h) * AUX_LANES
            aux = jnp.where((lane >= lane0) & (lane < lane0 + AUX_LANES), qa, zero)
            q_part[h] = jnp.concatenate([q_part[h], aux], axis=1)
    m_ref[...] = jnp.full(m_ref.shape, NEG, F32)
    l_ref[...] = jnp.zeros(l_ref.shape, F32)
    acc_ref[...] = jnp.zeros(acc_ref.shape, F32)
    q0 = past + i * tq
    nb = q0 // tk
    bufs = (sa_ref, sb_ref)

    def scores(j, buf):
        ks = pl.multiple_of(j * tk, tk)
        k = k_ref[:, pl.ds(ks, tk)]
        if fox:
            k = jnp.concatenate([k, ka_ref[:, pl.ds(ks, tk)]], axis=0)
        for h in range(2):
            s = jnp.dot(q_part[h], k, preferred_element_type=F32)
            bufs[buf][h] = s
            mx_ref[buf, h] = jnp.broadcast_to(jnp.max(s, axis=1, keepdims=True), (tq, LANES))

    def consume(j, buf, use_mask):
        ks = pl.multiple_of(j * tk, tk)
        v = jnp.concatenate([v_ref[pl.ds(ks, tk), :], jnp.ones((tk, LANES), BF16)], axis=1)
        pv, alpha = [], []
        for h in range(2):
            t = bufs[buf][h]
            if use_mask:
                qpos = q0 + lax.broadcasted_iota(I32, (tq, tk), 0)
                kpos = j * tk + lax.broadcasted_iota(I32, (tq, tk), 1)
                if fox:
                    seen = kpos <= qpos
                else:
                    seen = kpos < jnp.minimum((qpos // CHUNK + 1) * CHUNK, s_valid)
                t = jnp.where(seen, t, NEG)
                mx = jnp.max(t, axis=1, keepdims=True)
            else:
                mx = mx_ref[buf, h]
            m_prev = m_ref[h]
            m_new = jnp.maximum(m_prev, mx)
            a = jnp.exp2(m_prev - m_new)
            p = jnp.exp2(t - jnp.concatenate([m_new] * (tk // LANES), axis=1))
            m_ref[h] = m_new
            both = jnp.dot(p.astype(BF16), v, preferred_element_type=F32)
            pv.append(both[:, :LANES])
            l_ref[h] = a * l_ref[h] + both[:, LANES:]
            alpha.append(a)
        if fox:
            acc_ref[0] = acc_ref[0] * jnp.where(low, alpha[0], alpha[1]) + jnp.where(low, pv[0], pv[1])
        else:
            for h in range(2):
                acc_ref[h] = acc_ref[h] * alpha[h] + pv[h]

    scores(0, 0)

    def pair(jj, c):
        j = 2 * jj
        scores(j + 1, 1)
        consume(j, 0, False)
        scores(j + 2, 0)
        consume(j + 1, 1, False)
        return c

    lax.fori_loop(0, nb // 2, pair, 0)

    @pl.when(nb % 2 == 1)
    def _():
        scores(nb, 1)
        consume(nb - 1, 0, False)
        consume(nb, 1, True)

    @pl.when(nb % 2 == 0)
    def _():
        consume(nb, 0, True)

    if fox:
        o = acc_ref[0] / jnp.where(low, l_ref[0], l_ref[1])
    else:
        lam_v = lam_ref[...]
        lam = (jnp.exp(jnp.sum(lam_v[0:1] * lam_v[1:2], axis=1, keepdims=True))
               - jnp.exp(jnp.sum(lam_v[2:3] * lam_v[3:4], axis=1, keepdims=True)) + lam_init)
        o = acc_ref[0] / l_ref[0] - lam * (acc_ref[1] / l_ref[1])
        o = _rms(o, g_ref[...]) * (1.0 - lam_init)
    o_ref[...] = o.astype(o_ref.dtype)


def _self_attention(q, k, v, extra, *, b, t, s_pad, s_valid, past, tq, tk, fox, lam_init):
    assert tk % tq == 0 and past % tq == 0 and t % tq == 0 and tk % CHUNK == 0 and s_pad % tk == 0
    assert tq % CHUNK == 0 or (CHUNK % tq == 0 and past % CHUNK == 0)
    n_q = t // tq
    k_spec = pl.BlockSpec((None, LANES, s_pad), lambda bi, p, i: (bi, p, 0))
    v_spec = pl.BlockSpec((None, s_pad, LANES), lambda bi, p, i: (bi, 0, p))
    q_spec = pl.BlockSpec((tq, LANES), lambda bi, p, i: (bi * n_q + i, p))
    const = lambda bi, p, i: (0, 0)
    if fox:
        in_specs = [q_spec, pl.BlockSpec((tq, LANES), lambda bi, p, i: (bi * n_q + i, 0)), k_spec,
                    pl.BlockSpec((None, LANES, s_pad), lambda bi, p, i: (bi, 0, 0)), v_spec]
        args = (q, extra[0], k, extra[1], v)
    else:
        in_specs = [q_spec, k_spec, v_spec, pl.BlockSpec((4, HEAD_DIM), const), pl.BlockSpec((1, LANES), const)]
        args = (q, k, v, extra[0], extra[1].reshape(1, LANES))
    stat = pltpu.VMEM((2, tq, LANES), F32)
    return pl.pallas_call(
        functools.partial(_attn_kernel, tq=tq, tk=tk, past=past, s_valid=s_valid, fox=fox, lam_init=lam_init),
        grid=(b, FOX_W // LANES, n_q),
        in_specs=in_specs,
        out_specs=q_spec,
        out_shape=jax.ShapeDtypeStruct((b * t, FOX_W), BF16),
        scratch_shapes=[pltpu.VMEM((2, tq, tk), F32), pltpu.VMEM((2, tq, tk), F32),
                        pltpu.VMEM((2, 2, tq, LANES), F32), stat, stat, stat],
        compiler_params=_params("arbitrary", "arbitrary", "arbitrary"),
    )(*args)


def _outproj_kernel(x_ref, fo_ref, do_ref, wof_ref, wod_ref, g_ref, wq_ref, h_ref, q_ref, *, q_scale):
    h = (x_ref[...] + jnp.dot(fo_ref[...], wof_ref[...], preferred_element_type=F32)
         + jnp.dot(do_ref[...], wod_ref[...], preferred_element_type=F32))
    h_ref[...] = h
    n = _rms(h, g_ref[...]).astype(BF16)
    q_ref[...] = (jnp.dot(n, wq_ref[...], preferred_element_type=F32) * q_scale).astype(BF16)


def _outproj(x2, fox_o, diff_o, wo_f, wo_d, g_mem_q, w_mq, tm):
    n, d = x2.shape
    row = lambda i: (i, 0)
    const = lambda i: (0, 0)
    return pl.pallas_call(
        functools.partial(_outproj_kernel, q_scale=MEM_HEAD_DIM ** -0.5 * LOG2E),
        grid=(n // tm,),
        in_specs=[pl.BlockSpec((tm, d), row), pl.BlockSpec((tm, FOX_W), row),
                  pl.BlockSpec((tm, DIFF_W), row), pl.BlockSpec((FOX_W, d), const),
                  pl.BlockSpec((DIFF_W, d), const), pl.BlockSpec((1, d), const),
                  pl.BlockSpec((d, d), const)],
        out_specs=[pl.BlockSpec((tm, d), row), pl.BlockSpec((tm, d), row)],
        out_shape=[jax.ShapeDtypeStruct((n, d), F32), jax.ShapeDtypeStruct((n, d), BF16)],
        compiler_params=_params("arbitrary"),
    )(x2, fox_o, diff_o, wo_f, wo_d, g_mem_q.reshape(1, d), w_mq)


def _memkv_kernel(m_ref, g_ref, wk_ref, wv_ref, k_ref, v_ref, kb_ref, vb_ref):
    n = _rms(m_ref[...], g_ref[...]).astype(BF16)
    k = jnp.dot(n, wk_ref[...], preferred_element_type=F32)
    v = jnp.dot(n, wv_ref[...], preferred_element_type=F32)
    k_ref[...] = k
    v_ref[...] = v
    kb_ref[...] = k.astype(BF16)
    vb_ref[...] = v.astype(BF16)


def _memkv(mem2, g, wk, wv, tm):
    n, d = mem2.shape
    row = lambda i: (i, 0)
    const = lambda i: (0, 0)
    blk = pl.BlockSpec((tm, d), row)
    wspec = pl.BlockSpec((d, d), const)
    return pl.pallas_call(
        _memkv_kernel, grid=(n // tm,),
        in_specs=[blk, pl.BlockSpec((1, d), const), wspec, wspec],
        out_specs=[blk, blk, blk, blk],
        out_shape=[jax.ShapeDtypeStruct((n, d), F32)] * 2 + [jax.ShapeDtypeStruct((n, d), BF16)] * 2,
        compiler_params=_params("arbitrary"),
    )(mem2, g.reshape(1, d), wk, wv)


U32 = jnp.uint32
HALF_D = D_MODEL // 2
ROW_TILE = HALF_D // LANES
BF16_BITS = 0xFFFF0000


def _pack_rows(val):
    def rounded(x):
        bits = pltpu.bitcast(x, U32)
        return (bits + U32(0x7FFF) + ((bits >> 16) & U32(1))) & U32(BF16_BITS)
    return rounded(val[:, :HALF_D]) | (rounded(val[:, HALF_D:]) >> 16)


def _unpack_rows(words):
    return jnp.concatenate([pltpu.bitcast(words & U32(BF16_BITS), F32), pltpu.bitcast(words << 16, F32)], axis=1)


def _load_tile_rows(ref):
    rows = ref.shape[0] // ROW_TILE
    return jnp.concatenate([ref[pl.ds(s, rows, stride=ROW_TILE), :] for s in range(ROW_TILE)], axis=1)


def _store_tile_rows(ref, words):
    rows = words.shape[0]
    for s in range(ROW_TILE):
        ref[pl.ds(s, rows, stride=ROW_TILE), :] = words[:, s * LANES:(s + 1) * LANES]


def _memattn_kernel(q_ref, k_ref, v_ref, h_ref, wo_ref, g_ref, wr_ref, br_ref,
                    h2_ref, xn_ref, logit_ref):
    q = q_ref[...]
    outs = []
    for h in range(MEM_HEADS):
        sl = slice(h * MEM_HEAD_DIM, (h + 1) * MEM_HEAD_DIM)
        s = lax.dot_general(q[:, sl], k_ref[:, sl], (((1,), (1,)), ((), ())), preferred_element_type=F32)
        p = jnp.exp2(s - jnp.max(s, axis=1, keepdims=True))
        o = jnp.dot(p.astype(BF16), v_ref[:, sl], preferred_element_type=F32)
        outs.append((o / jnp.sum(p, axis=1, keepdims=True)).astype(BF16))
    h2 = h_ref[...] + jnp.dot(jnp.concatenate(outs, axis=1), wo_ref[...], preferred_element_type=F32)
    h2_ref[...] = h2
    xn_f32 = _rms(h2, g_ref[...])
    words = _pack_rows(xn_f32)
    for s in range(ROW_TILE):
        xn_ref[s] = words[:, s * LANES:(s + 1) * LANES]
    logit_ref[...] = jnp.dot(xn_f32.astype(BF16), wr_ref[...], preferred_element_type=F32) + br_ref[...]


def _memattn(qm, mk, mv, h1, w_mo, g_ffn, w_router, b_router, *, b0, b, t, tq):
    d = h1.shape[1]
    n = b * t
    n_q = t // tq
    row = lambda bi, i: (bi * n_q + i, 0)
    src_row = lambda bi, i: ((b0 + bi) * n_q + i, 0)
    const = lambda bi, i: (0, 0)
    mem = pl.BlockSpec((None, MEM_LEN, d), lambda bi, i: (b0 + bi, 0, 0))
    src = pl.BlockSpec((tq, d), src_row)
    blk = pl.BlockSpec((tq, d), row)
    planes = pl.BlockSpec((ROW_TILE, tq, LANES), lambda bi, i: (0, bi * n_q + i, 0))
    return pl.pallas_call(
        _memattn_kernel,
        grid=(b, n_q),
        in_specs=[src, mem, mem, src, pl.BlockSpec((d, d), const), pl.BlockSpec((1, d), const),
                  pl.BlockSpec((d, LANES), const), pl.BlockSpec((1, LANES), const)],
        out_specs=[blk, planes, pl.BlockSpec((tq, LANES), row)],
        out_shape=[jax.ShapeDtypeStruct((n, d), F32), jax.ShapeDtypeStruct((ROW_TILE, n, LANES), U32),
                   jax.ShapeDtypeStruct((n, LANES), F32)],
        compiler_params=_params("arbitrary", "arbitrary"),
    )(qm, mk, mv, h1, w_mo, g_ffn.reshape(1, d), w_router, b_router)


ROUTER_ROWS = 48


def _first_slot(hit, slot_f32):
    return jnp.min(jnp.where(hit, slot_f32, float(LANES)), axis=0, keepdims=True).astype(I32)


def _router_kernel(lt_ref, ri_ref, rg_ref, cnt_ref, carry_ref, *, rt):
    @pl.when(pl.program_id(0) == 0)
    def _():
        carry_ref[...] = jnp.zeros(carry_ref.shape, F32)

    lt = lt_ref[0:ROUTER_ROWS, :]
    slot = lax.broadcasted_iota(I32, (ROUTER_ROWS, rt), 0)
    slot_f = slot.astype(F32)
    g_log = jnp.where((slot >= GROUP_LANE0) & (slot < GROUP_LANE0 + N_GROUPS), lt, NEG)
    g_max = jnp.max(g_log, axis=0, keepdims=True)
    g_idx = _first_slot(g_log == g_max, slot_f) - GROUP_LANE0
    g_p = 1.0 / jnp.sum(jnp.exp(g_log - g_max), axis=0, keepdims=True)
    in_group = (slot >= g_idx * EXPERTS_PER_GROUP) & (slot < (g_idx + 1) * EXPERTS_PER_GROUP)
    e_log = jnp.where(in_group, lt, NEG)
    e_exp = jnp.exp(e_log - jnp.max(e_log, axis=0, keepdims=True))
    prob = jnp.where(in_group, e_exp / jnp.sum(e_exp, axis=0, keepdims=True), -1.0)
    p1 = jnp.max(prob, axis=0, keepdims=True)
    i1 = _first_slot(prob == p1, slot_f)
    rest = jnp.where(slot == i1, -1.0, prob)
    p2 = jnp.max(rest, axis=0, keepdims=True)
    i2 = _first_slot(rest == p2, slot_f)
    top_sum = p1 + p2

    hot1, hot2 = slot == i1, slot == i2
    onehot = jnp.where(hot1 | hot2, 1.0, 0.0)
    earlier = lax.broadcasted_iota(I32, (rt, rt), 0) < lax.broadcasted_iota(I32, (rt, rt), 1)
    before = jnp.dot(onehot.astype(BF16), jnp.where(earlier, 1.0, 0.0).astype(BF16),
                     preferred_element_type=F32) + carry_ref[0:ROUTER_ROWS, 0:1]
    r1 = jnp.sum(jnp.where(hot1, before, 0.0), axis=0, keepdims=True)
    r2 = jnp.sum(jnp.where(hot2, before, 0.0), axis=0, keepdims=True)
    carry_ref[0:ROUTER_ROWS, :] = carry_ref[0:ROUTER_ROWS, :] + jnp.sum(onehot, axis=1, keepdims=True)
    cnt_ref[...] = carry_ref[...]

    ri_ref[...] = jnp.concatenate([i1, i2, r1.astype(I32), r2.astype(I32), jnp.zeros((4, rt), I32)], axis=0)
    rg_ref[...] = jnp.concatenate([g_p * p1 / top_sum, g_p * p2 / top_sum, jnp.zeros((6, rt), F32)], axis=0)


def _router(logits_t, rt):
    n = logits_t.shape[1]
    col = lambda i: (0, i)
    return pl.pallas_call(
        functools.partial(_router_kernel, rt=rt),
        grid=(n // rt,),
        in_specs=[pl.BlockSpec((LANES, rt), col)],
        out_specs=[pl.BlockSpec((8, rt), col), pl.BlockSpec((8, rt), col),
                   pl.BlockSpec((LANES, LANES), lambda i: (0, 0))],
        out_shape=[jax.ShapeDtypeStruct((8, n), I32), jax.ShapeDtypeStruct((8, n), F32),
                   jax.ShapeDtypeStruct((LANES, LANES), F32)],
        scratch_shapes=[pltpu.VMEM((LANES, LANES), F32)],
        compiler_params=_params("arbitrary"),
    )(logits_t)


SC_CORES, SC_SUBCORES = 2, 16
SC_WINDOW = 128


def _sc_mesh():
    return plsc.VectorSubcoreMesh(core_axis_name="core", subcore_axis_name="subcore")


def _sc_split(n_windows):
    if n_windows % (SC_CORES * SC_SUBCORES) == 0:
        return ("core", "subcore")
    assert n_windows % SC_SUBCORES == 0
    return "subcore"


def _sc_repeat(idx):
    n_windows = idx.shape[0] // SC_WINDOW
    reps = SC_SUBCORES // math.gcd(n_windows, SC_SUBCORES)
    return jnp.tile(idx, reps)


def _sc_scatter_rows(src, idx, n_out):
    rows = src.shape[0]
    win = SC_WINDOW
    n_src_blocks = rows // win
    idx = _sc_repeat(idx)
    n_idx = idx.shape[0]

    @functools.partial(pl.kernel, out_type=jax.ShapeDtypeStruct((n_out, LANES), src.dtype),
                       mesh=_sc_mesh(), scratch_types=[])
    def scatter(x_hbm, i_hbm, o_hbm):
        def body(x_vmem, i_vmem):
            pltpu.sync_copy(x_vmem, o_hbm.at[i_vmem.at[0]])

        pltpu.emit_pipeline(
            body, grid=(n_idx // win,),
            in_specs=[pl.BlockSpec((win, LANES), lambda i: (i % n_src_blocks, 0)),
                      pl.BlockSpec((1, win), lambda i: (0, i))],
            out_specs=[],
            core_axis_name=_sc_split(n_idx // win), dimension_semantics=(pltpu.PARALLEL,),
        )(x_hbm, i_hbm)

    return scatter(src, idx.reshape(1, n_idx))


def _sc_gather_rows(src, idx):
    n_rows = idx.shape[0]
    idx = _sc_repeat(idx)
    n_out = idx.shape[0]
    win = SC_WINDOW

    @functools.partial(pl.kernel, out_type=jax.ShapeDtypeStruct((n_out, LANES), src.dtype),
                       mesh=_sc_mesh(), scratch_types=[])
    def gather(x_hbm, i_hbm, o_hbm):
        def body(i_vmem, o_vmem):
            pltpu.sync_copy(x_hbm.at[i_vmem.at[0]], o_vmem)

        pltpu.emit_pipeline(
            body, grid=(n_out // win,),
            in_specs=[pl.BlockSpec((1, win), lambda i: (0, i))],
            out_specs=[pl.BlockSpec((win, LANES), lambda i: (i, 0))],
            core_axis_name=_sc_split(n_out // win), dimension_semantics=(pltpu.PARALLEL,),
        )(i_hbm, o_hbm)

    return gather(src, idx.reshape(1, n_out))[:n_rows]


def _expert_kernel(blk_ref, e_ref, lo_ref, hi_ref, x_ref, w1_ref, w3_ref, w2_ref, y_ref):
    w = pl.program_id(0)
    lo, hi = lo_ref[w], hi_ref[w]
    row0 = blk_ref[w] * (x_ref.shape[0] // ROW_TILE)

    def compute():
        x = _unpack_rows(_load_tile_rows(x_ref)).astype(BF16)
        a = jnp.dot(x, w1_ref[...], preferred_element_type=F32)
        g = jnp.dot(x, w3_ref[...], preferred_element_type=F32)
        mid = ((a / (1.0 + jnp.exp(-a))) * g).astype(BF16)
        y = _pack_rows(jnp.dot(mid, w2_ref[...], preferred_element_type=F32))
        rows = row0 + lax.broadcasted_iota(I32, y.shape, 0)
        return y, (rows >= lo) & (rows < hi)

    @pl.when((hi > lo) & (lo == row0))
    def _():
        y, mine = compute()
        _store_tile_rows(y_ref, jnp.where(mine, y, jnp.zeros_like(y)))

    @pl.when((hi > lo) & (lo != row0))
    def _():
        y, mine = compute()
        _store_tile_rows(y_ref, jnp.where(mine, y, _load_tile_rows(y_ref)))


def _experts(items, xs, w1, w3, w2, rows):
    d = D_MODEL
    n_items = items[0].shape[0]
    xmap = lambda w, blk, e, lo, hi: (blk[w], 0)
    wmap = lambda w, blk, e, lo, hi: (e[w], 0, 0)
    return pl.pallas_call(
        _expert_kernel,
        grid_spec=pltpu.PrefetchScalarGridSpec(
            num_scalar_prefetch=4, grid=(n_items,),
            in_specs=[pl.BlockSpec((rows * ROW_TILE, LANES), xmap),
                      pl.BlockSpec((None, d, D_EXPERT), wmap), pl.BlockSpec((None, d, D_EXPERT), wmap),
                      pl.BlockSpec((None, D_EXPERT, d), wmap)],
            out_specs=pl.BlockSpec((rows * ROW_TILE, LANES), xmap)),
        out_shape=jax.ShapeDtypeStruct(xs.shape, U32),
        compiler_params=_params("arbitrary"),
    )(*items, xs, w1, w3, w2)


def _work_items(counts, n_rows, rows):
    n_blocks = n_rows // rows
    n_items = n_blocks + N_EXPERTS - 1
    ends = jnp.cumsum(counts)
    starts = ends - counts
    first_blk = starts // rows
    n_blk = jnp.where(counts > 0, (ends - 1) // rows - first_blk + 1, 0)
    item_end = jnp.cumsum(n_blk)
    total = item_end[-1]
    w = jnp.arange(n_items, dtype=I32)
    wc = jnp.minimum(w, total - 1)
    e = jnp.sum(item_end[None, :] <= wc[:, None], axis=1).astype(I32)
    mine = e[:, None] == jnp.arange(N_EXPERTS, dtype=I32)[None, :]

    def of_expert(per_expert):
        return jnp.sum(jnp.where(mine, per_expert[None, :], 0), axis=1)

    blk = (of_expert(first_blk) + wc - (of_expert(item_end) - of_expert(n_blk))).astype(I32)
    lo = jnp.maximum(of_expert(starts), blk * rows).astype(I32)
    hi = jnp.minimum(of_expert(ends), (blk + 1) * rows).astype(I32)
    live = w < total
    return blk, e, jnp.where(live, lo, 0), jnp.where(live, hi, 0), starts


def _combine_kernel(h_ref, z0_ref, z1_ref, rg_ref, g_ref, *rest):
    y_ref = rest[-1]
    rg = rg_ref[...]
    def rows_of(z_ref):
        return _unpack_rows(jnp.concatenate([z_ref[s] for s in range(ROW_TILE)], axis=1))

    moe = rows_of(z0_ref) * rg[:, 0:1] + rows_of(z1_ref) * rg[:, 1:2]
    y_ref[...] = _rms(h_ref[...] + moe, g_ref[...])


def _combine(h2, z, rg, g_final, tm, y_prev, row0, n_total):
    n, d = h2.shape
    n_t = n // tm
    row = lambda i: (i, 0)
    in_specs = [pl.BlockSpec((tm, d), row), pl.BlockSpec((ROW_TILE, tm, LANES), lambda i: (0, i, 0)),
                pl.BlockSpec((ROW_TILE, tm, LANES), lambda i: (1, i, 0)), pl.BlockSpec((tm, 8), row),
                pl.BlockSpec((1, d), lambda i: (0, 0))]
    args = [h2, z, z, rg, g_final.reshape(1, d)]
    aliases = {}
    if y_prev is not None:
        in_specs.append(pl.BlockSpec(memory_space=pl.ANY))
        args.append(y_prev)
        aliases = {len(args) - 1: 0}
    return pl.pallas_call(
        _combine_kernel, grid=(n_t,),
        in_specs=in_specs,
        out_specs=pl.BlockSpec((tm, d), lambda i: (row0 // tm + i, 0)),
        out_shape=jax.ShapeDtypeStruct((n_total, d), F32),
        input_output_aliases=aliases,
        compiler_params=_params("arbitrary"),
    )(*args)


def _moe_and_final(h2, xn, ri, rg, counts, w1, w3, w2, g_final, *, tm, y_prev, row0, n_total):
    n = h2.shape[0]
    rows = min(MOE_ROWS, 2 * n)
    blk, e, lo, hi, starts = _work_items(counts, 2 * n, rows)
    expert = jnp.arange(N_EXPERTS, dtype=I32)[:, None, None]
    first = jnp.sum(jnp.where(ri[None, 0:2] == expert, starts.astype(I32)[:, None, None], 0), axis=0)
    pos = (first + ri[2:4]).astype(I32)
    sub = (pos[:, None, :] * ROW_TILE + jnp.arange(ROW_TILE, dtype=I32)[None, :, None]).reshape(2 * n * ROW_TILE)
    xs = _sc_scatter_rows(xn.reshape(ROW_TILE * n, LANES), sub, 2 * n * ROW_TILE)
    ys = _experts((blk, e, lo, hi), xs, w1, w3, w2, rows)
    z = _sc_gather_rows(ys, sub).reshape(2 * ROW_TILE, n, LANES)
    return _combine(h2, z, rg.T, g_final, tm, y_prev, row0, n_total)


def _pad_time(a, s_pad):
    return jnp.pad(a, ((0, 0), (0, s_pad - a.shape[1])) + ((0, 0),) * (a.ndim - 2))


def _layer(x, past, mem_k, mem_v, wts, lam_init, g_final, *, tm, tq, tk, moe_parts):
    b, t, d = x.shape
    n = b * t
    p_len = 0 if past is None else past[0].shape[1]
    s_valid = p_len + t
    s_pad = -(-s_valid // tk) * tk
    pos = p_len + jnp.arange(t)

    fk, fv, logf, dk, dv, qf, kf, vf, qd, kd, vd = _inproj(
        x, wts['g_mix'], wts['w_main'], wts['w_ff'], wts['b_forget'], pos, tm)
    time_major = tm <= t

    def leaf(a, dims):
        if time_major:
            return jnp.moveaxis(a.reshape((b,) + dims + (t,)), -1, 1)
        return a.reshape((b, t) + dims)

    new = (leaf(fk, (FOX_HEADS, HEAD_DIM)), leaf(fv, (FOX_HEADS, HEAD_DIM)), leaf(logf, (FOX_HEADS,)),
           leaf(dk, (DIFF_HEADS, 2, HEAD_DIM)), dv.reshape(b, t, DIFF_HEADS, 2 * HEAD_DIM))

    def keys_t(cur, old_rows):
        if not time_major:
            cur = jnp.swapaxes(cur.reshape(b, t, FOX_W), 1, 2)
        if old_rows is not None:
            prev = jnp.swapaxes(old_rows.reshape(b, p_len, FOX_W).astype(BF16), 1, 2)
            cur = jnp.concatenate([prev, cur], axis=2)
        return jnp.pad(cur, ((0, 0), (0, 0), (0, s_pad - s_valid)))

    def with_past(new_b16, old_rows):
        cur = new_b16.reshape(b, t, FOX_W)
        if old_rows is not None:
            cur = jnp.concatenate([old_rows.reshape(b, p_len, FOX_W).astype(BF16), cur], axis=1)
        return _pad_time(cur, s_pad)

    logf_t = logf if time_major else jnp.swapaxes(logf.reshape(b, t, FOX_HEADS), 1, 2)
    if past is not None:
        logf_t = jnp.concatenate([jnp.swapaxes(past[2].astype(F32), 1, 2), logf_t], axis=2)
    k_bias, q_bias_t = _fox_bias(jnp.pad(logf_t, ((0, 0), (0, 0), (0, s_pad - s_valid))))
    q_bias = jnp.swapaxes(q_bias_t[:, :, p_len:p_len + t], 1, 2).reshape(n, LANES)

    old = (None,) * 5 if past is None else past
    geom = dict(b=b, t=t, s_pad=s_pad, s_valid=s_valid, past=p_len, tq=tq, tk=tk, lam_init=lam_init)
    fox_o = _self_attention(qf, keys_t(kf, old[0]), with_past(vf, old[1]), (q_bias, k_bias),
                            fox=True, **geom)
    diff_o = _self_attention(qd, keys_t(kd, old[3]), with_past(vd, old[4]),
                             (wts['lam_vecs'], wts['g_diff']), fox=False, **geom)
    h1, qm = _outproj(x.reshape(n, d), fox_o, diff_o, wts['wo_f'], wts['wo_d'], wts['g_mem_q'],
                      wts['w_mq'], tm)
    assert b % moe_parts == 0
    bp = b // moe_parts
    y = None
    for part in range(moe_parts):
        h2, xn, logits = _memattn(qm, mem_k, mem_v, h1, wts['w_mo'], wts['g_ffn'], wts['w_router'],
                                  wts['b_router'], b0=part * bp, b=bp, t=t, tq=tq)
        ri, rg, cnt = _router(logits.T, min(tm, bp * t))
        counts = cnt[:N_EXPERTS, 0].astype(I32)
        y = _moe_and_final(h2, xn, ri, rg, counts, wts['w1'], wts['w3'], wts['w2'], g_final, tm=tm,
                           y_prev=y, row0=part * bp * t, n_total=n)
    return y.reshape(b, t, d), new


def kernel(x_prompt, x_sample, cache_fox_k, cache_fox_v, cache_fox_logf, cache_diff_k, cache_diff_v, cache_mem_k, cache_mem_v, mem_prompt, g_mix, w_in, b_forget, lam_q1, lam_k1, lam_q2, lam_k2, g_diff, w_o, g_mem_q, g_mem_kv, w_mq, w_mk, w_mv, w_mo, g_ffn, w_group, b_group, w_erouter, b_erouter, w1, w3, w2, g_final):
    depth = w_in.shape[0]
    assert depth == 1
    l = 0
    lam_init = 0.8 - 0.6 * math.exp(-0.3 * l)
    d = D_MODEL
    w = w_in[l]
    ff0 = 3 * FOX_W
    w_router = jnp.zeros((d, LANES), F32).at[:, :N_EXPERTS].set(w_erouter[l])
    w_router = w_router.at[:, GROUP_LANE0:GROUP_LANE0 + N_GROUPS].set(w_group[l])
    b_router = jnp.zeros((1, LANES), F32).at[0, :N_EXPERTS].set(b_erouter[l])
    b_router = b_router.at[0, GROUP_LANE0:GROUP_LANE0 + N_GROUPS].set(b_group[l])
    wts = {
        'g_mix': g_mix[l],
        'w_main': jnp.concatenate([w[:, :ff0], w[:, ff0 + FOX_HEADS:]], axis=1).astype(BF16),
        'w_ff': jnp.pad(w[:, ff0:ff0 + FOX_HEADS], ((0, 0), (0, LANES - FOX_HEADS))).astype(BF16),
        'b_forget': b_forget[l],
        'lam_vecs': jnp.stack([lam_q1[l], lam_k1[l], lam_q2[l], lam_k2[l]]),
        'g_diff': g_diff[l],
        'wo_f': w_o[l][:FOX_W].astype(BF16), 'wo_d': w_o[l][FOX_W:].astype(BF16),
        'g_mem_q': g_mem_q[l], 'w_mq': w_mq[l].astype(BF16), 'w_mo': w_mo[l].astype(BF16),
        'g_ffn': g_ffn[l], 'w_router': w_router.astype(BF16), 'b_router': b_router,
        'w1': w1[l].astype(BF16), 'w3': w3[l].astype(BF16), 'w2': w2[l].astype(BF16),
    }

    bp, tp, _ = x_prompt.shape
    mk, mv, mk_b, mv_b = _memkv(mem_prompt.reshape(bp * MEM_LEN, d), g_mem_kv[l],
                                w_mk[l].astype(BF16), w_mv[l].astype(BF16), TOKEN_TILE)
    yp, new_p = _layer(x_prompt, None, mk_b.reshape(bp, MEM_LEN, d), mv_b.reshape(bp, MEM_LEN, d),
                       wts, lam_init, g_final, tm=TOKEN_TILE, tq=TOKEN_TILE, tk=KEY_TILE,
                       moe_parts=PROMPT_MOE_PARTS)

    bs, ts, _ = x_sample.shape
    past = (cache_fox_k[l], cache_fox_v[l], cache_fox_logf[l], cache_diff_k[l], cache_diff_v[l])
    ys, new_s = _layer(x_sample, past, cache_mem_k[l].reshape(bs, MEM_LEN, d).astype(BF16),
                       cache_mem_v[l].reshape(bs, MEM_LEN, d).astype(BF16),
                       wts, lam_init, g_final, tm=bs * ts, tq=ts, tk=KEY_TILE, moe_parts=1)

    mem_shape = (1, bp, MEM_LEN, MEM_HEADS, MEM_HEAD_DIM)
    return (yp, ys) + tuple(a[None] for a in new_p) + (mk.reshape(mem_shape), mv.reshape(mem_shape)) \
        + tuple(a[None] for a in new_s)
```

```python
import functools
import math

import jax
import jax.numpy as jnp
from jax import lax
from jax.experimental import pallas as pl
from jax.experimental.pallas import tpu as pltpu
from jax.experimental.pallas import tpu_sc as plsc

F32 = jnp.float32
BF16 = jnp.bfloat16
I32 = jnp.int32

D_MODEL = 1024
HEAD_DIM = 64
FOX_HEADS = 8
DIFF_HEADS = 4
FOX_W = FOX_HEADS * HEAD_DIM
DIFF_W = DIFF_HEADS * 2 * HEAD_DIM
ROT_DIM = HEAD_DIM // 4
ROPE_THETA = 500000.0
CHUNK = 64
MEM_LEN = 256
MEM_HEADS = 4
MEM_HEAD_DIM = D_MODEL // MEM_HEADS
N_GROUPS = 4
EXPERTS_PER_GROUP = 8
N_EXPERTS = N_GROUPS * EXPERTS_PER_GROUP
D_EXPERT = 512
EPS = 1e-6

LANES = 128
LOG2E = 1.4426950408889634
NEG = -1e30
VMEM_LIMIT_BYTES = 56 * 1024 * 1024
MOE_ROWS = 512
TOKEN_TILE = 512
KEY_TILE = 512
PROMPT_MOE_PARTS = 2
GROUP_LANE0 = N_EXPERTS


def _params(*sem):
    return pltpu.CompilerParams(dimension_semantics=sem, vmem_limit_bytes=VMEM_LIMIT_BYTES)


def _rms(x, g):
    return (x * lax.rsqrt(jnp.mean(x * x, axis=-1, keepdims=True) + EPS)) * g


def _rope(x, cos, sin_lo, sin_hi):
    outs = []
    for c in range(x.shape[1] // LANES):
        xc = x[:, c * LANES:(c + 1) * LANES]
        up = pltpu.roll(xc, LANES - ROT_DIM // 2, axis=1)
        dn = pltpu.roll(xc, ROT_DIM // 2, axis=1)
        outs.append(xc * cos + up * sin_lo + dn * sin_hi)
    return jnp.concatenate(outs, axis=1)


def _inproj_kernel(x_ref, g_ref, w_ref, wf_ref, bf_ref, cos_ref, slo_ref, shi_ref,
                   fk_ref, fv_ref, logf_ref, dk_ref, dv_ref,
                   qf_ref, kf_ref, vf_ref, qd_ref, kd_ref, vd_ref, *, q_scale, time_major):
    n = _rms(x_ref[...], g_ref[...]).astype(BF16)
    turn = (lambda a: a.T) if time_major else (lambda a: a)

    def proj(c):
        return jnp.dot(n, w_ref[:, c * FOX_W:(c + 1) * FOX_W], preferred_element_type=F32)

    cos, slo, shi = cos_ref[...], slo_ref[...], shi_ref[...]
    qf_ref[...] = (proj(0) * q_scale).astype(BF16)
    fk = turn(proj(1))
    fk_ref[...] = fk
    kf_ref[...] = fk.astype(BF16)
    fv = proj(2)
    fv_ref[...] = turn(fv)
    vf_ref[...] = fv.astype(BF16)
    qd_ref[...] = (_rope(proj(3), cos, slo, shi) * q_scale).astype(BF16)
    dk = turn(_rope(proj(4), cos, slo, shi))
    dk_ref[...] = dk
    kd_ref[...] = dk.astype(BF16)
    dv = proj(5)
    for h in range(DIFF_HEADS):
        dv_ref[pl.ds(h, dv.shape[0], stride=DIFF_HEADS), :] = dv[:, h * LANES:(h + 1) * LANES]
    vd_ref[...] = dv.astype(BF16)
    z = jnp.dot(n, wf_ref[...], preferred_element_type=F32) + bf_ref[...]
    logf = jnp.minimum(z, 0.0) - jnp.log1p(jnp.exp(-jnp.abs(z)))
    logf_ref[...] = logf.T[:FOX_HEADS] if time_major else logf[:, :FOX_HEADS]


def _rope_tables(pos):
    half = ROT_DIM // 2
    inv = jnp.power(ROPE_THETA, -jnp.arange(half, dtype=F32) / half)
    ang = pos.astype(F32)[:, None] * inv[None, :]
    cos, sin = jnp.cos(ang), jnp.sin(ang)
    t = pos.shape[0]
    pad = jnp.zeros((t, HEAD_DIM - ROT_DIM), F32)
    zero = jnp.zeros((t, half), F32)
    cos64 = jnp.concatenate([cos, cos, pad + 1.0], axis=1)
    slo64 = jnp.concatenate([-sin, zero, pad], axis=1)
    shi64 = jnp.concatenate([zero, sin, pad], axis=1)
    return tuple(jnp.tile(a, (1, LANES // HEAD_DIM)) for a in (cos64, slo64, shi64))


def _inproj(x, g, w_main, w_ff, b_ff, pos, tm):
    b, t, d = x.shape
    n = b * t
    tables = _rope_tables(pos)
    time_major = tm <= t
    if time_major:
        assert t % tm == 0
        per = t // tm
        tab_map = lambda i: (i % per, 0)
        turned = pl.BlockSpec((None, FOX_W, tm), lambda i: (i // per, 0, i % per))
        gate = pl.BlockSpec((None, FOX_HEADS, tm), lambda i: (i // per, 0, i % per))
        turned_shape, gate_shape = (b, FOX_W, t), (b, FOX_HEADS, t)
    else:
        assert tm % t == 0
        tables = tuple(jnp.tile(a, (tm // t, 1)) for a in tables)
        tab_map = lambda i: (0, 0)
    assert n % tm == 0
    row = lambda i: (i, 0)
    const = lambda i: (0, 0)
    wide = pl.BlockSpec((tm, FOX_W), row)
    if not time_major:
        turned, gate = wide, pl.BlockSpec((tm, FOX_HEADS), row)
        turned_shape, gate_shape = (n, FOX_W), (n, FOX_HEADS)
    tab = pl.BlockSpec((tm, LANES), tab_map)
    sds = jax.ShapeDtypeStruct
    return pl.pallas_call(
        functools.partial(_inproj_kernel, q_scale=HEAD_DIM ** -0.5 * LOG2E, time_major=time_major),
        grid=(n // tm,),
        in_specs=[pl.BlockSpec((tm, d), row), pl.BlockSpec((1, d), const),
                  pl.BlockSpec(w_main.shape, const), pl.BlockSpec(w_ff.shape, const),
                  pl.BlockSpec((1, LANES), const), tab, tab, tab],
        out_specs=[turned, turned, gate, turned, pl.BlockSpec((tm * DIFF_HEADS, LANES), row),
                   wide, turned, wide, wide, turned, wide],
        out_shape=[sds(turned_shape, F32), sds(turned_shape, F32), sds(gate_shape, F32),
                   sds(turned_shape, F32), sds((n * DIFF_HEADS, LANES), F32),
                   sds((n, FOX_W), BF16), sds(turned_shape, BF16), sds((n, FOX_W), BF16),
                   sds((n, FOX_W), BF16), sds(turned_shape, BF16), sds((n, FOX_W), BF16)],
        compiler_params=_params("arbitrary"),
    )(x.reshape(n, d), g.reshape(1, d), w_main, w_ff,
      jnp.pad(b_ff.reshape(1, FOX_HEADS), ((0, 0), (0, LANES - FOX_HEADS))), *tables)


AUX_LANES = 8


def _bf16_part(x):
    return pltpu.bitcast(pltpu.bitcast(x, jnp.uint32) & jnp.uint32(0xFFFF0000), F32)


def _cumsum_kernel(x_ref, kb_ref, qb_ref):
    x = x_ref[...]
    s_len = x.shape[1]
    lane = lax.broadcasted_iota(I32, x.shape, 1)
    shift = 1
    while shift < s_len:
        x = x + jnp.where(lane >= shift, pltpu.roll(x, shift, axis=1), 0.0)
        shift *= 2
    c = x * LOG2E
    hi = _bf16_part(c)
    mid = _bf16_part(c - hi)
    lo = _bf16_part(c - hi - mid)
    slot = lax.broadcasted_iota(I32, (LANES, x.shape[0]), 0)
    head = lax.broadcasted_iota(I32, (LANES, x.shape[0]), 1)

    def place(first_slot):
        out = jnp.zeros((LANES, s_len), F32)
        for i, part in enumerate((hi, mid, lo)):
            sel = jnp.where(slot == head * AUX_LANES + first_slot + i, 1.0, 0.0).astype(BF16)
            out = out + jnp.dot(sel, part.astype(BF16), preferred_element_type=F32)
        return out

    slot_col = lax.broadcasted_iota(I32, (LANES, s_len), 0)
    used = slot_col < FOX_HEADS * AUX_LANES
    within = slot_col % AUX_LANES
    kb_ref[...] = (jnp.where(used & (within < 3), 1.0, 0.0) - place(3)).astype(BF16)
    qb_ref[...] = (jnp.where(used & (within >= 3) & (within < 6), 1.0, 0.0) + place(0)).astype(BF16)


def _fox_bias(logf_t):
    b, h, s = logf_t.shape
    x = jnp.pad(logf_t, ((0, 0), (0, 16 - h), (0, 0)))
    out = pl.BlockSpec((None, LANES, s), lambda i: (i, 0, 0))
    return pl.pallas_call(
        _cumsum_kernel, grid=(b,), in_specs=[pl.BlockSpec((None, 16, s), lambda i: (i, 0, 0))],
        out_specs=[out, out],
        out_shape=[jax.ShapeDtypeStruct((b, LANES, s), BF16)] * 2,
        compiler_params=_params("arbitrary"),
    )(x)


def _attn_kernel(*refs, tq, tk, past, s_valid, fox, lam_init):
    if fox:
        q_ref, qa_ref, k_ref, ka_ref, v_ref, o_ref, sa_ref, sb_ref, mx_ref, m_ref, l_ref, acc_ref = refs
    else:
        q_ref, k_ref, v_ref, lam_ref, g_ref, o_ref, sa_ref, sb_ref, mx_ref, m_ref, l_ref, acc_ref = refs
    i = pl.program_id(2)
    q = q_ref[...]
    lane = lax.broadcasted_iota(I32, (tq, LANES), 1)
    low = lane < HEAD_DIM
    zero = jnp.zeros_like(q)
    q_part = [jnp.where(low, q, zero), jnp.where(low, zero, q)]
    if fox:
        qa = qa_ref[...]
        for h in range(2):
            lane0 = (2 * pl.program_id(1) + h) * AUX_LANES
            aux = jnp.where((lane >= lane0) & (lane < lane0 + AUX_LANES), qa, zero)
            q_part[h] = jnp.concatenate([q_part[h], aux], axis=1)
    m_ref[...] = jnp.full(m_ref.shape, NEG, F32)
    l_ref[...] = jnp.zeros(l_ref.shape, F32)
    acc_ref[...] = jnp.zeros(acc_ref.shape, F32)
    q0 = past + i * tq
    nb = q0 // tk
    bufs = (sa_ref, sb_ref)

    def scores(j, buf):
        ks = pl.multiple_of(j * tk, tk)
        k = k_ref[:, pl.ds(ks, tk)]
        if fox:
            k = jnp.concatenate([k, ka_ref[:, pl.ds(ks, tk)]], axis=0)
        for h in range(2):
            s = jnp.dot(q_part[h], k, preferred_element_type=F32)
            bufs[buf][h] = s
            mx_ref[buf, h] = jnp.broadcast_to(jnp.max(s, axis=1, keepdims=True), (tq, LANES))

    def consume(j, buf, use_mask):
        ks = pl.multiple_of(j * tk, tk)
        v = jnp.concatenate([v_ref[pl.ds(ks, tk), :], jnp.ones((tk, LANES), BF16)], axis=1)
        pv, alpha = [], []
        for h in range(2):
            t = bufs[buf][h]
            if use_mask:
                qpos = q0 + lax.broadcasted_iota(I32, (tq, tk), 0)
                kpos = j * tk + lax.broadcasted_iota(I32, (tq, tk), 1)
                if fox:
                    seen = kpos <= qpos
                else:
                    seen = kpos < jnp.minimum((qpos // CHUNK + 1) * CHUNK, s_valid)
                t = jnp.where(seen, t, NEG)
                mx = jnp.max(t, axis=1, keepdims=True)
            else:
                mx = mx_ref[buf, h]
            m_prev = m_ref[h]
            m_new = jnp.maximum(m_prev, mx)
            a = jnp.exp2(m_prev - m_new)
            p = jnp.exp2(t - jnp.concatenate([m_new] * (tk // LANES), axis=1))
            m_ref[h] = m_new
            both = jnp.dot(p.astype(BF16), v, preferred_element_type=F32)
            pv.append(both[:, :LANES])
            l_ref[h] = a * l_ref[h] + both[:, LANES:]
            alpha.append(a)
        if fox:
            acc_ref[0] = acc_ref[0] * jnp.where(low, alpha[0], alpha[1]) + jnp.where(low, pv[0], pv[1])
        else:
            for h in range(2):
                acc_ref[h] = acc_ref[h] * alpha[h] + pv[h]

    scores(0, 0)

    def pair(jj, c):
        j = 2 * jj
        scores(j + 1, 1)
        consume(j, 0, False)
        scores(j + 2, 0)
        consume(j + 1, 1, False)
        return c

    lax.fori_loop(0, nb // 2, pair, 0)

    @pl.when(nb % 2 == 1)
    def _():
        scores(nb, 1)
        consume(nb - 1, 0, False)
        consume(nb, 1, True)

    @pl.when(nb % 2 == 0)
    def _():
        consume(nb, 0, True)

    if fox:
        o = acc_ref[0] / jnp.where(low, l_ref[0], l_ref[1])
    else:
        lam_v = lam_ref[...]
        lam = (jnp.exp(jnp.sum(lam_v[0:1] * lam_v[1:2], axis=1, keepdims=True))
               - jnp.exp(jnp.sum(lam_v[2:3] * lam_v[3:4], axis=1, keepdims=True)) + lam_init)
        o = acc_ref[0] / l_ref[0] - lam * (acc_ref[1] / l_ref[1])
        o = _rms(o, g_ref[...]) * (1.0 - lam_init)
    o_ref[...] = o.astype(o_ref.dtype)


def _self_attention(q, k, v, extra, *, b, t, s_pad, s_valid, past, tq, tk, fox, lam_init):
    assert tk % tq == 0 and past % tq == 0 and t % tq == 0 and tk % CHUNK == 0 and s_pad % tk == 0
    assert tq % CHUNK == 0 or (CHUNK % tq == 0 and past % CHUNK == 0)
    n_q = t // tq
    k_spec = pl.BlockSpec((None, LANES, s_pad), lambda bi, p, i: (bi, p, 0))
    v_spec = pl.BlockSpec((None, s_pad, LANES), lambda bi, p, i: (bi, 0, p))
    q_spec = pl.BlockSpec((tq, LANES), lambda bi, p, i: (bi * n_q + i, p))
    const = lambda bi, p, i: (0, 0)
    if fox:
        in_specs = [q_spec, pl.BlockSpec((tq, LANES), lambda bi, p, i: (bi * n_q + i, 0)), k_spec,
                    pl.BlockSpec((None, LANES, s_pad), lambda bi, p, i: (bi, 0, 0)), v_spec]
        args = (q, extra[0], k, extra[1], v)
    else:
        in_specs = [q_spec, k_spec, v_spec, pl.BlockSpec((4, HEAD_DIM), const), pl.BlockSpec((1, LANES), const)]
        args = (q, k, v, extra[0], extra[1].reshape(1, LANES))
    stat = pltpu.VMEM((2, tq, LANES), F32)
    return pl.pallas_call(
        functools.partial(_attn_kernel, tq=tq, tk=tk, past=past, s_valid=s_valid, fox=fox, lam_init=lam_init),
        grid=(b, FOX_W // LANES, n_q),
        in_specs=in_specs,
        out_specs=q_spec,
        out_shape=jax.ShapeDtypeStruct((b * t, FOX_W), BF16),
        scratch_shapes=[pltpu.VMEM((2, tq, tk), F32), pltpu.VMEM((2, tq, tk), F32),
                        pltpu.VMEM((2, 2, tq, LANES), F32), stat, stat, stat],
        compiler_params=_params("arbitrary", "arbitrary", "arbitrary"),
    )(*args)


def _memkv_kernel(m_ref, g_ref, wk_ref, wv_ref, k_ref, v_ref, kb_ref, vb_ref):
    n = _rms(m_ref[...], g_ref[...]).astype(BF16)
    k = jnp.dot(n, wk_ref[...], preferred_element_type=F32)
    v = jnp.dot(n, wv_ref[...], preferred_element_type=F32)
    k_ref[...] = k
    v_ref[...] = v
    kb_ref[...] = k.astype(BF16)
    vb_ref[...] = v.astype(BF16)


def _memkv(mem2, g, wk, wv, tm):
    n, d = mem2.shape
    row = lambda i: (i, 0)
    const = lambda i: (0, 0)
    blk = pl.BlockSpec((tm, d), row)
    wspec = pl.BlockSpec((d, d), const)
    return pl.pallas_call(
        _memkv_kernel, grid=(n // tm,),
        in_specs=[blk, pl.BlockSpec((1, d), const), wspec, wspec],
        out_specs=[blk, blk, blk, blk],
        out_shape=[jax.ShapeDtypeStruct((n, d), F32)] * 2 + [jax.ShapeDtypeStruct((n, d), BF16)] * 2,
        compiler_params=_params("arbitrary"),
    )(mem2, g.reshape(1, d), wk, wv)


U32 = jnp.uint32
HALF_D = D_MODEL // 2
ROW_TILE = HALF_D // LANES
BF16_BITS = 0xFFFF0000


def _pack_rows(val):
    def rounded(x):
        bits = pltpu.bitcast(x, U32)
        return (bits + U32(0x7FFF) + ((bits >> 16) & U32(1))) & U32(BF16_BITS)
    return rounded(val[:, :HALF_D]) | (rounded(val[:, HALF_D:]) >> 16)


def _unpack_rows(words):
    return jnp.concatenate([pltpu.bitcast(words & U32(BF16_BITS), F32), pltpu.bitcast(words << 16, F32)], axis=1)


def _load_tile_rows(ref):
    rows = ref.shape[0] // ROW_TILE
    return jnp.concatenate([ref[pl.ds(s, rows, stride=ROW_TILE), :] for s in range(ROW_TILE)], axis=1)


def _store_tile_rows(ref, words):
    rows = words.shape[0]
    for s in range(ROW_TILE):
        ref[pl.ds(s, rows, stride=ROW_TILE), :] = words[:, s * LANES:(s + 1) * LANES]


def _memattn_kernel(x_ref, fo_ref, do_ref, wof_ref, wod_ref, gq_ref, wq_ref, k_ref, v_ref, wo_ref, g_ref,
                    wr_ref, br_ref, h2_ref, xn_ref, logit_ref, *, q_scale):
    h1 = (x_ref[...] + jnp.dot(fo_ref[...], wof_ref[...], preferred_element_type=F32)
          + jnp.dot(do_ref[...], wod_ref[...], preferred_element_type=F32))
    nq = _rms(h1, gq_ref[...]).astype(BF16)
    q = (jnp.dot(nq, wq_ref[...], preferred_element_type=F32) * q_scale).astype(BF16)
    outs = []
    for h in range(MEM_HEADS):
        sl = slice(h * MEM_HEAD_DIM, (h + 1) * MEM_HEAD_DIM)
        s = lax.dot_general(q[:, sl], k_ref[:, sl], (((1,), (1,)), ((), ())), preferred_element_type=F32)
        p = jnp.exp2(s - jnp.max(s, axis=1, keepdims=True))
        o = jnp.dot(p.astype(BF16), v_ref[:, sl], preferred_element_type=F32)
        outs.append((o / jnp.sum(p, axis=1, keepdims=True)).astype(BF16))
    h2 = h1 + jnp.dot(jnp.concatenate(outs, axis=1), wo_ref[...], preferred_element_type=F32)
    h2_ref[...] = h2
    xn_f32 = _rms(h2, g_ref[...])
    words = _pack_rows(xn_f32)
    for s in range(ROW_TILE):
        xn_ref[s] = words[:, s * LANES:(s + 1) * LANES]
    logit_ref[...] = jnp.dot(xn_f32.astype(BF16), wr_ref[...], preferred_element_type=F32) + br_ref[...]


def _memattn(x2, fox_o, diff_o, wo_f, wo_d, g_mem_q, w_mq, mk, mv, w_mo, g_ffn, w_router, b_router,
             *, b0, b, t, tq):
    d = x2.shape[1]
    n = b * t
    n_q = t // tq
    row = lambda bi, i: (bi * n_q + i, 0)
    src_row = lambda bi, i: ((b0 + bi) * n_q + i, 0)
    const = lambda bi, i: (0, 0)
    mem = pl.BlockSpec((None, MEM_LEN, d), lambda bi, i: (b0 + bi, 0, 0))
    src = pl.BlockSpec((tq, d), src_row)
    half = pl.BlockSpec((tq, FOX_W), src_row)
    blk = pl.BlockSpec((tq, d), row)
    square = pl.BlockSpec((d, d), const)
    vec = pl.BlockSpec((1, d), const)
    planes = pl.BlockSpec((ROW_TILE, tq, LANES), lambda bi, i: (0, bi * n_q + i, 0))
    return pl.pallas_call(
        functools.partial(_memattn_kernel, q_scale=MEM_HEAD_DIM ** -0.5 * LOG2E),
        grid=(b, n_q),
        in_specs=[src, half, half, pl.BlockSpec((FOX_W, d), const), pl.BlockSpec((DIFF_W, d), const), vec,
                  square, mem, mem, square, vec,
                  pl.BlockSpec((d, LANES), const), pl.BlockSpec((1, LANES), const)],
        out_specs=[blk, planes, pl.BlockSpec((tq, LANES), row)],
        out_shape=[jax.ShapeDtypeStruct((n, d), F32), jax.ShapeDtypeStruct((ROW_TILE, n, LANES), U32),
                   jax.ShapeDtypeStruct((n, LANES), F32)],
        compiler_params=_params("arbitrary", "arbitrary"),
    )(x2, fox_o, diff_o, wo_f, wo_d, g_mem_q.reshape(1, d), w_mq, mk, mv, w_mo, g_ffn.reshape(1, d),
      w_router, b_router)


ROUTER_ROWS = 48


def _first_slot(hit, slot_f32):
    return jnp.min(jnp.where(hit, slot_f32, float(LANES)), axis=0, keepdims=True).astype(I32)


def _router_kernel(lt_ref, ri_ref, rg_ref, cnt_ref, carry_ref, *, rt):
    @pl.when(pl.program_id(0) == 0)
    def _():
        carry_ref[...] = jnp.zeros(carry_ref.shape, F32)

    lt = lt_ref[0:ROUTER_ROWS, :]
    slot = lax.broadcasted_iota(I32, (ROUTER_ROWS, rt), 0)
    slot_f = slot.astype(F32)
    g_log = jnp.where((slot >= GROUP_LANE0) & (slot < GROUP_LANE0 + N_GROUPS), lt, NEG)
    g_max = jnp.max(g_log, axis=0, keepdims=True)
    g_idx = _first_slot(g_log == g_max, slot_f) - GROUP_LANE0
    g_p = 1.0 / jnp.sum(jnp.exp(g_log - g_max), axis=0, keepdims=True)
    in_group = (slot >= g_idx * EXPERTS_PER_GROUP) & (slot < (g_idx + 1) * EXPERTS_PER_GROUP)
    e_log = jnp.where(in_group, lt, NEG)
    e_exp = jnp.exp(e_log - jnp.max(e_log, axis=0, keepdims=True))
    prob = jnp.where(in_group, e_exp / jnp.sum(e_exp, axis=0, keepdims=True), -1.0)
    p1 = jnp.max(prob, axis=0, keepdims=True)
    i1 = _first_slot(prob == p1, slot_f)
    rest = jnp.where(slot == i1, -1.0, prob)
    p2 = jnp.max(rest, axis=0, keepdims=True)
    i2 = _first_slot(rest == p2, slot_f)
    top_sum = p1 + p2

    hot1, hot2 = slot == i1, slot == i2
    onehot = jnp.where(hot1 | hot2, 1.0, 0.0)
    earlier = lax.broadcasted_iota(I32, (rt, rt), 0) < lax.broadcasted_iota(I32, (rt, rt), 1)
    before = jnp.dot(onehot.astype(BF16), jnp.where(earlier, 1.0, 0.0).astype(BF16),
                     preferred_element_type=F32) + carry_ref[0:ROUTER_ROWS, 0:1]
    r1 = jnp.sum(jnp.where(hot1, before, 0.0), axis=0, keepdims=True)
    r2 = jnp.sum(jnp.where(hot2, before, 0.0), axis=0, keepdims=True)
    carry_ref[0:ROUTER_ROWS, :] = carry_ref[0:ROUTER_ROWS, :] + jnp.sum(onehot, axis=1, keepdims=True)
    cnt_ref[...] = carry_ref[...]

    ri_ref[...] = jnp.concatenate([i1, i2, r1.astype(I32), r2.astype(I32), jnp.zeros((4, rt), I32)], axis=0)
    rg_ref[...] = jnp.concatenate([g_p * p1 / top_sum, g_p * p2 / top_sum, jnp.zeros((6, rt), F32)], axis=0)


def _router(logits_t, rt):
    n = logits_t.shape[1]
    col = lambda i: (0, i)
    return pl.pallas_call(
        functools.partial(_router_kernel, rt=rt),
        grid=(n // rt,),
        in_specs=[pl.BlockSpec((LANES, rt), col)],
        out_specs=[pl.BlockSpec((8, rt), col), pl.BlockSpec((8, rt), col),
                   pl.BlockSpec((LANES, LANES), lambda i: (0, 0))],
        out_shape=[jax.ShapeDtypeStruct((8, n), I32), jax.ShapeDtypeStruct((8, n), F32),
                   jax.ShapeDtypeStruct((LANES, LANES), F32)],
        scratch_shapes=[pltpu.VMEM((LANES, LANES), F32)],
        compiler_params=_params("arbitrary"),
    )(logits_t)


SC_CORES, SC_SUBCORES = 2, 16
SC_WINDOW = 128


def _sc_mesh():
    return plsc.VectorSubcoreMesh(core_axis_name="core", subcore_axis_name="subcore")


def _sc_split(n_windows):
    if n_windows % (SC_CORES * SC_SUBCORES) == 0:
        return ("core", "subcore")
    assert n_windows % SC_SUBCORES == 0
    return "subcore"


def _sc_repeat(idx):
    n_windows = idx.shape[0] // SC_WINDOW
    reps = SC_SUBCORES // math.gcd(n_windows, SC_SUBCORES)
    return jnp.tile(idx, reps)


def _sc_scatter_rows(src, idx, n_out):
    rows = src.shape[0]
    win = SC_WINDOW
    n_src_blocks = rows // win
    idx = _sc_repeat(idx)
    n_idx = idx.shape[0]

    @functools.partial(pl.kernel, out_type=jax.ShapeDtypeStruct((n_out, LANES), src.dtype),
                       mesh=_sc_mesh(), scratch_types=[])
    def scatter(x_hbm, i_hbm, o_hbm):
        def body(x_vmem, i_vmem):
            pltpu.sync_copy(x_vmem, o_hbm.at[i_vmem.at[0]])

        pltpu.emit_pipeline(
            body, grid=(n_idx // win,),
            in_specs=[pl.BlockSpec((win, LANES), lambda i: (i % n_src_blocks, 0)),
                      pl.BlockSpec((1, win), lambda i: (0, i))],
            out_specs=[],
            core_axis_name=_sc_split(n_idx // win), dimension_semantics=(pltpu.PARALLEL,),
        )(x_hbm, i_hbm)

    return scatter(src, idx.reshape(1, n_idx))


def _sc_gather_rows(src, idx):
    n_rows = idx.shape[0]
    idx = _sc_repeat(idx)
    n_out = idx.shape[0]
    win = SC_WINDOW

    @functools.partial(pl.kernel, out_type=jax.ShapeDtypeStruct((n_out, LANES), src.dtype),
                       mesh=_sc_mesh(), scratch_types=[])
    def gather(x_hbm, i_hbm, o_hbm):
        def body(i_vmem, o_vmem):
            pltpu.sync_copy(x_hbm.at[i_vmem.at[0]], o_vmem)

        pltpu.emit_pipeline(
            body, grid=(n_out // win,),
            in_specs=[pl.BlockSpec((1, win), lambda i: (0, i))],
            out_specs=[pl.BlockSpec((win, LANES), lambda i: (i, 0))],
            core_axis_name=_sc_split(n_out // win), dimension_semantics=(pltpu.PARALLEL,),
        )(i_hbm, o_hbm)

    return gather(src, idx.reshape(1, n_out))[:n_rows]


def _expert_kernel(blk_ref, e_ref, lo_ref, hi_ref, x_ref, w1_ref, w3_ref, w2_ref, y_ref):
    w = pl.program_id(0)
    lo, hi = lo_ref[w], hi_ref[w]
    row0 = blk_ref[w] * (x_ref.shape[0] // ROW_TILE)

    def compute():
        x = _unpack_rows(_load_tile_rows(x_ref)).astype(BF16)
        a = jnp.dot(x, w1_ref[...], preferred_element_type=F32)
        g = jnp.dot(x, w3_ref[...], preferred_element_type=F32)
        mid = ((a / (1.0 + jnp.exp(-a))) * g).astype(BF16)
        y = _pack_rows(jnp.dot(mid, w2_ref[...], preferred_element_type=F32))
        rows = row0 + lax.broadcasted_iota(I32, y.shape, 0)
        return y, (rows >= lo) & (rows < hi)

    @pl.when((hi > lo) & (lo == row0))
    def _():
        y, mine = compute()
        _store_tile_rows(y_ref, jnp.where(mine, y, jnp.zeros_like(y)))

    @pl.when((hi > lo) & (lo != row0))
    def _():
        y, mine = compute()
        _store_tile_rows(y_ref, jnp.where(mine, y, _load_tile_rows(y_ref)))


def _experts(items, xs, w1, w3, w2, rows):
    d = D_MODEL
    n_items = items[0].shape[0]
    xmap = lambda w, blk, e, lo, hi: (blk[w], 0)
    wmap = lambda w, blk, e, lo, hi: (e[w], 0, 0)
    return pl.pallas_call(
        _expert_kernel,
        grid_spec=pltpu.PrefetchScalarGridSpec(
            num_scalar_prefetch=4, grid=(n_items,),
            in_specs=[pl.BlockSpec((rows * ROW_TILE, LANES), xmap),
                      pl.BlockSpec((None, d, D_EXPERT), wmap), pl.BlockSpec((None, d, D_EXPERT), wmap),
                      pl.BlockSpec((None, D_EXPERT, d), wmap)],
            out_specs=pl.BlockSpec((rows * ROW_TILE, LANES), xmap)),
        out_shape=jax.ShapeDtypeStruct(xs.shape, U32),
        compiler_params=_params("arbitrary"),
    )(*items, xs, w1, w3, w2)


def _work_items(counts, n_rows, rows):
    n_blocks = n_rows // rows
    n_items = n_blocks + N_EXPERTS - 1
    ends = jnp.cumsum(counts)
    starts = ends - counts
    first_blk = starts // rows
    n_blk = jnp.where(counts > 0, (ends - 1) // rows - first_blk + 1, 0)
    item_end = jnp.cumsum(n_blk)
    total = item_end[-1]
    w = jnp.arange(n_items, dtype=I32)
    wc = jnp.minimum(w, total - 1)
    e = jnp.sum(item_end[None, :] <= wc[:, None], axis=1).astype(I32)
    mine = e[:, None] == jnp.arange(N_EXPERTS, dtype=I32)[None, :]

    def of_expert(per_expert):
        return jnp.sum(jnp.where(mine, per_expert[None, :], 0), axis=1)

    blk = (of_expert(first_blk) + wc - (of_expert(item_end) - of_expert(n_blk))).astype(I32)
    lo = jnp.maximum(of_expert(starts), blk * rows).astype(I32)
    hi = jnp.minimum(of_expert(ends), (blk + 1) * rows).astype(I32)
    live = w < total
    return blk, e, jnp.where(live, lo, 0), jnp.where(live, hi, 0), starts


def _combine_kernel(h_ref, z0_ref, z1_ref, rg_ref, g_ref, *rest):
    y_ref = rest[-1]
    rg = rg_ref[...]
    def rows_of(z_ref):
        return _unpack_rows(jnp.concatenate([z_ref[s] for s in range(ROW_TILE)], axis=1))

    moe = rows_of(z0_ref) * rg[:, 0:1] + rows_of(z1_ref) * rg[:, 1:2]
    y_ref[...] = _rms(h_ref[...] + moe, g_ref[...])


def _combine(h2, z, rg, g_final, tm, y_prev, row0, n_total):
    n, d = h2.shape
    n_t = n // tm
    row = lambda i: (i, 0)
    in_specs = [pl.BlockSpec((tm, d), row), pl.BlockSpec((ROW_TILE, tm, LANES), lambda i: (0, i, 0)),
                pl.BlockSpec((ROW_TILE, tm, LANES), lambda i: (1, i, 0)), pl.BlockSpec((tm, 8), row),
                pl.BlockSpec((1, d), lambda i: (0, 0))]
    args = [h2, z, z, rg, g_final.reshape(1, d)]
    aliases = {}
    if y_prev is not None:
        in_specs.append(pl.BlockSpec(memory_space=pl.ANY))
        args.append(y_prev)
        aliases = {len(args) - 1: 0}
    return pl.pallas_call(
        _combine_kernel, grid=(n_t,),
        in_specs=in_specs,
        out_specs=pl.BlockSpec((tm, d), lambda i: (row0 // tm + i, 0)),
        out_shape=jax.ShapeDtypeStruct((n_total, d), F32),
        input_output_aliases=aliases,
        compiler_params=_params("arbitrary"),
    )(*args)


def _moe_and_final(h2, xn, ri, rg, counts, w1, w3, w2, g_final, *, tm, y_prev, row0, n_total):
    n = h2.shape[0]
    rows = min(MOE_ROWS, 2 * n)
    blk, e, lo, hi, starts = _work_items(counts, 2 * n, rows)
    expert = jnp.arange(N_EXPERTS, dtype=I32)[:, None, None]
    first = jnp.sum(jnp.where(ri[None, 0:2] == expert, starts.astype(I32)[:, None, None], 0), axis=0)
    pos = (first + ri[2:4]).astype(I32)
    sub = (pos[:, None, :] * ROW_TILE + jnp.arange(ROW_TILE, dtype=I32)[None, :, None]).reshape(2 * n * ROW_TILE)
    xs = _sc_scatter_rows(xn.reshape(ROW_TILE * n, LANES), sub, 2 * n * ROW_TILE)
    ys = _experts((blk, e, lo, hi), xs, w1, w3, w2, rows)
    z = _sc_gather_rows(ys, sub).reshape(2 * ROW_TILE, n, LANES)
    return _combine(h2, z, rg.T, g_final, tm, y_prev, row0, n_total)


def _pad_time(a, s_pad):
    return jnp.pad(a, ((0, 0), (0, s_pad - a.shape[1])) + ((0, 0),) * (a.ndim - 2))


def _layer(x, past, mem_k, mem_v, wts, lam_init, g_final, *, tm, tq, tk, moe_parts):
    b, t, d = x.shape
    n = b * t
    p_len = 0 if past is None else past[0].shape[1]
    s_valid = p_len + t
    s_pad = -(-s_valid // tk) * tk
    pos = p_len + jnp.arange(t)

    fk, fv, logf, dk, dv, qf, kf, vf, qd, kd, vd = _inproj(
        x, wts['g_mix'], wts['w_main'], wts['w_ff'], wts['b_forget'], pos, tm)
    time_major = tm <= t

    def leaf(a, dims):
        if time_major:
            return jnp.moveaxis(a.reshape((b,) + dims + (t,)), -1, 1)
        return a.reshape((b, t) + dims)

    new = (leaf(fk, (FOX_HEADS, HEAD_DIM)), leaf(fv, (FOX_HEADS, HEAD_DIM)), leaf(logf, (FOX_HEADS,)),
           leaf(dk, (DIFF_HEADS, 2, HEAD_DIM)), dv.reshape(b, t, DIFF_HEADS, 2 * HEAD_DIM))

    def keys_t(cur, old_rows):
        if not time_major:
            cur = jnp.swapaxes(cur.reshape(b, t, FOX_W), 1, 2)
        if old_rows is not None:
            prev = jnp.swapaxes(old_rows.reshape(b, p_len, FOX_W).astype(BF16), 1, 2)
            cur = jnp.concatenate([prev, cur], axis=2)
        return jnp.pad(cur, ((0, 0), (0, 0), (0, s_pad - s_valid)))

    def with_past(new_b16, old_rows):
        cur = new_b16.reshape(b, t, FOX_W)
        if old_rows is not None:
            cur = jnp.concatenate([old_rows.reshape(b, p_len, FOX_W).astype(BF16), cur], axis=1)
        return _pad_time(cur, s_pad)

    logf_t = logf if time_major else jnp.swapaxes(logf.reshape(b, t, FOX_HEADS), 1, 2)
    if past is not None:
        logf_t = jnp.concatenate([jnp.swapaxes(past[2].astype(F32), 1, 2), logf_t], axis=2)
    k_bias, q_bias_t = _fox_bias(jnp.pad(logf_t, ((0, 0), (0, 0), (0, s_pad - s_valid))))
    q_bias = jnp.swapaxes(q_bias_t[:, :, p_len:p_len + t], 1, 2).reshape(n, LANES)

    old = (None,) * 5 if past is None else past
    geom = dict(b=b, t=t, s_pad=s_pad, s_valid=s_valid, past=p_len, tq=tq, tk=tk, lam_init=lam_init)
    fox_o = _self_attention(qf, keys_t(kf, old[0]), with_past(vf, old[1]), (q_bias, k_bias),
                            fox=True, **geom)
    diff_o = _self_attention(qd, keys_t(kd, old[3]), with_past(vd, old[4]),
                             (wts['lam_vecs'], wts['g_diff']), fox=False, **geom)
    assert b % moe_parts == 0
    bp = b // moe_parts
    y = None
    for part in range(moe_parts):
        h2, xn, logits = _memattn(x.reshape(n, d), fox_o, diff_o, wts['wo_f'], wts['wo_d'], wts['g_mem_q'],
                                  wts['w_mq'], mem_k, mem_v, wts['w_mo'], wts['g_ffn'], wts['w_router'],
                                  wts['b_router'], b0=part * bp, b=bp, t=t, tq=tq)
        ri, rg, cnt = _router(logits.T, min(tm, bp * t))
        counts = cnt[:N_EXPERTS, 0].astype(I32)
        y = _moe_and_final(h2, xn, ri, rg, counts, wts['w1'], wts['w3'], wts['w2'], g_final, tm=tm,
                           y_prev=y, row0=part * bp * t, n_total=n)
    return y.reshape(b, t, d), new


def kernel(x_prompt, x_sample, cache_fox_k, cache_fox_v, cache_fox_logf, cache_diff_k, cache_diff_v, cache_mem_k, cache_mem_v, mem_prompt, g_mix, w_in, b_forget, lam_q1, lam_k1, lam_q2, lam_k2, g_diff, w_o, g_mem_q, g_mem_kv, w_mq, w_mk, w_mv, w_mo, g_ffn, w_group, b_group, w_erouter, b_erouter, w1, w3, w2, g_final):
    depth = w_in.shape[0]
    assert depth == 1
    l = 0
    lam_init = 0.8 - 0.6 * math.exp(-0.3 * l)
    d = D_MODEL
    w = w_in[l]
    ff0 = 3 * FOX_W
    w_router = jnp.zeros((d, LANES), F32).at[:, :N_EXPERTS].set(w_erouter[l])
    w_router = w_router.at[:, GROUP_LANE0:GROUP_LANE0 + N_GROUPS].set(w_group[l])
    b_router = jnp.zeros((1, LANES), F32).at[0, :N_EXPERTS].set(b_erouter[l])
    b_router = b_router.at[0, GROUP_LANE0:GROUP_LANE0 + N_GROUPS].set(b_group[l])
    wts = {
        'g_mix': g_mix[l],
        'w_main': jnp.concatenate([w[:, :ff0], w[:, ff0 + FOX_HEADS:]], axis=1).astype(BF16),
        'w_ff': jnp.pad(w[:, ff0:ff0 + FOX_HEADS], ((0, 0), (0, LANES - FOX_HEADS))).astype(BF16),
        'b_forget': b_forget[l],
        'lam_vecs': jnp.stack([lam_q1[l], lam_k1[l], lam_q2[l], lam_k2[l]]),
        'g_diff': g_diff[l],
        'wo_f': w_o[l][:FOX_W].astype(BF16), 'wo_d': w_o[l][FOX_W:].astype(BF16),
        'g_mem_q': g_mem_q[l], 'w_mq': w_mq[l].astype(BF16), 'w_mo': w_mo[l].astype(BF16),
        'g_ffn': g_ffn[l], 'w_router': w_router.astype(BF16), 'b_router': b_router,
        'w1': w1[l].astype(BF16), 'w3': w3[l].astype(BF16), 'w2': w2[l].astype(BF16),
    }

    bp, tp, _ = x_prompt.shape
    mk, mv, mk_b, mv_b = _memkv(mem_prompt.reshape(bp * MEM_LEN, d), g_mem_kv[l],
                                w_mk[l].astype(BF16), w_mv[l].astype(BF16), TOKEN_TILE)
    yp, new_p = _layer(x_prompt, None, mk_b.reshape(bp, MEM_LEN, d), mv_b.reshape(bp, MEM_LEN, d),
                       wts, lam_init, g_final, tm=TOKEN_TILE, tq=TOKEN_TILE, tk=KEY_TILE,
                       moe_parts=PROMPT_MOE_PARTS)

    bs, ts, _ = x_sample.shape
    past = (cache_fox_k[l], cache_fox_v[l], cache_fox_logf[l], cache_diff_k[l], cache_diff_v[l])
    ys, new_s = _layer(x_sample, past, cache_mem_k[l].reshape(bs, MEM_LEN, d).astype(BF16),
                       cache_mem_v[l].reshape(bs, MEM_LEN, d).astype(BF16),
                       wts, lam_init, g_final, tm=bs * ts, tq=ts, tk=KEY_TILE, moe_parts=1)

    mem_shape = (1, bp, MEM_LEN, MEM_HEADS, MEM_HEAD_DIM)
    return (yp, ys) + tuple(a[None] for a in new_p) + (mk.reshape(mem_shape), mv.reshape(mem_shape)) \
        + tuple(a[None] for a in new_s)
```
